```python
import math
import jax, jax.numpy as jnp
from jax import lax
import numpy as np

D_MODEL = 1024
BATCH = 8
SEQ = 2048
DEPTH = 4
DEC_BATCH = 16
DEC_SEQ = 32
PAST_LEN = 4096

CHUNK = 64
N_A_LAYERS = DEPTH // 2
N_B_LAYERS = DEPTH - N_A_LAYERS
A_HEADS = 8
A_DK = 128
A_DV = 128
A_QK = A_HEADS * A_DK
A_VW = A_HEADS * A_DV
CONV_W = 4
CONV_DIM = 2 * A_QK + A_VW
A_IN = CONV_DIM + A_VW + 2 * A_HEADS
B_Q_HEADS = 16
B_KV_HEADS = 4
B_GROUP = B_Q_HEADS // B_KV_HEADS
B_HD = 64
B_QW = B_Q_HEADS * B_HD
B_KVW = B_KV_HEADS * B_HD
B_IN = 2 * B_QW
WINDOW = 128
WINDOW_CHUNKS = WINDOW // CHUNK
ROPE_DIMS = B_HD // 4
ROPE_THETA = 500000.0
DN_ALPHA = (2 * DEPTH) ** 0.25
DN_BETA = (8 * DEPTH) ** -0.25
LN_EPS = 1e-5
RMS_EPS = 1e-6

kernel_name = 'yoco_gdn_swa_sink_stream_step'


def layer_norm(x, g, b):
    xf = x.astype(jnp.float32)
    mu = xf.mean(-1, keepdims=True)
    var = jnp.square(xf - mu).mean(-1, keepdims=True)
    return ((xf - mu) * lax.rsqrt(var + LN_EPS) * g + b).astype(x.dtype)


def l2norm(x):
    xf = x.astype(jnp.float32)
    return xf * lax.rsqrt(jnp.sum(xf * xf, -1, keepdims=True) + RMS_EPS)


def rope_partial(x, pos):
    half = ROPE_DIMS // 2
    inv = ROPE_THETA ** (-jnp.arange(half, dtype=jnp.float32) * 2.0 / ROPE_DIMS)
    ang = pos.astype(jnp.float32)[:, None] * inv[None, :]
    cos = jnp.cos(ang)[None, :, None, :]
    sin = jnp.sin(ang)[None, :, None, :]
    xf = x.astype(jnp.float32)
    x1, x2 = xf[..., :half], xf[..., half:ROPE_DIMS]
    rot = jnp.concatenate([x1 * cos - x2 * sin, x2 * cos + x1 * sin], -1)
    return jnp.concatenate([rot, xf[..., ROPE_DIMS:]], -1).astype(x.dtype)


def causal_conv(x, buf, w):
    l = x.shape[1]
    xp = jnp.concatenate([buf.astype(x.dtype), x], axis=1)
    y = xp[:, 0:l] * w[0]
    for j in range(1, CONV_W):
        y = y + xp[:, j:j + l] * w[j]
    return jax.nn.silu(y), xp[:, -(CONV_W - 1):]


def gated_delta_rule(q, k, v, g, beta, s0):
    b, l, h, _ = q.shape
    c = min(CHUNK, l)
    n = l // c

    def blocks(t):
        t = t.reshape((b, n, c) + t.shape[2:])
        return jnp.swapaxes(jnp.moveaxis(t, 1, 0), 2, 3)

    qc, kc, vc, gb, bc = blocks(q), blocks(k), blocks(v), blocks(g), blocks(beta)
    gcum = jnp.cumsum(gb, axis=-1)
    incl = jnp.tril(jnp.ones((c, c), dtype=bool))
    strict = jnp.tril(jnp.ones((c, c), dtype=bool), -1)
    decay = jnp.exp(jnp.where(incl, gcum[..., :, None] - gcum[..., None, :], -jnp.inf))
    kbeta = kc * bc[..., None]
    lower = jnp.where(strict, jnp.einsum('nbhik,nbhjk->nbhij', kbeta, kc) * decay, 0.0)
    tmat = lower + jnp.eye(c, dtype=lower.dtype)
    u = lax.linalg.triangular_solve(tmat, vc * bc[..., None], left_side=True, lower=True, unit_diagonal=True)
    w = lax.linalg.triangular_solve(tmat, kbeta * jnp.exp(gcum)[..., None], left_side=True, lower=True,
                                    unit_diagonal=True)
    intra = jnp.einsum('nbhik,nbhjk->nbhij', qc, kc) * decay
    qdec = qc * jnp.exp(gcum)[..., None]
    kdec = kc * jnp.exp(gcum[..., -1:] - gcum)[..., None]
    glast = jnp.exp(gcum[..., -1])

    def step(s, xs):
        qd, kd, ui, wi, ai, gl = xs
        v_new = ui - jnp.einsum('bhck,bhkv->bhcv', wi, s)
        o = jnp.einsum('bhck,bhkv->bhcv', qd, s) + jnp.einsum('bhij,bhjv->bhiv', ai, v_new)
        s = s * gl[..., None, None] + jnp.einsum('bhck,bhcv->bhkv', kd, v_new)
        return s, o

    s_fin, o = lax.scan(step, s0, (qdec, kdec, u, w, intra, glast))
    o = jnp.moveaxis(jnp.swapaxes(o, 2, 3), 0, 1).reshape(b, l, h, o.shape[-1])
    return o, s_fin


def delta_layer(x, conv_buf, s0, w_in, conv_w, a_log, dt_bias, norm_w, w_out, ln_g, ln_b):
    b, l, _ = x.shape
    proj = x @ w_in
    qkv, new_buf = causal_conv(proj[..., :CONV_DIM], conv_buf, conv_w)
    z = proj[..., CONV_DIM:CONV_DIM + A_VW].reshape(b, l, A_HEADS, A_DV)
    a_in = proj[..., CONV_DIM + A_VW:CONV_DIM + A_VW + A_HEADS].astype(jnp.float32)
    b_in = proj[..., CONV_DIM + A_VW + A_HEADS:].astype(jnp.float32)
    q = l2norm(qkv[..., :A_QK].reshape(b, l, A_HEADS, A_DK)) * (A_DK ** -0.5)
    k = l2norm(qkv[..., A_QK:2 * A_QK].reshape(b, l, A_HEADS, A_DK))
    v = qkv[..., 2 * A_QK:].reshape(b, l, A_HEADS, A_DV).astype(jnp.float32)
    beta = jax.nn.sigmoid(b_in)
    g = -jnp.exp(a_log.astype(jnp.float32)) * jax.nn.softplus(a_in + dt_bias.astype(jnp.float32))
    o, s_new = gated_delta_rule(q, k, v, g, beta, s0.astype(jnp.float32))
    o = o * lax.rsqrt(jnp.mean(o * o, -1, keepdims=True) + RMS_EPS) * norm_w * jax.nn.silu(z.astype(jnp.float32))
    out = o.reshape(b, l, A_VW).astype(x.dtype) @ w_out
    return layer_norm(DN_ALPHA * x + out, ln_g, ln_b), new_buf, s_new.astype(x.dtype)


def shared_kv(h, w_kv, pos):
    b, l, _ = h.shape
    kv = h @ w_kv
    k = rope_partial(kv[..., :B_KVW].reshape(b, l, B_KV_HEADS, B_HD), pos)
    v = kv[..., B_KVW:].reshape(b, l, B_KV_HEADS, B_HD)
    return k, v


def sink_softmax(s, sinks):
    sk = sinks.astype(jnp.float32)[..., None, None]
    m = jnp.maximum(s.max(-1, keepdims=True), sk)
    p = jnp.exp(s - m)
    return p / (p.sum(-1, keepdims=True) + jnp.exp(sk - m))


def banded_attention(q, k, v, sinks):
    b, l = q.shape[:2]
    n = l // CHUNK
    qb = q.reshape(b, n, CHUNK, B_KV_HEADS, B_GROUP, B_HD)

    def band(t):
        t = t.reshape(b, n, CHUNK, B_KV_HEADS, B_HD)
        t = jnp.pad(t, ((0, 0), (WINDOW_CHUNKS, 0), (0, 0), (0, 0), (0, 0)))
        return jnp.concatenate([t[:, j:j + n] for j in range(WINDOW_CHUNKS + 1)], axis=2)

    kb, vb = band(k), band(v)
    s = jnp.einsum('bnqkgd,bnskd->bnkgqs', qb, kb, preferred_element_type=jnp.float32)
    key_chunk = jnp.arange(n)[:, None] + (jnp.arange((WINDOW_CHUNKS + 1) * CHUNK) // CHUNK)[None, :] - WINDOW_CHUNKS
    s = jnp.where((key_chunk >= 0)[None, :, None, None, None, :], s, -jnp.inf)
    p = sink_softmax(s, sinks)
    o = jnp.einsum('bnkgqs,bnskd->bnqkgd', p.astype(v.dtype), vb)
    return o.reshape(b, l, B_QW)


def cached_attention(q, k, v, sinks):
    b, l = q.shape[:2]
    s = jnp.einsum('bqkgd,bskd->bkgqs', q, k, preferred_element_type=jnp.float32)
    p = sink_softmax(s, sinks)
    o = jnp.einsum('bkgqs,bskd->bqkgd', p.astype(v.dtype), v)
    return o.reshape(b, l, B_QW)


def swa_layer(x, k, v, pos, cached, w_in, sinks, w_out, ln_g, ln_b):
    b, l, _ = x.shape
    proj = x @ w_in
    q = rope_partial(proj[..., :B_QW].reshape(b, l, B_Q_HEADS, B_HD), pos) * (B_HD ** -0.5)
    q = q.reshape(b, l, B_KV_HEADS, B_GROUP, B_HD)
    z = proj[..., B_QW:]
    sk = sinks.reshape(B_KV_HEADS, B_GROUP)
    o = cached_attention(q, k, v, sk) if cached else banded_attention(q, k, v, sk)
    out = (o * jax.nn.silu(z)) @ w_out
    return layer_norm(DN_ALPHA * x + out, ln_g, ln_b)


def trunk(x, pos, conv_state, delta_state, past_k, past_v, a_w_in, a_conv_w, a_log, a_dt_bias, a_norm_w, a_w_out,
          a_ln_g, a_ln_b, b_w_kv, b_w_in, b_sinks, b_w_out, b_ln_g, b_ln_b):
    bsz = x.shape[0]
    cached = past_k is not None
    new_conv, new_delta = [], []
    for layer in range(DEPTH):
        if layer < N_A_LAYERS:
            i = layer
            if cached:
                cbuf, s0 = conv_state[i], delta_state[i]
            else:
                cbuf = jnp.zeros((bsz, CONV_W - 1, CONV_DIM), x.dtype)
                s0 = jnp.zeros((bsz, A_HEADS, A_DK, A_DV), jnp.float32)
            x, cb, st = delta_layer(x, cbuf, s0, a_w_in[i], a_conv_w[i], a_log[i], a_dt_bias[i], a_norm_w[i],
                                    a_w_out[i], a_ln_g[i], a_ln_b[i])
            new_conv.append(cb)
            new_delta.append(st)
            if layer == N_A_LAYERS - 1:
                k, v = shared_kv(x, b_w_kv, pos)
                if cached:
                    k = jnp.concatenate([past_k.astype(k.dtype), k], axis=1)
                    v = jnp.concatenate([past_v.astype(v.dtype), v], axis=1)
                new_k, new_v = k[:, -WINDOW:], v[:, -WINDOW:]
        else:
            j = layer - N_A_LAYERS
            x = swa_layer(x, k, v, pos, cached, b_w_in[j], b_sinks[j], b_w_out[j], b_ln_g[j], b_ln_b[j])
    return x, jnp.stack(new_conv), jnp.stack(new_delta), new_k, new_v


def setup_inputs(seed: int = 0) -> dict:
    key = jax.random.key(seed)
    ks = jax.random.split(key, 24)

    def nrm(k, shape, scale):
        return jax.random.normal(k, shape, jnp.float32) * scale

    cache_rows = min(WINDOW, PAST_LEN)
    dt = jnp.exp(jax.random.uniform(ks[9], (N_A_LAYERS, A_HEADS), jnp.float32,
                                    minval=math.log(1e-3), maxval=math.log(1e-1)))
    return {
        'x_prompt': nrm(ks[0], (BATCH, SEQ, D_MODEL), 1.0),
        'x_sample': nrm(ks[1], (DEC_BATCH, DEC_SEQ, D_MODEL), 1.0),
        'state_delta': nrm(ks[2], (N_A_LAYERS, DEC_BATCH, A_HEADS, A_DK, A_DV), A_DK ** -0.5),
        'state_conv': nrm(ks[3], (N_A_LAYERS, DEC_BATCH, CONV_W - 1, CONV_DIM), 1.0),
        'cache_k': nrm(ks[4], (DEC_BATCH, cache_rows, B_KV_HEADS, B_HD), 1.0),
        'cache_v': nrm(ks[5], (DEC_BATCH, cache_rows, B_KV_HEADS, B_HD), 1.0),
        'a_w_in': nrm(ks[6], (N_A_LAYERS, D_MODEL, A_IN), D_MODEL ** -0.5),
        'a_conv_w': nrm(ks[7], (N_A_LAYERS, CONV_W, CONV_DIM), CONV_W ** -0.5),
        'a_log': jnp.log(jax.random.uniform(ks[8], (N_A_LAYERS, A_HEADS), jnp.float32, minval=1.0, maxval=16.0)),
        'a_dt_bias': dt + jnp.log(-jnp.expm1(-dt)),
        'a_norm_w': 1.0 + nrm(ks[10], (N_A_LAYERS, A_DV), 0.02),
        'a_w_out': nrm(ks[11], (N_A_LAYERS, A_VW, D_MODEL), (A_VW ** -0.5) * DN_BETA),
        'a_ln_g': 1.0 + nrm(ks[12], (N_A_LAYERS, D_MODEL), 0.02),
        'a_ln_b': nrm(ks[13], (N_A_LAYERS, D_MODEL), 0.02),
        'b_w_kv': nrm(ks[14], (D_MODEL, 2 * B_KVW), D_MODEL ** -0.5),
        'b_w_in': nrm(ks[15], (N_B_LAYERS, D_MODEL, B_IN), D_MODEL ** -0.5),
        'b_sinks': nrm(ks[16], (N_B_LAYERS, B_Q_HEADS), 0.5),
        'b_w_out': nrm(ks[17], (N_B_LAYERS, B_QW, D_MODEL), (B_QW ** -0.5) * DN_BETA),
        'b_ln_g': 1.0 + nrm(ks[18], (N_B_LAYERS, D_MODEL), 0.02),
        'b_ln_b': nrm(ks[19], (N_B_LAYERS, D_MODEL), 0.02),
    }


def reference(x_prompt, x_sample, state_delta, state_conv, cache_k, cache_v, a_w_in, a_conv_w, a_log, a_dt_bias,
              a_norm_w, a_w_out, a_ln_g, a_ln_b, b_w_kv, b_w_in, b_sinks, b_w_out, b_ln_g, b_ln_b):
    pos_prompt = jnp.arange(x_prompt.shape[1], dtype=jnp.int32)
    pos_sample = PAST_LEN + jnp.arange(x_sample.shape[1], dtype=jnp.int32)
    y_prompt, p_conv, p_delta, p_k, p_v = trunk(
        x_prompt, pos_prompt, None, None, None, None, a_w_in, a_conv_w, a_log, a_dt_bias, a_norm_w, a_w_out,
        a_ln_g, a_ln_b, b_w_kv, b_w_in, b_sinks, b_w_out, b_ln_g, b_ln_b)
    y_sample, s_conv, s_delta, s_k, s_v = trunk(
        x_sample, pos_sample, state_conv, state_delta, cache_k, cache_v, a_w_in, a_conv_w, a_log, a_dt_bias,
        a_norm_w, a_w_out, a_ln_g, a_ln_b, b_w_kv, b_w_in, b_sinks, b_w_out, b_ln_g, b_ln_b)
    return (y_prompt, y_sample, p_delta, p_conv, p_k, p_v, s_delta, s_conv, s_k, s_v)
```

```python
import functools
import math

import jax
import jax.numpy as jnp
from jax import lax
from jax.experimental import pallas as pl
from jax.experimental.pallas import tpu as pltpu

D_MODEL = 1024
DEPTH = 4
PAST_LEN = 4096
CHUNK = 64
N_A_LAYERS = DEPTH // 2
N_B_LAYERS = DEPTH - N_A_LAYERS
A_HEADS = 8
A_DK = 128
A_DV = 128
A_QK = A_HEADS * A_DK
A_VW = A_HEADS * A_DV
CONV_W = 4
CONV_DIM = 2 * A_QK + A_VW
A_MAIN = CONV_DIM + A_VW
B_Q_HEADS = 16
B_KV_HEADS = 4
B_GROUP = B_Q_HEADS // B_KV_HEADS
B_HD = 64
B_QW = B_Q_HEADS * B_HD
B_KVW = B_KV_HEADS * B_HD
WINDOW = 128
WINDOW_CHUNKS = WINDOW // CHUNK
ROPE_DIMS = B_HD // 4
ROPE_THETA = 500000.0
DN_ALPHA = (2 * DEPTH) ** 0.25
LN_EPS = 1e-5
RMS_EPS = 1e-6

LANES = 128
SUBLANES = 8
VMEM_LIMIT = 48 * 1024 * 1024
ROW_TILE = 256

F32 = jnp.float32
BF16 = jnp.bfloat16


def _mm(a, b):
    return jnp.dot(a.astype(BF16), b.astype(BF16), preferred_element_type=F32)


def _mm_nt(a, b):
    return lax.dot_general(a.astype(BF16), b.astype(BF16), (((1,), (1,)), ((), ())),
                           preferred_element_type=F32)


def _mm_tn(a, b):
    return lax.dot_general(a.astype(BF16), b.astype(BF16), (((0,), (0,)), ((), ())),
                           preferred_element_type=F32)


def _silu(x):
    return x * (1.0 / (1.0 + jnp.exp(-x)))


def _params(semantics):
    return pltpu.CompilerParams(dimension_semantics=semantics, vmem_limit_bytes=VMEM_LIMIT)


def _a_in_kernel(x_ref, w_ref, wg_ref, cw_ref, c0_ref, alog_ref, dt_ref,
                 q_ref, k_ref, v_ref, z_ref, gb_ref, cout_ref, pbuf, *, tm):
    l = pl.program_id(1)
    tail = SUBLANES - (CONV_W - 1)

    @pl.when(l == 0)
    def _():
        pbuf[tail:SUBLANES, :] = c0_ref[0]

    xb = x_ref[...].astype(BF16)
    cb = 512
    for j in range(CONV_DIM // cb):
        pbuf[SUBLANES:SUBLANES + tm, j * cb:(j + 1) * cb] = jnp.dot(
            xb, w_ref[:, j * cb:(j + 1) * cb], preferred_element_type=F32)
    for j in range(A_VW // cb):
        z_ref[:, j * cb:(j + 1) * cb] = jnp.dot(
            xb, w_ref[:, CONV_DIM + j * cb:CONV_DIM + (j + 1) * cb], preferred_element_type=F32)

    gates = jnp.dot(xb, wg_ref[...], preferred_element_type=F32)
    lane = lax.broadcasted_iota(jnp.int32, gates.shape, 1)
    sp_in = gates + dt_ref[...]
    softplus = jnp.maximum(sp_in, 0.0) + jnp.log1p(jnp.exp(-jnp.abs(sp_in)))
    gval = -jnp.exp(alog_ref[...]) * softplus
    bval = 1.0 / (1.0 + jnp.exp(-gates))
    gb_ref[...] = jnp.where(lane < A_HEADS, gval, bval)

    outs = (q_ref, k_ref, v_ref)
    for blk in range(CONV_DIM // LANES):
        cols = slice(blk * LANES, (blk + 1) * LANES)
        y = pbuf[tail:tail + tm, cols] * cw_ref[0:1, cols]
        for j in range(1, CONV_W):
            y = y + pbuf[tail + j:tail + j + tm, cols] * cw_ref[j:j + 1, cols]
        y = _silu(y)
        which, head = divmod(blk, A_HEADS)
        if which < 2:
            y = y * lax.rsqrt(jnp.sum(y * y, axis=-1, keepdims=True) + RMS_EPS)
            if which == 0:
                y = y * (A_DK ** -0.5)
        outs[which][:, head * LANES:(head + 1) * LANES] = y

    last = pbuf[tm + tail:tm + SUBLANES, :]
    cout_ref[0] = last
    pbuf[tail:SUBLANES, :] = last


def _a_in(x2, w_main, w_gate, conv_w, conv0, alog_pad, dt_pad, *, batch, seq):
    tm = min(ROW_TILE, seq)
    nl = seq // tm
    tokens = batch * seq
    row = lambda b, l: (b * nl + l, 0)
    const = lambda b, l: (0, 0)
    wide = jax.ShapeDtypeStruct((tokens, A_QK), F32)
    return pl.pallas_call(
        functools.partial(_a_in_kernel, tm=tm),
        grid=(batch, nl),
        in_specs=[
            pl.BlockSpec((tm, D_MODEL), row),
            pl.BlockSpec((D_MODEL, A_MAIN), const),
            pl.BlockSpec((D_MODEL, LANES), const),
            pl.BlockSpec((CONV_W, CONV_DIM), const),
            pl.BlockSpec((1, CONV_W - 1, CONV_DIM), lambda b, l: (b, 0, 0)),
            pl.BlockSpec((1, LANES), const),
            pl.BlockSpec((1, LANES), const),
        ],
        out_specs=[
            pl.BlockSpec((tm, A_QK), row),
            pl.BlockSpec((tm, A_QK), row),
            pl.BlockSpec((tm, A_VW), row),
            pl.BlockSpec((tm, A_VW), row),
            pl.BlockSpec((tm, LANES), row),
            pl.BlockSpec((1, CONV_W - 1, CONV_DIM), lambda b, l: (b, 0, 0)),
        ],
        out_shape=[wide, wide, wide, wide,
                   jax.ShapeDtypeStruct((tokens, LANES), F32),
                   jax.ShapeDtypeStruct((batch, CONV_W - 1, CONV_DIM), F32)],
        scratch_shapes=[pltpu.VMEM((tm + SUBLANES, CONV_DIM), F32)],
        compiler_params=_params(("arbitrary", "arbitrary")),
        name="a_in",
    )(x2, w_main, w_gate, conv_w, conv0, alog_pad, dt_pad)


def _unit_lower_inverse(a, row, col, c):
    eye = (row == col).astype(F32)
    d = jnp.where((row >> 3) == (col >> 3), a, 0.0)
    d2 = _mm(d, d)
    d4 = _mm(d2, d2)
    x = _mm(_mm(eye - d, eye + d2), eye + d4)
    shift = 3
    while (1 << shift) < c:
        coupling = jnp.where(((row >> (shift + 1)) == (col >> (shift + 1))) & ((row >> shift) > (col >> shift)), a, 0.0)
        x = x - _mm(_mm(x, coupling), x)
        shift += 1
    return x


def _delta_kernel(q_ref, k_ref, v_ref, z_ref, gb_ref, s0_ref, nw_ref, o_ref, s_ref, *, c):
    n = pl.program_id(1)

    @pl.when(n == 0)
    def _():
        s_ref[...] = s0_ref[...]

    gb = gb_ref[...]
    row = lax.broadcasted_iota(jnp.int32, (c, c), 0)
    col = lax.broadcasted_iota(jnp.int32, (c, c), 1)
    incl = row >= col
    strict = row > col

    rowl = lax.broadcasted_iota(jnp.int32, gb.shape, 0)
    gcum = gb
    step = 1
    while step < c:
        gcum = gcum + jnp.where(rowl >= step, pltpu.roll(gcum, step, axis=0), 0.0)
        step *= 2
    gcum_t = gcum.T
    nw = nw_ref[...]

    for h in range(A_HEADS):
        cols = slice(h * LANES, (h + 1) * LANES)
        qh = q_ref[:, cols]
        kh = k_ref[:, cols]
        vh = v_ref[:, cols]
        gc = gcum[:, h:h + 1]
        gr = gcum_t[h:h + 1, :]
        beta = gb[:, A_HEADS + h:A_HEADS + h + 1]
        glast = gcum[c - 1:c, h:h + 1]
        eg = jnp.exp(gc)
        decay = jnp.exp(jnp.where(incl, gc - gr, -jnp.inf))
        kb = kh * beta
        kk = _mm_nt(kb, kh)
        qk = _mm_nt(qh, kh)
        a = jnp.where(strict, kk * decay, 0.0)
        intra = qk * decay
        tinv = _unit_lower_inverse(a, row, col, c)
        u = _mm(tinv, vh * beta)
        w = _mm(tinv, kb * eg)
        s = s_ref[0, h]
        v_new = u - _mm(w, s)
        o = _mm(qh * eg, s) + _mm(intra, v_new)
        kd = kh * jnp.exp(glast - gc)
        s_ref[0, h] = s * jnp.exp(glast) + _mm_tn(kd, v_new)
        o = o * lax.rsqrt(jnp.mean(o * o, axis=-1, keepdims=True) + RMS_EPS) * nw * _silu(z_ref[:, cols])
        o_ref[:, cols] = o.astype(o_ref.dtype)


def _delta(q, k, v, z, gb, s0, norm_w, *, batch, seq):
    c = min(CHUNK, seq)
    n = seq // c
    tokens = batch * seq
    row = lambda b, i: (b * n + i, 0)
    state = lambda b, i: (b, 0, 0, 0)
    return pl.pallas_call(
        functools.partial(_delta_kernel, c=c),
        grid=(batch, n),
        in_specs=[
            pl.BlockSpec((c, A_QK), row),
            pl.BlockSpec((c, A_QK), row),
            pl.BlockSpec((c, A_VW), row),
            pl.BlockSpec((c, A_VW), row),
            pl.BlockSpec((c, LANES), row),
            pl.BlockSpec((1, A_HEADS, A_DK, A_DV), state),
            pl.BlockSpec((1, LANES), lambda b, i: (0, 0)),
        ],
        out_specs=[
            pl.BlockSpec((c, A_VW), row),
            pl.BlockSpec((1, A_HEADS, A_DK, A_DV), state),
        ],
        out_shape=[jax.ShapeDtypeStruct((tokens, A_VW), F32),
                   jax.ShapeDtypeStruct((batch, A_HEADS, A_DK, A_DV), F32)],
        compiler_params=_params(("arbitrary", "arbitrary")),
        name="delta_rule",
    )(q, k, v, z, gb, s0, norm_w)


def _out_ln_kernel(o_ref, x_ref, w_ref, g_ref, b_ref, y_ref):
    r = DN_ALPHA * x_ref[...] + jnp.dot(o_ref[...].astype(BF16), w_ref[...], preferred_element_type=F32)
    mu = jnp.mean(r, axis=-1, keepdims=True)
    d = r - mu
    var = jnp.mean(d * d, axis=-1, keepdims=True)
    y_ref[...] = d * lax.rsqrt(var + LN_EPS) * g_ref[...] + b_ref[...]


def _out_ln(o, x2, w_out, ln_g, ln_b):
    tokens = x2.shape[0]
    tm = min(ROW_TILE, tokens)
    row = lambda i: (i, 0)
    const = lambda i: (0, 0)
    return pl.pallas_call(
        _out_ln_kernel,
        grid=(tokens // tm,),
        in_specs=[
            pl.BlockSpec((tm, o.shape[1]), row),
            pl.BlockSpec((tm, D_MODEL), row),
            pl.BlockSpec((o.shape[1], D_MODEL), const),
            pl.BlockSpec((1, D_MODEL), const),
            pl.BlockSpec((1, D_MODEL), const),
        ],
        out_specs=pl.BlockSpec((tm, D_MODEL), row),
        out_shape=jax.ShapeDtypeStruct((tokens, D_MODEL), F32),
        compiler_params=_params(("arbitrary",)),
        name="out_ln",
    )(o, x2, w_out, ln_g, ln_b)


def _rope_slab(x, cos_t, sin_up, sin_dn):
    half = ROPE_DIMS // 2
    return x * cos_t + pltpu.roll(x, LANES - half, axis=1) * sin_up + pltpu.roll(x, half, axis=1) * sin_dn


def _kv_kernel(h_ref, w_ref, cos_ref, sup_ref, sdn_ref, k_ref, v_ref):
    kv = jnp.dot(h_ref[...].astype(BF16), w_ref[...], preferred_element_type=F32)
    cos_t, sin_up, sin_dn = cos_ref[...], sup_ref[...], sdn_ref[...]
    for slab in range(B_KVW // LANES):
        cols = slice(slab * LANES, (slab + 1) * LANES)
        k_ref[:, cols] = _rope_slab(kv[:, cols], cos_t, sin_up, sin_dn)
    v_ref[...] = kv[:, B_KVW:]


def _shared_kv(h2, w_kv, tables, *, batch, seq):
    tm = min(ROW_TILE, seq)
    nl = seq // tm
    tokens = batch * seq
    row = lambda b, l: (b * nl + l, 0)
    tab = lambda b, l: (l, 0)
    out = jax.ShapeDtypeStruct((tokens, B_KVW), F32)
    return pl.pallas_call(
        _kv_kernel,
        grid=(batch, nl),
        in_specs=[
            pl.BlockSpec((tm, D_MODEL), row),
            pl.BlockSpec((D_MODEL, 2 * B_KVW), lambda b, l: (0, 0)),
            pl.BlockSpec((tm, LANES), tab),
            pl.BlockSpec((tm, LANES), tab),
            pl.BlockSpec((tm, LANES), tab),
        ],
        out_specs=[pl.BlockSpec((tm, B_KVW), row), pl.BlockSpec((tm, B_KVW), row)],
        out_shape=[out, out],
        compiler_params=_params(("arbitrary", "arbitrary")),
        name="shared_kv",
    )(h2, w_kv, *tables)


def _b_in_kernel(x_ref, w_ref, cos_ref, sup_ref, sdn_ref, q_ref, z_ref):
    xb = x_ref[...].astype(BF16)
    cos_t, sin_up, sin_dn = cos_ref[...], sup_ref[...], sdn_ref[...]
    cb = 512
    for j in range(B_QW // cb):
        proj = jnp.dot(xb, w_ref[:, j * cb:(j + 1) * cb], preferred_element_type=F32)
        for slab in range(cb // LANES):
            cols = slice(slab * LANES, (slab + 1) * LANES)
            q_ref[:, j * cb + slab * LANES:j * cb + (slab + 1) * LANES] = (
                _rope_slab(proj[:, cols], cos_t, sin_up, sin_dn) * (B_HD ** -0.5))
    for j in range(B_QW // cb):
        z_ref[:, j * cb:(j + 1) * cb] = jnp.dot(
            xb, w_ref[:, B_QW + j * cb:B_QW + (j + 1) * cb], preferred_element_type=F32)


def _b_in(x2, w_in, tables, *, batch, seq):
    tm = min(ROW_TILE, seq)
    nl = seq // tm
    tokens = batch * seq
    row = lambda b, l: (b * nl + l, 0)
    tab = lambda b, l: (l, 0)
    out = jax.ShapeDtypeStruct((tokens, B_QW), F32)
    return pl.pallas_call(
        _b_in_kernel,
        grid=(batch, nl),
        in_specs=[
            pl.BlockSpec((tm, D_MODEL), row),
            pl.BlockSpec((D_MODEL, 2 * B_QW), lambda b, l: (0, 0)),
            pl.BlockSpec((tm, LANES), tab),
            pl.BlockSpec((tm, LANES), tab),
            pl.BlockSpec((tm, LANES), tab),
        ],
        out_specs=[pl.BlockSpec((tm, B_QW), row), pl.BlockSpec((tm, B_QW), row)],
        out_shape=[out, out],
        compiler_params=_params(("arbitrary", "arbitrary")),
        name="b_in",
    )(x2, w_in, *tables)


def _attn_kernel(sink_ref, q_ref, z_ref, k_ref, v_ref, o_ref, *, lq, lk, banded):
    c = pl.program_id(1)
    if banded:
        first = jnp.maximum(c - WINDOW_CHUNKS, 0)
        start = pl.multiple_of(first * CHUNK, CHUNK)
        key_chunk = first + (lax.broadcasted_iota(jnp.int32, (lq, lk), 1) >> int(math.log2(CHUNK)))
        valid = key_chunk <= c
    else:
        start = 0
        valid = None
    lane_half = lax.broadcasted_iota(jnp.int32, (lq, LANES), 1) >> int(math.log2(B_HD))
    kwin = [k_ref[0, pl.ds(start, lk), s * LANES:(s + 1) * LANES] for s in range(B_KVW // LANES)]
    vwin = [v_ref[0, pl.ds(start, lk), s * LANES:(s + 1) * LANES] for s in range(B_KVW // LANES)]
    for slab in range(B_QW // LANES):
        qs = q_ref[:, slab * LANES:(slab + 1) * LANES]
        slab_out = None
        for p in range(2):
            hq = 2 * slab + p
            j = hq // B_GROUP
            kslab, kp = divmod(j, 2)
            qm = jnp.where(lane_half == p, qs, 0.0)
            if p != kp:
                qm = pltpu.roll(qm, B_HD, axis=1)
            s = _mm_nt(qm, kwin[kslab])
            if valid is not None:
                s = jnp.where(valid, s, -jnp.inf)
            sk = sink_ref[hq]
            m = jnp.maximum(jnp.max(s, axis=-1, keepdims=True), sk)
            e = jnp.exp(s - m)
            prob = e / (jnp.sum(e, axis=-1, keepdims=True) + jnp.exp(sk - m))
            pv = _mm(prob, vwin[kslab])
            if p != kp:
                pv = pltpu.roll(pv, B_HD, axis=1)
            slab_out = pv if slab_out is None else jnp.where(lane_half == 0, slab_out, pv)
        zs = z_ref[:, slab * LANES:(slab + 1) * LANES]
        o_ref[:, slab * LANES:(slab + 1) * LANES] = slab_out * _silu(zs)


def _attention(q, z, k3, v3, sinks, *, batch, seq, banded):
    lq = min(CHUNK, seq)
    nq = seq // lq
    ltot = k3.shape[1]
    lk = (WINDOW_CHUNKS + 1) * CHUNK if banded else ltot
    tokens = batch * seq
    row = lambda b, c: (b * nq + c, 0)
    whole = lambda b, c: (b, 0, 0)
    return pl.pallas_call(
        functools.partial(_attn_kernel, lq=lq, lk=lk, banded=banded),
        grid=(batch, nq),
        in_specs=[
            pl.BlockSpec(memory_space=pltpu.SMEM),
            pl.BlockSpec((lq, B_QW), row),
            pl.BlockSpec((lq, B_QW), row),
            pl.BlockSpec((1, ltot, B_KVW), whole),
            pl.BlockSpec((1, ltot, B_KVW), whole),
        ],
        out_specs=pl.BlockSpec((lq, B_QW), row),
        out_shape=jax.ShapeDtypeStruct((tokens, B_QW), F32),
        compiler_params=_params(("arbitrary", "arbitrary")),
        name="swa_attention",
    )(sinks, q, z, k3, v3)


def _rope_tables(pos):
    half = ROPE_DIMS // 2
    inv = ROPE_THETA ** (-jnp.arange(half, dtype=F32) * 2.0 / ROPE_DIMS)
    ang = pos.astype(F32)[:, None] * inv[None, :]
    cos, sin = jnp.cos(ang), jnp.sin(ang)
    ones = jnp.ones((pos.shape[0], B_HD - ROPE_DIMS), F32)
    zeros_h = jnp.zeros((pos.shape[0], half), F32)
    zeros_r = jnp.zeros((pos.shape[0], B_HD - ROPE_DIMS), F32)
    cos_head = jnp.concatenate([cos, cos, ones], axis=1)
    up_head = jnp.concatenate([-sin, zeros_h, zeros_r], axis=1)
    dn_head = jnp.concatenate([zeros_h, sin, zeros_r], axis=1)
    rep = LANES // B_HD
    return tuple(jnp.tile(t, (1, rep)) for t in (cos_head, up_head, dn_head))


def _pad_lanes(vec, offset=0):
    out = jnp.zeros((1, LANES), F32)
    return out.at[0, offset:offset + vec.shape[0]].set(vec.astype(F32))


def _trunk(x, pos, conv_state, delta_state, past_k, past_v, wts):
    batch, seq, _ = x.shape
    tokens = batch * seq
    h = x.reshape(tokens, D_MODEL)
    tables = _rope_tables(pos)
    new_conv, new_delta = [], []
    for i in range(N_A_LAYERS):
        q, k, v, z, gb, cbuf = _a_in(h, wts["a_w_main"][i], wts["a_w_gate"][i], wts["a_conv_w"][i], conv_state[i],
                                     wts["a_log"][i], wts["a_dt"][i], batch=batch, seq=seq)
        o, s_new = _delta(q, k, v, z, gb, delta_state[i], wts["a_norm_w"][i], batch=batch, seq=seq)
        h = _out_ln(o, h, wts["a_w_out"][i], wts["a_ln_g"][i], wts["a_ln_b"][i])
        new_conv.append(cbuf)
        new_delta.append(s_new)
    k2, v2 = _shared_kv(h, wts["b_w_kv"], tables, batch=batch, seq=seq)
    k3 = k2.reshape(batch, seq, B_KVW)
    v3 = v2.reshape(batch, seq, B_KVW)
    cached = past_k is not None
    if cached:
        k3 = jnp.concatenate([past_k.reshape(batch, -1, B_KVW), k3], axis=1)
        v3 = jnp.concatenate([past_v.reshape(batch, -1, B_KVW), v3], axis=1)
    new_k = k3[:, -WINDOW:].reshape(batch, WINDOW, B_KV_HEADS, B_HD)
    new_v = v3[:, -WINDOW:].reshape(batch, WINDOW, B_KV_HEADS, B_HD)
    for j in range(N_B_LAYERS):
        q, z = _b_in(h, wts["b_w_in"][j], tables, batch=batch, seq=seq)
        o = _attention(q, z, k3, v3, wts["b_sinks"][j], batch=batch, seq=seq, banded=not cached)
        h = _out_ln(o, h, wts["b_w_out"][j], wts["b_ln_g"][j], wts["b_ln_b"][j])
    return h.reshape(batch, seq, D_MODEL), jnp.stack(new_conv), jnp.stack(new_delta), new_k, new_v


def kernel(x_prompt, x_sample, state_delta, state_conv, cache_k, cache_v, a_w_in, a_conv_w, a_log, a_dt_bias,
           a_norm_w, a_w_out, a_ln_g, a_ln_b, b_w_kv, b_w_in, b_sinks, b_w_out, b_ln_g, b_ln_b):
    gate_cols = a_w_in[:, :, A_MAIN:]
    wts = {
        "a_w_main": a_w_in[:, :, :A_MAIN].astype(BF16),
        "a_w_gate": jnp.pad(gate_cols, ((0, 0), (0, 0), (0, LANES - 2 * A_HEADS))).astype(BF16),
        "a_conv_w": a_conv_w,
        "a_log": [_pad_lanes(a_log[i]) for i in range(N_A_LAYERS)],
        "a_dt": [_pad_lanes(a_dt_bias[i]) for i in range(N_A_LAYERS)],
        "a_norm_w": a_norm_w.reshape(N_A_LAYERS, 1, A_DV),
        "a_w_out": a_w_out.astype(BF16),
        "a_ln_g": a_ln_g.reshape(N_A_LAYERS, 1, D_MODEL),
        "a_ln_b": a_ln_b.reshape(N_A_LAYERS, 1, D_MODEL),
        "b_w_kv": b_w_kv.astype(BF16),
        "b_w_in": b_w_in.astype(BF16),
        "b_sinks": b_sinks,
        "b_w_out": b_w_out.astype(BF16),
        "b_ln_g": b_ln_g.reshape(N_B_LAYERS, 1, D_MODEL),
        "b_ln_b": b_ln_b.reshape(N_B_LAYERS, 1, D_MODEL),
    }
    bp, lp, _ = x_prompt.shape
    bs, ls, _ = x_sample.shape
    pos_prompt = jnp.arange(lp, dtype=jnp.int32)
    pos_sample = PAST_LEN + jnp.arange(ls, dtype=jnp.int32)
    zero_conv = jnp.zeros((N_A_LAYERS, bp, CONV_W - 1, CONV_DIM), F32)
    zero_delta = jnp.zeros((N_A_LAYERS, bp, A_HEADS, A_DK, A_DV), F32)
    y_p, p_conv, p_delta, p_k, p_v = _trunk(x_prompt, pos_prompt, zero_conv, zero_delta, None, None, wts)
    y_s, s_conv, s_delta, s_k, s_v = _trunk(x_sample, pos_sample, state_conv, state_delta, cache_k, cache_v, wts)
    return (y_p, y_s, p_delta, p_conv, p_k, p_v, s_delta, s_conv, s_k, s_v)
```

```python
import functools
import math

import jax
import jax.numpy as jnp
from jax import lax
from jax.experimental import pallas as pl
from jax.experimental.pallas import tpu as pltpu

D_MODEL = 1024
DEPTH = 4
PAST_LEN = 4096
CHUNK = 64
N_A_LAYERS = DEPTH // 2
N_B_LAYERS = DEPTH - N_A_LAYERS
A_HEADS = 8
A_DK = 128
A_DV = 128
A_QK = A_HEADS * A_DK
A_VW = A_HEADS * A_DV
CONV_W = 4
CONV_DIM = 2 * A_QK + A_VW
A_MAIN = CONV_DIM + A_VW
B_Q_HEADS = 16
B_KV_HEADS = 4
B_GROUP = B_Q_HEADS // B_KV_HEADS
B_HD = 64
B_QW = B_Q_HEADS * B_HD
B_KVW = B_KV_HEADS * B_HD
WINDOW = 128
WINDOW_CHUNKS = WINDOW // CHUNK
ROPE_DIMS = B_HD // 4
ROPE_THETA = 500000.0
DN_ALPHA = (2 * DEPTH) ** 0.25
LN_EPS = 1e-5
RMS_EPS = 1e-6

LANES = 128
SUBLANES = 8
VMEM_LIMIT = 48 * 1024 * 1024
ROW_TILE = 256

F32 = jnp.float32
BF16 = jnp.bfloat16


def _mm(a, b):
    return jnp.dot(a.astype(BF16), b.astype(BF16), preferred_element_type=F32)


def _mm_nt(a, b):
    return lax.dot_general(a.astype(BF16), b.astype(BF16), (((1,), (1,)), ((), ())),
                           preferred_element_type=F32)


def _mm_tn(a, b):
    return lax.dot_general(a.astype(BF16), b.astype(BF16), (((0,), (0,)), ((), ())),
                           preferred_element_type=F32)


def _silu(x):
    return x * (1.0 / (1.0 + jnp.exp(-x)))


def _params(semantics):
    return pltpu.CompilerParams(dimension_semantics=semantics, vmem_limit_bytes=VMEM_LIMIT)


def _a_in_kernel(x_ref, w_ref, wg_ref, cw_ref, c0_ref, alog_ref, dt_ref,
                 q_ref, k_ref, v_ref, z_ref, gb_ref, cout_ref, pbuf, *, tm):
    l = pl.program_id(1)
    tail = SUBLANES - (CONV_W - 1)

    @pl.when(l == 0)
    def _():
        pbuf[tail:SUBLANES, :] = c0_ref[0]

    xb = x_ref[...].astype(BF16)
    cb = 512
    for j in range(CONV_DIM // cb):
        pbuf[SUBLANES:SUBLANES + tm, j * cb:(j + 1) * cb] = jnp.dot(
            xb, w_ref[:, j * cb:(j + 1) * cb], preferred_element_type=F32)
    for j in range(A_VW // cb):
        z_ref[:, j * cb:(j + 1) * cb] = jnp.dot(
            xb, w_ref[:, CONV_DIM + j * cb:CONV_DIM + (j + 1) * cb], preferred_element_type=F32)

    gates = jnp.dot(xb, wg_ref[...], preferred_element_type=F32)
    lane = lax.broadcasted_iota(jnp.int32, gates.shape, 1)
    sp_in = gates + dt_ref[...]
    softplus = jnp.maximum(sp_in, 0.0) + jnp.log1p(jnp.exp(-jnp.abs(sp_in)))
    gval = -jnp.exp(alog_ref[...]) * softplus
    bval = 1.0 / (1.0 + jnp.exp(-gates))
    gb_ref[...] = jnp.where(lane < A_HEADS, gval, bval)

    outs = (q_ref, k_ref, v_ref)
    for blk in range(CONV_DIM // LANES):
        cols = slice(blk * LANES, (blk + 1) * LANES)
        y = pbuf[tail:tail + tm, cols] * cw_ref[0:1, cols]
        for j in range(1, CONV_W):
            y = y + pbuf[tail + j:tail + j + tm, cols] * cw_ref[j:j + 1, cols]
        y = _silu(y)
        which, head = divmod(blk, A_HEADS)
        if which < 2:
            y = y * lax.rsqrt(jnp.sum(y * y, axis=-1, keepdims=True) + RMS_EPS)
            if which == 0:
                y = y * (A_DK ** -0.5)
        outs[which][:, head * LANES:(head + 1) * LANES] = y

    last = pbuf[tm + tail:tm + SUBLANES, :]
    cout_ref[0] = last
    pbuf[tail:SUBLANES, :] = last


def _a_in(x2, w_main, w_gate, conv_w, conv0, alog_pad, dt_pad, *, batch, seq):
    tm = min(ROW_TILE, seq)
    nl = seq // tm
    tokens = batch * seq
    row = lambda b, l: (b * nl + l, 0)
    const = lambda b, l: (0, 0)
    wide = jax.ShapeDtypeStruct((tokens, A_QK), F32)
    return pl.pallas_call(
        functools.partial(_a_in_kernel, tm=tm),
        grid=(batch, nl),
        in_specs=[
            pl.BlockSpec((tm, D_MODEL), row),
            pl.BlockSpec((D_MODEL, A_MAIN), const),
            pl.BlockSpec((D_MODEL, LANES), const),
            pl.BlockSpec((CONV_W, CONV_DIM), const),
            pl.BlockSpec((1, CONV_W - 1, CONV_DIM), lambda b, l: (b, 0, 0)),
            pl.BlockSpec((1, LANES), const),
            pl.BlockSpec((1, LANES), const),
        ],
        out_specs=[
            pl.BlockSpec((tm, A_QK), row),
            pl.BlockSpec((tm, A_QK), row),
            pl.BlockSpec((tm, A_VW), row),
            pl.BlockSpec((tm, A_VW), row),
            pl.BlockSpec((tm, LANES), row),
            pl.BlockSpec((1, CONV_W - 1, CONV_DIM), lambda b, l: (b, 0, 0)),
        ],
        out_shape=[wide, wide, wide, wide,
                   jax.ShapeDtypeStruct((tokens, LANES), F32),
                   jax.ShapeDtypeStruct((batch, CONV_W - 1, CONV_DIM), F32)],
        scratch_shapes=[pltpu.VMEM((tm + SUBLANES, CONV_DIM), F32)],
        compiler_params=_params(("arbitrary", "arbitrary")),
        name="a_in",
    )(x2, w_main, w_gate, conv_w, conv0, alog_pad, dt_pad)


def _unit_lower_inverse(mats, row, col, c):
    eye = (row == col).astype(F32)
    diag8 = (row >> 3) == (col >> 3)
    d = [jnp.where(diag8, a, 0.0) for a in mats]
    d2 = [_mm(x, x) for x in d]
    d4 = [_mm(x, x) for x in d2]
    p1 = [_mm(eye - x, eye + y) for x, y in zip(d, d2)]
    xs = [_mm(p, eye + y) for p, y in zip(p1, d4)]
    shift = 3
    while (1 << shift) < c:
        mask = ((row >> (shift + 1)) == (col >> (shift + 1))) & ((row >> shift) > (col >> shift))
        xc = [_mm(x, jnp.where(mask, a, 0.0)) for x, a in zip(xs, mats)]
        xs = [x - _mm(y, x) for x, y in zip(xs, xc)]
        shift += 1
    return xs


def _delta_kernel(q_ref, k_ref, v_ref, z_ref, gb_ref, s0_ref, nw_ref, o_ref, s_ref, *, c):
    n = pl.program_id(1)

    @pl.when(n == 0)
    def _():
        s_ref[...] = s0_ref[...]

    gb = gb_ref[...]
    row = lax.broadcasted_iota(jnp.int32, (c, c), 0)
    col = lax.broadcasted_iota(jnp.int32, (c, c), 1)
    incl = row >= col
    strict = row > col

    rowl = lax.broadcasted_iota(jnp.int32, gb.shape, 0)
    gcum = gb
    step = 1
    while step < c:
        gcum = gcum + jnp.where(rowl >= step, pltpu.roll(gcum, step, axis=0), 0.0)
        step *= 2
    gcum_t = gcum.T
    nw = nw_ref[...]

    heads = range(A_HEADS)
    cols = [slice(h * LANES, (h + 1) * LANES) for h in heads]
    q = [q_ref[:, cs] for cs in cols]
    k = [k_ref[:, cs] for cs in cols]
    v = [v_ref[:, cs] for cs in cols]
    s = [s_ref[0, h] for h in heads]
    gc = [gcum[:, h:h + 1] for h in heads]
    beta = [gb[:, A_HEADS + h:A_HEADS + h + 1] for h in heads]
    glast = [gcum[c - 1:c, h:h + 1] for h in heads]
    eg = [jnp.exp(x) for x in gc]
    decay = [jnp.exp(jnp.where(incl, gc[h] - gcum_t[h:h + 1, :], -jnp.inf)) for h in heads]
    kb = [k[h] * beta[h] for h in heads]
    kk = [_mm_nt(kb[h], k[h]) for h in heads]
    qk = [_mm_nt(q[h], k[h]) for h in heads]
    a = [jnp.where(strict, kk[h] * decay[h], 0.0) for h in heads]
    tinv = _unit_lower_inverse(a, row, col, c)
    u = [_mm(tinv[h], v[h] * beta[h]) for h in heads]
    w = [_mm(tinv[h], kb[h] * eg[h]) for h in heads]
    v_new = [u[h] - _mm(w[h], s[h]) for h in heads]
    o = [_mm(q[h] * eg[h], s[h]) + _mm(qk[h] * decay[h], v_new[h]) for h in heads]
    s_new = [s[h] * jnp.exp(glast[h]) + _mm_tn(k[h] * jnp.exp(glast[h] - gc[h]), v_new[h]) for h in heads]
    for h in heads:
        s_ref[0, h] = s_new[h]
        gated = (o[h] * lax.rsqrt(jnp.mean(o[h] * o[h], axis=-1, keepdims=True) + RMS_EPS) * nw
                 * _silu(z_ref[:, cols[h]]))
        o_ref[:, cols[h]] = gated.astype(o_ref.dtype)


def _delta(q, k, v, z, gb, s0, norm_w, *, batch, seq):
    c = min(CHUNK, seq)
    n = seq // c
    tokens = batch * seq
    row = lambda b, i: (b * n + i, 0)
    state = lambda b, i: (b, 0, 0, 0)
    return pl.pallas_call(
        functools.partial(_delta_kernel, c=c),
        grid=(batch, n),
        in_specs=[
            pl.BlockSpec((c, A_QK), row),
            pl.BlockSpec((c, A_QK), row),
            pl.BlockSpec((c, A_VW), row),
            pl.BlockSpec((c, A_VW), row),
            pl.BlockSpec((c, LANES), row),
            pl.BlockSpec((1, A_HEADS, A_DK, A_DV), state),
            pl.BlockSpec((1, LANES), lambda b, i: (0, 0)),
        ],
        out_specs=[
            pl.BlockSpec((c, A_VW), row),
            pl.BlockSpec((1, A_HEADS, A_DK, A_DV), state),
        ],
        out_shape=[jax.ShapeDtypeStruct((tokens, A_VW), F32),
                   jax.ShapeDtypeStruct((batch, A_HEADS, A_DK, A_DV), F32)],
        compiler_params=_params(("arbitrary", "arbitrary")),
        name="delta_rule",
    )(q, k, v, z, gb, s0, norm_w)


def _out_ln_kernel(o_ref, x_ref, w_ref, g_ref, b_ref, y_ref):
    r = DN_ALPHA * x_ref[...] + jnp.dot(o_ref[...].astype(BF16), w_ref[...], preferred_element_type=F32)
    mu = jnp.mean(r, axis=-1, keepdims=True)
    d = r - mu
    var = jnp.mean(d * d, axis=-1, keepdims=True)
    y_ref[...] = d * lax.rsqrt(var + LN_EPS) * g_ref[...] + b_ref[...]


def _out_ln(o, x2, w_out, ln_g, ln_b):
    tokens = x2.shape[0]
    tm = min(ROW_TILE, tokens)
    row = lambda i: (i, 0)
    const = lambda i: (0, 0)
    return pl.pallas_call(
        _out_ln_kernel,
        grid=(tokens // tm,),
        in_specs=[
            pl.BlockSpec((tm, o.shape[1]), row),
            pl.BlockSpec((tm, D_MODEL), row),
            pl.BlockSpec((o.shape[1], D_MODEL), const),
            pl.BlockSpec((1, D_MODEL), const),
            pl.BlockSpec((1, D_MODEL), const),
        ],
        out_specs=pl.BlockSpec((tm, D_MODEL), row),
        out_shape=jax.ShapeDtypeStruct((tokens, D_MODEL), F32),
        compiler_params=_params(("arbitrary",)),
        name="out_ln",
    )(o, x2, w_out, ln_g, ln_b)


def _rope_slab(x, cos_t, sin_up, sin_dn):
    half = ROPE_DIMS // 2
    return x * cos_t + pltpu.roll(x, LANES - half, axis=1) * sin_up + pltpu.roll(x, half, axis=1) * sin_dn


def _kv_kernel(h_ref, w_ref, cos_ref, sup_ref, sdn_ref, k_ref, v_ref):
    kv = jnp.dot(h_ref[...].astype(BF16), w_ref[...], preferred_element_type=F32)
    cos_t, sin_up, sin_dn = cos_ref[...], sup_ref[...], sdn_ref[...]
    for slab in range(B_KVW // LANES):
        cols = slice(slab * LANES, (slab + 1) * LANES)
        k_ref[:, cols] = _rope_slab(kv[:, cols], cos_t, sin_up, sin_dn)
    v_ref[...] = kv[:, B_KVW:]


def _shared_kv(h2, w_kv, tables, *, batch, seq):
    tm = min(ROW_TILE, seq)
    nl = seq // tm
    tokens = batch * seq
    row = lambda b, l: (b * nl + l, 0)
    tab = lambda b, l: (l, 0)
    out = jax.ShapeDtypeStruct((tokens, B_KVW), F32)
    return pl.pallas_call(
        _kv_kernel,
        grid=(batch, nl),
        in_specs=[
            pl.BlockSpec((tm, D_MODEL), row),
            pl.BlockSpec((D_MODEL, 2 * B_KVW), lambda b, l: (0, 0)),
            pl.BlockSpec((tm, LANES), tab),
            pl.BlockSpec((tm, LANES), tab),
            pl.BlockSpec((tm, LANES), tab),
        ],
        out_specs=[pl.BlockSpec((tm, B_KVW), row), pl.BlockSpec((tm, B_KVW), row)],
        out_shape=[out, out],
        compiler_params=_params(("arbitrary", "arbitrary")),
        name="shared_kv",
    )(h2, w_kv, *tables)


def _b_in_kernel(x_ref, w_ref, cos_ref, sup_ref, sdn_ref, q_ref, z_ref):
    xb = x_ref[...].astype(BF16)
    cos_t, sin_up, sin_dn = cos_ref[...], sup_ref[...], sdn_ref[...]
    cb = 512
    for j in range(B_QW // cb):
        proj = jnp.dot(xb, w_ref[:, j * cb:(j + 1) * cb], preferred_element_type=F32)
        for slab in range(cb // LANES):
            cols = slice(slab * LANES, (slab + 1) * LANES)
            q_ref[:, j * cb + slab * LANES:j * cb + (slab + 1) * LANES] = (
                _rope_slab(proj[:, cols], cos_t, sin_up, sin_dn) * (B_HD ** -0.5))
    for j in range(B_QW // cb):
        z_ref[:, j * cb:(j + 1) * cb] = jnp.dot(
            xb, w_ref[:, B_QW + j * cb:B_QW + (j + 1) * cb], preferred_element_type=F32)


def _b_in(x2, w_in, tables, *, batch, seq):
    tm = min(ROW_TILE, seq)
    nl = seq // tm
    tokens = batch * seq
    row = lambda b, l: (b * nl + l, 0)
    tab = lambda b, l: (l, 0)
    out = jax.ShapeDtypeStruct((tokens, B_QW), F32)
    return pl.pallas_call(
        _b_in_kernel,
        grid=(batch, nl),
        in_specs=[
            pl.BlockSpec((tm, D_MODEL), row),
            pl.BlockSpec((D_MODEL, 2 * B_QW), lambda b, l: (0, 0)),
            pl.BlockSpec((tm, LANES), tab),
            pl.BlockSpec((tm, LANES), tab),
            pl.BlockSpec((tm, LANES), tab),
        ],
        out_specs=[pl.BlockSpec((tm, B_QW), row), pl.BlockSpec((tm, B_QW), row)],
        out_shape=[out, out],
        compiler_params=_params(("arbitrary", "arbitrary")),
        name="b_in",
    )(x2, w_in, *tables)


def _attn_kernel(sink_ref, q_ref, z_ref, k_ref, v_ref, o_ref, *, lq, lk, banded):
    c = pl.program_id(1)
    if banded:
        first = jnp.maximum(c - WINDOW_CHUNKS, 0)
        start = pl.multiple_of(first * CHUNK, CHUNK)
        key_chunk = first + (lax.broadcasted_iota(jnp.int32, (lq, lk), 1) >> int(math.log2(CHUNK)))
        valid = key_chunk <= c
    else:
        start = 0
        valid = None
    lane_half = lax.broadcasted_iota(jnp.int32, (lq, LANES), 1) >> int(math.log2(B_HD))
    kwin = [k_ref[0, pl.ds(start, lk), s * LANES:(s + 1) * LANES] for s in range(B_KVW // LANES)]
    vwin = [v_ref[0, pl.ds(start, lk), s * LANES:(s + 1) * LANES] for s in range(B_KVW // LANES)]
    heads = range(B_Q_HEADS)
    kslab = [(hq // B_GROUP) // 2 for hq in heads]
    swap = [(hq % 2) != ((hq // B_GROUP) % 2) for hq in heads]
    qm = []
    for hq in heads:
        qs = q_ref[:, (hq // 2) * LANES:(hq // 2 + 1) * LANES]
        x = jnp.where(lane_half == hq % 2, qs, 0.0)
        qm.append(pltpu.roll(x, B_HD, axis=1) if swap[hq] else x)
    scores = [_mm_nt(qm[hq], kwin[kslab[hq]]) for hq in heads]
    if valid is not None:
        scores = [jnp.where(valid, s, -jnp.inf) for s in scores]
    probs = []
    for hq in heads:
        sk = sink_ref[hq]
        m = jnp.maximum(jnp.max(scores[hq], axis=-1, keepdims=True), sk)
        e = jnp.exp(scores[hq] - m)
        probs.append(e / (jnp.sum(e, axis=-1, keepdims=True) + jnp.exp(sk - m)))
    pv = [_mm(probs[hq], vwin[kslab[hq]]) for hq in heads]
    pv = [pltpu.roll(x, B_HD, axis=1) if swap[hq] else x for hq, x in enumerate(pv)]
    for slab in range(B_QW // LANES):
        both = jnp.where(lane_half == 0, pv[2 * slab], pv[2 * slab + 1])
        zs = z_ref[:, slab * LANES:(slab + 1) * LANES]
        o_ref[:, slab * LANES:(slab + 1) * LANES] = both * _silu(zs)


def _attention(q, z, k3, v3, sinks, *, batch, seq, banded):
    lq = min(CHUNK, seq)
    nq = seq // lq
    ltot = k3.shape[1]
    lk = (WINDOW_CHUNKS + 1) * CHUNK if banded else ltot
    tokens = batch * seq
    row = lambda b, c: (b * nq + c, 0)
    whole = lambda b, c: (b, 0, 0)
    return pl.pallas_call(
        functools.partial(_attn_kernel, lq=lq, lk=lk, banded=banded),
        grid=(batch, nq),
        in_specs=[
            pl.BlockSpec(memory_space=pltpu.SMEM),
            pl.BlockSpec((lq, B_QW), row),
            pl.BlockSpec((lq, B_QW), row),
            pl.BlockSpec((1, ltot, B_KVW), whole),
            pl.BlockSpec((1, ltot, B_KVW), whole),
        ],
        out_specs=pl.BlockSpec((lq, B_QW), row),
        out_shape=jax.ShapeDtypeStruct((tokens, B_QW), F32),
        compiler_params=_params(("arbitrary", "arbitrary")),
        name="swa_attention",
    )(sinks, q, z, k3, v3)


def _rope_tables(pos):
    half = ROPE_DIMS // 2
    inv = ROPE_THETA ** (-jnp.arange(half, dtype=F32) * 2.0 / ROPE_DIMS)
    ang = pos.astype(F32)[:, None] * inv[None, :]
    cos, sin = jnp.cos(ang), jnp.sin(ang)
    ones = jnp.ones((pos.shape[0], B_HD - ROPE_DIMS), F32)
    zeros_h = jnp.zeros((pos.shape[0], half), F32)
    zeros_r = jnp.zeros((pos.shape[0], B_HD - ROPE_DIMS), F32)
    cos_head = jnp.concatenate([cos, cos, ones], axis=1)
    up_head = jnp.concatenate([-sin, zeros_h, zeros_r], axis=1)
    dn_head = jnp.concatenate([zeros_h, sin, zeros_r], axis=1)
    rep = LANES // B_HD
    return tuple(jnp.tile(t, (1, rep)) for t in (cos_head, up_head, dn_head))


def _pad_lanes(vec, offset=0):
    out = jnp.zeros((1, LANES), F32)
    return out.at[0, offset:offset + vec.shape[0]].set(vec.astype(F32))


def _trunk(x, pos, conv_state, delta_state, past_k, past_v, wts):
    batch, seq, _ = x.shape
    tokens = batch * seq
    h = x.reshape(tokens, D_MODEL)
    tables = _rope_tables(pos)
    new_conv, new_delta = [], []
    for i in range(N_A_LAYERS):
        q, k, v, z, gb, cbuf = _a_in(h, wts["a_w_main"][i], wts["a_w_gate"][i], wts["a_conv_w"][i], conv_state[i],
                                     wts["a_log"][i], wts["a_dt"][i], batch=batch, seq=seq)
        o, s_new = _delta(q, k, v, z, gb, delta_state[i], wts["a_norm_w"][i], batch=batch, seq=seq)
        h = _out_ln(o, h, wts["a_w_out"][i], wts["a_ln_g"][i], wts["a_ln_b"][i])
        new_conv.append(cbuf)
        new_delta.append(s_new)
    k2, v2 = _shared_kv(h, wts["b_w_kv"], tables, batch=batch, seq=seq)
    k3 = k2.reshape(batch, seq, B_KVW)
    v3 = v2.reshape(batch, seq, B_KVW)
    cached = past_k is not None
    if cached:
        k3 = jnp.concatenate([past_k.reshape(batch, -1, B_KVW), k3], axis=1)
        v3 = jnp.concatenate([past_v.reshape(batch, -1, B_KVW), v3], axis=1)
    new_k = k3[:, -WINDOW:].reshape(batch, WINDOW, B_KV_HEADS, B_HD)
    new_v = v3[:, -WINDOW:].reshape(batch, WINDOW, B_KV_HEADS, B_HD)
    for j in range(N_B_LAYERS):
        q, z = _b_in(h, wts["b_w_in"][j], tables, batch=batch, seq=seq)
        o = _attention(q, z, k3, v3, wts["b_sinks"][j], batch=batch, seq=seq, banded=not cached)
        h = _out_ln(o, h, wts["b_w_out"][j], wts["b_ln_g"][j], wts["b_ln_b"][j])
    return h.reshape(batch, seq, D_MODEL), jnp.stack(new_conv), jnp.stack(new_delta), new_k, new_v


def kernel(x_prompt, x_sample, state_delta, state_conv, cache_k, cache_v, a_w_in, a_conv_w, a_log, a_dt_bias,
           a_norm_w, a_w_out, a_ln_g, a_ln_b, b_w_kv, b_w_in, b_sinks, b_w_out, b_ln_g, b_ln_b):
    gate_cols = a_w_in[:, :, A_MAIN:]
    wts = {
        "a_w_main": a_w_in[:, :, :A_MAIN].astype(BF16),
        "a_w_gate": jnp.pad(gate_cols, ((0, 0), (0, 0), (0, LANES - 2 * A_HEADS))).astype(BF16),
        "a_conv_w": a_conv_w,
        "a_log": [_pad_lanes(a_log[i]) for i in range(N_A_LAYERS)],
        "a_dt": [_pad_lanes(a_dt_bias[i]) for i in range(N_A_LAYERS)],
        "a_norm_w": a_norm_w.reshape(N_A_LAYERS, 1, A_DV),
        "a_w_out": a_w_out.astype(BF16),
        "a_ln_g": a_ln_g.reshape(N_A_LAYERS, 1, D_MODEL),
        "a_ln_b": a_ln_b.reshape(N_A_LAYERS, 1, D_MODEL),
        "b_w_kv": b_w_kv.astype(BF16),
        "b_w_in": b_w_in.astype(BF16),
        "b_sinks": b_sinks,
        "b_w_out": b_w_out.astype(BF16),
        "b_ln_g": b_ln_g.reshape(N_B_LAYERS, 1, D_MODEL),
        "b_ln_b": b_ln_b.reshape(N_B_LAYERS, 1, D_MODEL),
    }
    bp, lp, _ = x_prompt.shape
    bs, ls, _ = x_sample.shape
    pos_prompt = jnp.arange(lp, dtype=jnp.int32)
    pos_sample = PAST_LEN + jnp.arange(ls, dtype=jnp.int32)
    zero_conv = jnp.zeros((N_A_LAYERS, bp, CONV_W - 1, CONV_DIM), F32)
    zero_delta = jnp.zeros((N_A_LAYERS, bp, A_HEADS, A_DK, A_DV), F32)
    y_p, p_conv, p_delta, p_k, p_v = _trunk(x_prompt, pos_prompt, zero_conv, zero_delta, None, None, wts)
    y_s, s_conv, s_delta, s_k, s_v = _trunk(x_sample, pos_sample, state_conv, state_delta, cache_k, cache_v, wts)
    return (y_p, y_s, p_delta, p_conv, p_k, p_v, s_delta, s_conv, s_k, s_v)
```

```python
import functools
import math

import jax
import jax.numpy as jnp
from jax import lax
from jax.experimental import pallas as pl
from jax.experimental.pallas import tpu as pltpu

D_MODEL = 1024
DEPTH = 4
PAST_LEN = 4096
CHUNK = 64
N_A_LAYERS = DEPTH // 2
N_B_LAYERS = DEPTH - N_A_LAYERS
A_HEADS = 8
A_DK = 128
A_DV = 128
A_QK = A_HEADS * A_DK
A_VW = A_HEADS * A_DV
CONV_W = 4
CONV_DIM = 2 * A_QK + A_VW
A_MAIN = CONV_DIM + A_VW
B_Q_HEADS = 16
B_KV_HEADS = 4
B_GROUP = B_Q_HEADS // B_KV_HEADS
B_HD = 64
B_QW = B_Q_HEADS * B_HD
B_KVW = B_KV_HEADS * B_HD
WINDOW = 128
WINDOW_CHUNKS = WINDOW // CHUNK
ROPE_DIMS = B_HD // 4
ROPE_THETA = 500000.0
DN_ALPHA = (2 * DEPTH) ** 0.25
LN_EPS = 1e-5
RMS_EPS = 1e-6

LANES = 128
SUBLANES = 8
VMEM_LIMIT = 48 * 1024 * 1024
ROW_TILE = 256
DELTA_BATCHES = 2
B_QX = B_Q_HEADS * LANES

F32 = jnp.float32
BF16 = jnp.bfloat16


def _mm(a, b):
    return jnp.dot(a.astype(BF16), b.astype(BF16), preferred_element_type=F32)


def _mm_nt(a, b):
    return lax.dot_general(a.astype(BF16), b.astype(BF16), (((1,), (1,)), ((), ())),
                           preferred_element_type=F32)


def _silu(x):
    return x * (1.0 / (1.0 + jnp.exp(-x)))


def _params(semantics):
    return pltpu.CompilerParams(dimension_semantics=semantics, vmem_limit_bytes=VMEM_LIMIT)


def _a_in_kernel(x_ref, w_ref, wg_ref, cw_ref, c0_ref, alog_ref, dt_ref,
                 q_ref, k_ref, v_ref, z_ref, gb_ref, cout_ref, pbuf, *, tm):
    l = pl.program_id(1)
    tail = SUBLANES - (CONV_W - 1)

    @pl.when(l == 0)
    def _():
        pbuf[tail:SUBLANES, :] = c0_ref[0]

    xb = x_ref[...].astype(BF16)
    cb = 512
    for j in range(CONV_DIM // cb):
        pbuf[SUBLANES:SUBLANES + tm, j * cb:(j + 1) * cb] = jnp.dot(
            xb, w_ref[:, j * cb:(j + 1) * cb], preferred_element_type=F32)
    for j in range(A_VW // cb):
        z_ref[:, j * cb:(j + 1) * cb] = jnp.dot(
            xb, w_ref[:, CONV_DIM + j * cb:CONV_DIM + (j + 1) * cb], preferred_element_type=F32).astype(z_ref.dtype)

    gates = jnp.dot(xb, wg_ref[...], preferred_element_type=F32)
    gt = gates.T[:2 * A_HEADS]
    rowi = lax.broadcasted_iota(jnp.int32, gt.shape, 0)
    sp_in = gt + dt_ref[...]
    softplus = jnp.maximum(sp_in, 0.0) + jnp.log1p(jnp.exp(-jnp.abs(sp_in)))
    gval = -jnp.exp(alog_ref[...]) * softplus
    bval = 1.0 / (1.0 + jnp.exp(-gt))
    res = jnp.where(rowi < A_HEADS, gval, bval)
    gb_ref[...] = jnp.concatenate([res, jnp.zeros((LANES - 2 * A_HEADS, tm), F32)], axis=0).T

    outs = (q_ref, k_ref, v_ref)
    for blk in range(CONV_DIM // LANES):
        cols = slice(blk * LANES, (blk + 1) * LANES)
        ext = pbuf[0:SUBLANES + tm, cols]
        y = None
        for j in range(CONV_W):
            back = CONV_W - 1 - j
            tap = (pltpu.roll(ext, back, axis=0) if back else ext)[SUBLANES:] * cw_ref[j:j + 1, cols]
            y = tap if y is None else y + tap
        y = _silu(y)
        which, head = divmod(blk, A_HEADS)
        if which < 2:
            y = y * lax.rsqrt(jnp.sum(y * y, axis=-1, keepdims=True) + RMS_EPS)
            if which == 0:
                y = y * (A_DK ** -0.5)
        outs[which][:, head * LANES:(head + 1) * LANES] = y.astype(outs[which].dtype)

    last = pbuf[tm + tail:tm + SUBLANES, :]
    cout_ref[0] = last
    pbuf[tail:SUBLANES, :] = last


def _a_in(x2, w_main, w_gate, conv_w, conv0, alog_col, dt_col, *, batch, seq):
    tm = min(ROW_TILE, seq)
    nl = seq // tm
    tokens = batch * seq
    row = lambda b, l: (b * nl + l, 0)
    const = lambda b, l: (0, 0)
    wide = jax.ShapeDtypeStruct((tokens, A_QK), BF16)
    return pl.pallas_call(
        functools.partial(_a_in_kernel, tm=tm),
        grid=(batch, nl),
        in_specs=[
            pl.BlockSpec((tm, D_MODEL), row),
            pl.BlockSpec((D_MODEL, A_MAIN), const),
            pl.BlockSpec((D_MODEL, LANES), const),
            pl.BlockSpec((CONV_W, CONV_DIM), const),
            pl.BlockSpec((1, CONV_W - 1, CONV_DIM), lambda b, l: (b, 0, 0)),
            pl.BlockSpec((2 * A_HEADS, 1), const),
            pl.BlockSpec((2 * A_HEADS, 1), const),
        ],
        out_specs=[
            pl.BlockSpec((tm, A_QK), row),
            pl.BlockSpec((tm, A_QK), row),
            pl.BlockSpec((tm, A_VW), row),
            pl.BlockSpec((tm, A_VW), row),
            pl.BlockSpec((tm, LANES), row),
            pl.BlockSpec((1, CONV_W - 1, CONV_DIM), lambda b, l: (b, 0, 0)),
        ],
        out_shape=[wide, wide, wide, wide,
                   jax.ShapeDtypeStruct((tokens, LANES), F32),
                   jax.ShapeDtypeStruct((batch, CONV_W - 1, CONV_DIM), F32)],
        scratch_shapes=[pltpu.VMEM((tm + SUBLANES, CONV_DIM), F32)],
        compiler_params=_params(("arbitrary", "arbitrary")),
        name="a_in",
    )(x2, w_main, w_gate, conv_w, conv0, alog_col, dt_col)


def _unit_lower_inverse(mats, row, col, c):
    eye = (row == col).astype(F32)
    diag8 = (row >> 3) == (col >> 3)
    d = [jnp.where(diag8, a, 0.0) for a in mats]
    d2 = [_mm(x, x) for x in d]
    d4 = [_mm(x, x) for x in d2]
    p1 = [_mm(eye - x, eye + y) for x, y in zip(d, d2)]
    xs = [_mm(p, eye + y) for p, y in zip(p1, d4)]
    shift = 3
    while (1 << shift) < c:
        mask = ((row >> (shift + 1)) == (col >> (shift + 1))) & ((row >> shift) > (col >> shift))
        xc = [_mm(x, jnp.where(mask, a, 0.0)) for x, a in zip(xs, mats)]
        xs = [x - _mm(y, x) for x, y in zip(xs, xc)]
        shift += 1
    return xs


def _delta_kernel(q_ref, k_ref, v_ref, z_ref, gb_ref, s0_ref, nw_ref, o_ref, s_ref, *, c, nb):
    n = pl.program_id(1)

    @pl.when(n == 0)
    def _():
        s_ref[...] = s0_ref[...]

    row = lax.broadcasted_iota(jnp.int32, (c, c), 0)
    col = lax.broadcasted_iota(jnp.int32, (c, c), 1)
    incl = row >= col
    strict = row > col
    rowl = lax.broadcasted_iota(jnp.int32, (c, LANES), 0)
    nw = nw_ref[...]

    gb, gcum, gcum_t = [], [], []
    for bi in range(nb):
        g = gb_ref[bi]
        acc = g
        step = 1
        while step < c:
            acc = acc + jnp.where(rowl >= step, pltpu.roll(acc, step, axis=0), 0.0)
            step *= 2
        gb.append(g)
        gcum.append(acc)
        gcum_t.append(acc.T)

    units = [(bi, h) for bi in range(nb) for h in range(A_HEADS)]
    idx = range(len(units))
    cols = [slice(h * LANES, (h + 1) * LANES) for _, h in units]
    qb = [q_ref[bi, :, cols[i]] for i, (bi, _) in enumerate(units)]
    kb16 = [k_ref[bi, :, cols[i]] for i, (bi, _) in enumerate(units)]
    kf = [x.astype(F32) for x in kb16]
    vf = [v_ref[bi, :, cols[i]].astype(F32) for i, (bi, _) in enumerate(units)]
    s = [s_ref[bi, h] for bi, h in units]
    gc = [gcum[bi][:, h:h + 1] for bi, h in units]
    beta = [gb[bi][:, A_HEADS + h:A_HEADS + h + 1] for bi, h in units]
    glast = [gcum[bi][c - 1:c, h:h + 1] for bi, h in units]
    eg = [jnp.exp(x) for x in gc]
    decay = [jnp.exp(jnp.where(incl, gc[i] - gcum_t[bi][h:h + 1, :], -jnp.inf))
             for i, (bi, h) in enumerate(units)]
    kbeta = [kf[i] * beta[i] for i in idx]
    kq = [_mm_nt(jnp.concatenate([kbeta[i].astype(BF16), qb[i]], axis=0), kb16[i]) for i in idx]
    a = [jnp.where(strict, kq[i][:c] * decay[i], 0.0) for i in idx]
    tinv = _unit_lower_inverse(a, row, col, c)
    uw = [_mm(tinv[i], jnp.concatenate([vf[i] * beta[i], kbeta[i] * eg[i]], axis=1)) for i in idx]
    wq = [_mm(jnp.concatenate([uw[i][:, A_DV:], qb[i].astype(F32) * eg[i]], axis=0), s[i]) for i in idx]
    v_new = [uw[i][:, :A_DV] - wq[i][:c] for i in idx]
    kd_t = [(kf[i] * jnp.exp(glast[i] - gc[i])).T for i in idx]
    ov = [_mm(jnp.concatenate([kq[i][c:] * decay[i], kd_t[i]], axis=0), v_new[i]) for i in idx]
    for i, (bi, h) in enumerate(units):
        s_ref[bi, h] = s[i] * jnp.exp(glast[i]) + ov[i][c:]
        o = wq[i][c:] + ov[i][:c]
        zf = z_ref[bi, :, cols[i]].astype(F32)
        gated = o * lax.rsqrt(jnp.mean(o * o, axis=-1, keepdims=True) + RMS_EPS) * nw * _silu(zf)
        o_ref[bi, :, cols[i]] = gated.astype(o_ref.dtype)


def _delta(q, k, v, z, gb, s0, norm_w, *, batch, seq):
    c = min(CHUNK, seq)
    n = seq // c
    nb = DELTA_BATCHES
    seq3 = lambda a: a.reshape(batch, seq, a.shape[-1])
    blk = lambda b, i: (b, i, 0)
    state = lambda b, i: (b, 0, 0, 0)
    o, s_new = pl.pallas_call(
        functools.partial(_delta_kernel, c=c, nb=nb),
        grid=(batch // nb, n),
        in_specs=[
            pl.BlockSpec((nb, c, A_QK), blk),
            pl.BlockSpec((nb, c, A_QK), blk),
            pl.BlockSpec((nb, c, A_VW), blk),
            pl.BlockSpec((nb, c, A_VW), blk),
            pl.BlockSpec((nb, c, LANES), blk),
            pl.BlockSpec((nb, A_HEADS, A_DK, A_DV), state),
            pl.BlockSpec((1, LANES), lambda b, i: (0, 0)),
        ],
        out_specs=[
            pl.BlockSpec((nb, c, A_VW), blk),
            pl.BlockSpec((nb, A_HEADS, A_DK, A_DV), state),
        ],
        out_shape=[jax.ShapeDtypeStruct((batch, seq, A_VW), BF16),
                   jax.ShapeDtypeStruct((batch, A_HEADS, A_DK, A_DV), F32)],
        compiler_params=_params(("arbitrary", "arbitrary")),
        name="delta_rule",
    )(seq3(q), seq3(k), seq3(v), seq3(z), seq3(gb), s0, norm_w)
    return o.reshape(batch * seq, A_VW), s_new


def _out_ln_kernel(o_ref, x_ref, w_ref, g_ref, b_ref, y_ref):
    r = DN_ALPHA * x_ref[...] + jnp.dot(o_ref[...], w_ref[...], preferred_element_type=F32)
    mu = jnp.mean(r, axis=-1, keepdims=True)
    d = r - mu
    var = jnp.mean(d * d, axis=-1, keepdims=True)
    y_ref[...] = d * lax.rsqrt(var + LN_EPS) * g_ref[...] + b_ref[...]


def _out_ln(o, x2, w_out, ln_g, ln_b):
    tokens = x2.shape[0]
    tm = min(ROW_TILE, tokens)
    row = lambda i: (i, 0)
    const = lambda i: (0, 0)
    return pl.pallas_call(
        _out_ln_kernel,
        grid=(tokens // tm,),
        in_specs=[
            pl.BlockSpec((tm, o.shape[1]), row),
            pl.BlockSpec((tm, D_MODEL), row),
            pl.BlockSpec((o.shape[1], D_MODEL), const),
            pl.BlockSpec((1, D_MODEL), const),
            pl.BlockSpec((1, D_MODEL), const),
        ],
        out_specs=pl.BlockSpec((tm, D_MODEL), row),
        out_shape=jax.ShapeDtypeStruct((tokens, D_MODEL), F32),
        compiler_params=_params(("arbitrary",)),
        name="out_ln",
    )(o, x2, w_out, ln_g, ln_b)


def _rope_slab(x, cos_t, sin_up, sin_dn):
    half = ROPE_DIMS // 2
    return x * cos_t + pltpu.roll(x, LANES - half, axis=1) * sin_up + pltpu.roll(x, half, axis=1) * sin_dn


def _kv_kernel(h_ref, w_ref, cos_ref, sup_ref, sdn_ref, k_ref, v_ref, klast_ref, vlast_ref, *, tm, keep):
    kv = jnp.dot(h_ref[...].astype(BF16), w_ref[...], preferred_element_type=F32)
    cos_t, sin_up, sin_dn = cos_ref[...], sup_ref[...], sdn_ref[...]
    k = jnp.concatenate([_rope_slab(kv[:, s * LANES:(s + 1) * LANES], cos_t, sin_up, sin_dn)
                         for s in range(B_KVW // LANES)], axis=1)
    v = kv[:, B_KVW:]
    k_ref[...] = k.astype(k_ref.dtype)
    v_ref[...] = v.astype(v_ref.dtype)
    klast_ref[0] = k[tm - keep:, :]
    vlast_ref[0] = v[tm - keep:, :]


def _shared_kv(h2, w_kv, tables, *, batch, seq):
    tm = min(ROW_TILE, seq)
    nl = seq // tm
    keep = min(WINDOW, seq)
    assert keep <= tm
    tokens = batch * seq
    row = lambda b, l: (b * nl + l, 0)
    tab = lambda b, l: (l, 0)
    lastb = lambda b, l: (b, 0, 0)
    out = jax.ShapeDtypeStruct((tokens, B_KVW), BF16)
    last = jax.ShapeDtypeStruct((batch, keep, B_KVW), F32)
    return pl.pallas_call(
        functools.partial(_kv_kernel, tm=tm, keep=keep),
        grid=(batch, nl),
        in_specs=[
            pl.BlockSpec((tm, D_MODEL), row),
            pl.BlockSpec((D_MODEL, 2 * B_KVW), lambda b, l: (0, 0)),
            pl.BlockSpec((tm, LANES), tab),
            pl.BlockSpec((tm, LANES), tab),
            pl.BlockSpec((tm, LANES), tab),
        ],
        out_specs=[pl.BlockSpec((tm, B_KVW), row), pl.BlockSpec((tm, B_KVW), row),
                   pl.BlockSpec((1, keep, B_KVW), lastb), pl.BlockSpec((1, keep, B_KVW), lastb)],
        out_shape=[out, out, last, last],
        compiler_params=_params(("arbitrary", "arbitrary")),
        name="shared_kv",
    )(h2, w_kv, *tables)


def _b_in_kernel(x_ref, w_ref, cos_ref, sup_ref, sdn_ref, qx_ref, z_ref, *, tm):
    xb = x_ref[...].astype(BF16)
    cos_t, sin_up, sin_dn = cos_ref[...], sup_ref[...], sdn_ref[...]
    lane_half = lax.broadcasted_iota(jnp.int32, (tm, LANES), 1) >> int(math.log2(B_HD))
    cb = 512
    for j in range(B_QW // cb):
        proj = jnp.dot(xb, w_ref[:, j * cb:(j + 1) * cb], preferred_element_type=F32)
        for sl in range(cb // LANES):
            slab = j * (cb // LANES) + sl
            rot = _rope_slab(proj[:, sl * LANES:(sl + 1) * LANES], cos_t, sin_up, sin_dn) * (B_HD ** -0.5)
            for p in range(2):
                hq = 2 * slab + p
                x = jnp.where(lane_half == p, rot, 0.0)
                if p != (hq // B_GROUP) % 2:
                    x = pltpu.roll(x, B_HD, axis=1)
                qx_ref[:, hq * LANES:(hq + 1) * LANES] = x.astype(qx_ref.dtype)
    for j in range(B_QW // cb):
        z_ref[:, j * cb:(j + 1) * cb] = jnp.dot(
            xb, w_ref[:, B_QW + j * cb:B_QW + (j + 1) * cb], preferred_element_type=F32).astype(z_ref.dtype)


def _b_in(x2, w_in, tables, *, batch, seq):
    tm = min(ROW_TILE, seq)
    nl = seq // tm
    tokens = batch * seq
    row = lambda b, l: (b * nl + l, 0)
    tab = lambda b, l: (l, 0)
    return pl.pallas_call(
        functools.partial(_b_in_kernel, tm=tm),
        grid=(batch, nl),
        in_specs=[
            pl.BlockSpec((tm, D_MODEL), row),
            pl.BlockSpec((D_MODEL, 2 * B_QW), lambda b, l: (0, 0)),
            pl.BlockSpec((tm, LANES), tab),
            pl.BlockSpec((tm, LANES), tab),
            pl.BlockSpec((tm, LANES), tab),
        ],
        out_specs=[pl.BlockSpec((tm, B_QX), row), pl.BlockSpec((tm, B_QW), row)],
        out_shape=[jax.ShapeDtypeStruct((tokens, B_QX), BF16), jax.ShapeDtypeStruct((tokens, B_QW), BF16)],
        compiler_params=_params(("arbitrary", "arbitrary")),
        name="b_in",
    )(x2, w_in, *tables)


def _attn_kernel(sink_ref, qx_ref, z_ref, k_ref, v_ref, o_ref, *, lq, lk, banded):
    c = pl.program_id(1)
    if banded:
        first = jnp.maximum(c - WINDOW_CHUNKS, 0)
        start = pl.multiple_of(first * CHUNK, CHUNK)
        key_chunk = first + (lax.broadcasted_iota(jnp.int32, (lq, lk), 1) >> int(math.log2(CHUNK)))
        valid = key_chunk <= c
    else:
        start = 0
        valid = None
    lane_half = lax.broadcasted_iota(jnp.int32, (lq, LANES), 1) >> int(math.log2(B_HD))
    kv_slabs = range(B_KVW // LANES)
    kwin = [k_ref[0, pl.ds(start, lk), s * LANES:(s + 1) * LANES] for s in kv_slabs]
    vwin = [v_ref[0, pl.ds(start, lk), s * LANES:(s + 1) * LANES] for s in kv_slabs]
    groups = range(B_KV_HEADS)
    qstack = [jnp.concatenate([qx_ref[:, (B_GROUP * j + g) * LANES:(B_GROUP * j + g + 1) * LANES]
                               for g in range(B_GROUP)], axis=0) for j in groups]
    scores = [_mm_nt(qstack[j], kwin[j // 2]) for j in groups]
    pstack = []
    for j in groups:
        probs = []
        for g in range(B_GROUP):
            s = scores[j][g * lq:(g + 1) * lq]
            if valid is not None:
                s = jnp.where(valid, s, -jnp.inf)
            sk = sink_ref[B_GROUP * j + g]
            m = jnp.maximum(jnp.max(s, axis=-1, keepdims=True), sk)
            e = jnp.exp(s - m)
            probs.append((e / (jnp.sum(e, axis=-1, keepdims=True) + jnp.exp(sk - m))).astype(BF16))
        pstack.append(jnp.concatenate(probs, axis=0))
    pv = [_mm(pstack[j], vwin[j // 2]) for j in groups]
    for slab in range(B_QW // LANES):
        halves = []
        for p in range(2):
            hq = 2 * slab + p
            j, g = divmod(hq, B_GROUP)
            x = pv[j][g * lq:(g + 1) * lq]
            halves.append(pltpu.roll(x, B_HD, axis=1) if p != j % 2 else x)
        both = jnp.where(lane_half == 0, halves[0], halves[1])
        zs = z_ref[:, slab * LANES:(slab + 1) * LANES].astype(F32)
        o_ref[:, slab * LANES:(slab + 1) * LANES] = (both * _silu(zs)).astype(o_ref.dtype)


def _attention(qx, z, k3, v3, sinks, *, batch, seq, banded):
    lq = min(CHUNK, seq)
    nq = seq // lq
    ltot = k3.shape[1]
    lk = (WINDOW_CHUNKS + 1) * CHUNK if banded else ltot
    tokens = batch * seq
    row = lambda b, c: (b * nq + c, 0)
    whole = lambda b, c: (b, 0, 0)
    return pl.pallas_call(
        functools.partial(_attn_kernel, lq=lq, lk=lk, banded=banded),
        grid=(batch, nq),
        in_specs=[
            pl.BlockSpec(memory_space=pltpu.SMEM),
            pl.BlockSpec((lq, B_QX), row),
            pl.BlockSpec((lq, B_QW), row),
            pl.BlockSpec((1, ltot, B_KVW), whole),
            pl.BlockSpec((1, ltot, B_KVW), whole),
        ],
        out_specs=pl.BlockSpec((lq, B_QW), row),
        out_shape=jax.ShapeDtypeStruct((tokens, B_QW), BF16),
        compiler_params=_params(("arbitrary", "arbitrary")),
        name="swa_attention",
    )(sinks, qx, z, k3, v3)


def _rope_tables(pos):
    half = ROPE_DIMS // 2
    inv = ROPE_THETA ** (-jnp.arange(half, dtype=F32) * 2.0 / ROPE_DIMS)
    ang = pos.astype(F32)[:, None] * inv[None, :]
    cos, sin = jnp.cos(ang), jnp.sin(ang)
    ones = jnp.ones((pos.shape[0], B_HD - ROPE_DIMS), F32)
    zeros_h = jnp.zeros((pos.shape[0], half), F32)
    zeros_r = jnp.zeros((pos.shape[0], B_HD - ROPE_DIMS), F32)
    cos_head = jnp.concatenate([cos, cos, ones], axis=1)
    up_head = jnp.concatenate([-sin, zeros_h, zeros_r], axis=1)
    dn_head = jnp.concatenate([zeros_h, sin, zeros_r], axis=1)
    rep = LANES // B_HD
    return tuple(jnp.tile(t, (1, rep)) for t in (cos_head, up_head, dn_head))


def _trunk(x, pos, conv_state, delta_state, past_k, past_v, wts):
    batch, seq, _ = x.shape
    tokens = batch * seq
    h = x.reshape(tokens, D_MODEL)
    tables = _rope_tables(pos)
    new_conv, new_delta = [], []
    for i in range(N_A_LAYERS):
        q, k, v, z, gb, cbuf = _a_in(h, wts["a_w_main"][i], wts["a_w_gate"][i], wts["a_conv_w"][i], conv_state[i],
                                     wts["a_log"][i], wts["a_dt"][i], batch=batch, seq=seq)
        o, s_new = _delta(q, k, v, z, gb, delta_state[i], wts["a_norm_w"][i], batch=batch, seq=seq)
        h = _out_ln(o, h, wts["a_w_out"][i], wts["a_ln_g"][i], wts["a_ln_b"][i])
        new_conv.append(cbuf)
        new_delta.append(s_new)
    k2, v2, k_last, v_last = _shared_kv(h, wts["b_w_kv"], tables, batch=batch, seq=seq)
    k3 = k2.reshape(batch, seq, B_KVW)
    v3 = v2.reshape(batch, seq, B_KVW)
    cached = past_k is not None
    if cached:
        pk = past_k.reshape(batch, -1, B_KVW)
        pv = past_v.reshape(batch, -1, B_KVW)
        k3 = jnp.concatenate([pk.astype(BF16), k3], axis=1)
        v3 = jnp.concatenate([pv.astype(BF16), v3], axis=1)
        k_last = jnp.concatenate([pk, k_last], axis=1)[:, -WINDOW:]
        v_last = jnp.concatenate([pv, v_last], axis=1)[:, -WINDOW:]
    new_k = k_last.reshape(batch, WINDOW, B_KV_HEADS, B_HD)
    new_v = v_last.reshape(batch, WINDOW, B_KV_HEADS, B_HD)
    for j in range(N_B_LAYERS):
        qx, z = _b_in(h, wts["b_w_in"][j], tables, batch=batch, seq=seq)
        o = _attention(qx, z, k3, v3, wts["b_sinks"][j], batch=batch, seq=seq, banded=not cached)
        h = _out_ln(o, h, wts["b_w_out"][j], wts["b_ln_g"][j], wts["b_ln_b"][j])
    return h.reshape(batch, seq, D_MODEL), jnp.stack(new_conv), jnp.stack(new_delta), new_k, new_v


def kernel(x_prompt, x_sample, state_delta, state_conv, cache_k, cache_v, a_w_in, a_conv_w, a_log, a_dt_bias,
           a_norm_w, a_w_out, a_ln_g, a_ln_b, b_w_kv, b_w_in, b_sinks, b_w_out, b_ln_g, b_ln_b):
    gate_cols = a_w_in[:, :, A_MAIN:]
    zeros_h = jnp.zeros((N_A_LAYERS, A_HEADS), F32)
    wts = {
        "a_w_main": a_w_in[:, :, :A_MAIN].astype(BF16),
        "a_w_gate": jnp.pad(gate_cols, ((0, 0), (0, 0), (0, LANES - 2 * A_HEADS))).astype(BF16),
        "a_conv_w": a_conv_w,
        "a_log": jnp.concatenate([a_log.astype(F32), zeros_h], axis=1)[:, :, None],
        "a_dt": jnp.concatenate([a_dt_bias.astype(F32), zeros_h], axis=1)[:, :, None],
        "a_norm_w": a_norm_w.reshape(N_A_LAYERS, 1, A_DV),
        "a_w_out": a_w_out.astype(BF16),
        "a_ln_g": a_ln_g.reshape(N_A_LAYERS, 1, D_MODEL),
        "a_ln_b": a_ln_b.reshape(N_A_LAYERS, 1, D_MODEL),
        "b_w_kv": b_w_kv.astype(BF16),
        "b_w_in": b_w_in.astype(BF16),
        "b_sinks": b_sinks,
        "b_w_out": b_w_out.astype(BF16),
        "b_ln_g": b_ln_g.reshape(N_B_LAYERS, 1, D_MODEL),
        "b_ln_b": b_ln_b.reshape(N_B_LAYERS, 1, D_MODEL),
    }
    bp, lp, _ = x_prompt.shape
    bs, ls, _ = x_sample.shape
    pos_prompt = jnp.arange(lp, dtype=jnp.int32)
    pos_sample = PAST_LEN + jnp.arange(ls, dtype=jnp.int32)
    zero_conv = jnp.zeros((N_A_LAYERS, bp, CONV_W - 1, CONV_DIM), F32)
    zero_delta = jnp.zeros((N_A_LAYERS, bp, A_HEADS, A_DK, A_DV), F32)
    y_p, p_conv, p_delta, p_k, p_v = _trunk(x_prompt, pos_prompt, zero_conv, zero_delta, None, None, wts)
    y_s, s_conv, s_delta, s_k, s_v = _trunk(x_sample, pos_sample, state_conv, state_delta, cache_k, cache_v, wts)
    return (y_p, y_s, p_delta, p_conv, p_k, p_v, s_delta, s_conv, s_k, s_v)
```

```python
import functools
import math

import jax
import jax.numpy as jnp
from jax import lax
from jax.experimental import pallas as pl
from jax.experimental.pallas import tpu as pltpu

D_MODEL = 1024
DEPTH = 4
PAST_LEN = 4096
CHUNK = 64
N_A_LAYERS = DEPTH // 2
N_B_LAYERS = DEPTH - N_A_LAYERS
A_HEADS = 8
A_DK = 128
A_DV = 128
A_QK = A_HEADS * A_DK
A_VW = A_HEADS * A_DV
CONV_W = 4
CONV_DIM = 2 * A_QK + A_VW
A_MAIN = CONV_DIM + A_VW
B_Q_HEADS = 16
B_KV_HEADS = 4
B_GROUP = B_Q_HEADS // B_KV_HEADS
B_HD = 64
B_QW = B_Q_HEADS * B_HD
B_KVW = B_KV_HEADS * B_HD
WINDOW = 128
WINDOW_CHUNKS = WINDOW // CHUNK
ROPE_DIMS = B_HD // 4
ROPE_THETA = 500000.0
DN_ALPHA = (2 * DEPTH) ** 0.25
LN_EPS = 1e-5
RMS_EPS = 1e-6

LANES = 128
SUBLANES = 8
VMEM_LIMIT = 48 * 1024 * 1024
ROW_TILE = 256
WIDE_TILE = 512
DELTA_BATCHES = 2
ATTN_CHUNKS = 4
B_QX = B_Q_HEADS * LANES

F32 = jnp.float32
BF16 = jnp.bfloat16


def _mm(a, b):
    return jnp.dot(a.astype(BF16), b.astype(BF16), preferred_element_type=F32)


def _mm_nt(a, b):
    return lax.dot_general(a.astype(BF16), b.astype(BF16), (((1,), (1,)), ((), ())),
                           preferred_element_type=F32)


def _silu(x):
    return x * (1.0 / (1.0 + jnp.exp(-x)))


def _params(semantics):
    return pltpu.CompilerParams(dimension_semantics=semantics, vmem_limit_bytes=VMEM_LIMIT)


def _a_in_kernel(x_ref, w_ref, wg_ref, cw_ref, c0_ref, alog_ref, dt_ref,
                 q_ref, k_ref, v_ref, z_ref, gb_ref, cout_ref, pbuf, *, tm):
    l = pl.program_id(1)
    tail = SUBLANES - (CONV_W - 1)

    @pl.when(l == 0)
    def _():
        pbuf[tail:SUBLANES, :] = c0_ref[0]

    xb = x_ref[...].astype(BF16)
    cb = 512
    for j in range(CONV_DIM // cb):
        pbuf[SUBLANES:SUBLANES + tm, j * cb:(j + 1) * cb] = jnp.dot(
            xb, w_ref[:, j * cb:(j + 1) * cb], preferred_element_type=F32)
    for j in range(A_VW // cb):
        z_ref[:, j * cb:(j + 1) * cb] = jnp.dot(
            xb, w_ref[:, CONV_DIM + j * cb:CONV_DIM + (j + 1) * cb], preferred_element_type=F32).astype(z_ref.dtype)

    gates = jnp.dot(xb, wg_ref[...], preferred_element_type=F32)
    gt = gates.T[:2 * A_HEADS]
    rowi = lax.broadcasted_iota(jnp.int32, gt.shape, 0)
    sp_in = gt + dt_ref[...]
    softplus = jnp.maximum(sp_in, 0.0) + jnp.log1p(jnp.exp(-jnp.abs(sp_in)))
    gval = -jnp.exp(alog_ref[...]) * softplus
    bval = 1.0 / (1.0 + jnp.exp(-gt))
    res = jnp.where(rowi < A_HEADS, gval, bval)
    gb_ref[...] = jnp.concatenate([res, jnp.zeros((LANES - 2 * A_HEADS, tm), F32)], axis=0).T

    outs = (q_ref, k_ref, v_ref)
    for blk in range(CONV_DIM // LANES):
        cols = slice(blk * LANES, (blk + 1) * LANES)
        ext = pbuf[0:SUBLANES + tm, cols]
        y = None
        for j in range(CONV_W):
            back = CONV_W - 1 - j
            tap = (pltpu.roll(ext, back, axis=0) if back else ext)[SUBLANES:] * cw_ref[j:j + 1, cols]
            y = tap if y is None else y + tap
        y = _silu(y)
        which, head = divmod(blk, A_HEADS)
        if which < 2:
            y = y * lax.rsqrt(jnp.sum(y * y, axis=-1, keepdims=True) + RMS_EPS)
            if which == 0:
                y = y * (A_DK ** -0.5)
        outs[which][:, head * LANES:(head + 1) * LANES] = y.astype(outs[which].dtype)

    last = pbuf[tm + tail:tm + SUBLANES, :]
    cout_ref[0] = last
    pbuf[tail:SUBLANES, :] = last


def _a_in(x2, w_main, w_gate, conv_w, conv0, alog_col, dt_col, *, batch, seq):
    tm = min(ROW_TILE, seq)
    nl = seq // tm
    tokens = batch * seq
    row = lambda b, l: (b * nl + l, 0)
    const = lambda b, l: (0, 0)
    wide = jax.ShapeDtypeStruct((tokens, A_QK), BF16)
    return pl.pallas_call(
        functools.partial(_a_in_kernel, tm=tm),
        grid=(batch, nl),
        in_specs=[
            pl.BlockSpec((tm, D_MODEL), row),
            pl.BlockSpec((D_MODEL, A_MAIN), const),
            pl.BlockSpec((D_MODEL, LANES), const),
            pl.BlockSpec((CONV_W, CONV_DIM), const),
            pl.BlockSpec((1, CONV_W - 1, CONV_DIM), lambda b, l: (b, 0, 0)),
            pl.BlockSpec((2 * A_HEADS, 1), const),
            pl.BlockSpec((2 * A_HEADS, 1), const),
        ],
        out_specs=[
            pl.BlockSpec((tm, A_QK), row),
            pl.BlockSpec((tm, A_QK), row),
            pl.BlockSpec((tm, A_VW), row),
            pl.BlockSpec((tm, A_VW), row),
            pl.BlockSpec((tm, LANES), row),
            pl.BlockSpec((1, CONV_W - 1, CONV_DIM), lambda b, l: (b, 0, 0)),
        ],
        out_shape=[wide, wide, wide, wide,
                   jax.ShapeDtypeStruct((tokens, LANES), F32),
                   jax.ShapeDtypeStruct((batch, CONV_W - 1, CONV_DIM), F32)],
        scratch_shapes=[pltpu.VMEM((tm + SUBLANES, CONV_DIM), F32)],
        compiler_params=_params(("arbitrary", "arbitrary")),
        name="a_in",
    )(x2, w_main, w_gate, conv_w, conv0, alog_col, dt_col)


def _unit_lower_inverse(mats, row, col, c):
    eye = (row == col).astype(F32)
    diag8 = (row >> 3) == (col >> 3)
    d = [jnp.where(diag8, a, 0.0) for a in mats]
    d2 = [_mm(x, x) for x in d]
    d4 = [_mm(x, x) for x in d2]
    p1 = [_mm(eye - x, eye + y) for x, y in zip(d, d2)]
    xs = [_mm(p, eye + y) for p, y in zip(p1, d4)]
    shift = 3
    while (1 << shift) < c:
        mask = ((row >> (shift + 1)) == (col >> (shift + 1))) & ((row >> shift) > (col >> shift))
        xc = [_mm(x, jnp.where(mask, a, 0.0)) for x, a in zip(xs, mats)]
        xs = [x - _mm(y, x) for x, y in zip(xs, xc)]
        shift += 1
    return xs


def _delta_kernel(q_ref, k_ref, v_ref, z_ref, gb_ref, s0_ref, nw_ref, o_ref, s_ref, *, c, nb):
    n = pl.program_id(1)

    @pl.when(n == 0)
    def _():
        s_ref[...] = s0_ref[...]

    row = lax.broadcasted_iota(jnp.int32, (c, c), 0)
    col = lax.broadcasted_iota(jnp.int32, (c, c), 1)
    incl = row >= col
    strict = row > col
    rowl = lax.broadcasted_iota(jnp.int32, (c, LANES), 0)
    nw = nw_ref[...]

    gb, gcum, gcum_t = [], [], []
    for bi in range(nb):
        g = gb_ref[bi]
        acc = g
        step = 1
        while step < c:
            acc = acc + jnp.where(rowl >= step, pltpu.roll(acc, step, axis=0), 0.0)
            step *= 2
        gb.append(g)
        gcum.append(acc)
        gcum_t.append(acc.T)

    units = [(bi, h) for bi in range(nb) for h in range(A_HEADS)]
    idx = range(len(units))
    cols = [slice(h * LANES, (h + 1) * LANES) for _, h in units]
    qb = [q_ref[bi, :, cols[i]] for i, (bi, _) in enumerate(units)]
    kb16 = [k_ref[bi, :, cols[i]] for i, (bi, _) in enumerate(units)]
    kf = [x.astype(F32) for x in kb16]
    vf = [v_ref[bi, :, cols[i]].astype(F32) for i, (bi, _) in enumerate(units)]
    s = [s_ref[bi, h] for bi, h in units]
    gc = [gcum[bi][:, h:h + 1] for bi, h in units]
    beta = [gb[bi][:, A_HEADS + h:A_HEADS + h + 1] for bi, h in units]
    glast = [gcum[bi][c - 1:c, h:h + 1] for bi, h in units]
    eg = [jnp.exp(x) for x in gc]
    decay = [jnp.exp(jnp.where(incl, gc[i] - gcum_t[bi][h:h + 1, :], -jnp.inf))
             for i, (bi, h) in enumerate(units)]
    kbeta = [kf[i] * beta[i] for i in idx]
    kq = [_mm_nt(jnp.concatenate([kbeta[i].astype(BF16), qb[i]], axis=0), kb16[i]) for i in idx]
    a = [jnp.where(strict, kq[i][:c] * decay[i], 0.0) for i in idx]
    tinv = _unit_lower_inverse(a, row, col, c)
    uw = [_mm(tinv[i], jnp.concatenate([vf[i] * beta[i], kbeta[i] * eg[i]], axis=1)) for i in idx]
    wq = [_mm(jnp.concatenate([uw[i][:, A_DV:], qb[i].astype(F32) * eg[i]], axis=0), s[i]) for i in idx]
    v_new = [uw[i][:, :A_DV] - wq[i][:c] for i in idx]
    kd_t = [(kf[i] * jnp.exp(glast[i] - gc[i])).T for i in idx]
    ov = [_mm(jnp.concatenate([kq[i][c:] * decay[i], kd_t[i]], axis=0), v_new[i]) for i in idx]
    for i, (bi, h) in enumerate(units):
        s_ref[bi, h] = s[i] * jnp.exp(glast[i]) + ov[i][c:]
        o = wq[i][c:] + ov[i][:c]
        zf = z_ref[bi, :, cols[i]].astype(F32)
        gated = o * lax.rsqrt(jnp.mean(o * o, axis=-1, keepdims=True) + RMS_EPS) * nw * _silu(zf)
        o_ref[bi, :, cols[i]] = gated.astype(o_ref.dtype)


def _delta(q, k, v, z, gb, s0, norm_w, *, batch, seq):
    c = min(CHUNK, seq)
    n = seq // c
    nb = DELTA_BATCHES
    seq3 = lambda a: a.reshape(batch, seq, a.shape[-1])
    blk = lambda b, i: (b, i, 0)
    state = lambda b, i: (b, 0, 0, 0)
    o, s_new = pl.pallas_call(
        functools.partial(_delta_kernel, c=c, nb=nb),
        grid=(batch // nb, n),
        in_specs=[
            pl.BlockSpec((nb, c, A_QK), blk),
            pl.BlockSpec((nb, c, A_QK), blk),
            pl.BlockSpec((nb, c, A_VW), blk),
            pl.BlockSpec((nb, c, A_VW), blk),
            pl.BlockSpec((nb, c, LANES), blk),
            pl.BlockSpec((nb, A_HEADS, A_DK, A_DV), state),
            pl.BlockSpec((1, LANES), lambda b, i: (0, 0)),
        ],
        out_specs=[
            pl.BlockSpec((nb, c, A_VW), blk),
            pl.BlockSpec((nb, A_HEADS, A_DK, A_DV), state),
        ],
        out_shape=[jax.ShapeDtypeStruct((batch, seq, A_VW), BF16),
                   jax.ShapeDtypeStruct((batch, A_HEADS, A_DK, A_DV), F32)],
        compiler_params=_params(("arbitrary", "arbitrary")),
        name="delta_rule",
    )(seq3(q), seq3(k), seq3(v), seq3(z), seq3(gb), s0, norm_w)
    return o.reshape(batch * seq, A_VW), s_new


def _out_ln_kernel(o_ref, x_ref, w_ref, g_ref, b_ref, y_ref):
    r = DN_ALPHA * x_ref[...] + jnp.dot(o_ref[...], w_ref[...], preferred_element_type=F32)
    mu = jnp.mean(r, axis=-1, keepdims=True)
    d = r - mu
    var = jnp.mean(d * d, axis=-1, keepdims=True)
    y_ref[...] = d * lax.rsqrt(var + LN_EPS) * g_ref[...] + b_ref[...]


def _out_ln(o, x2, w_out, ln_g, ln_b):
    tokens = x2.shape[0]
    tm = min(WIDE_TILE, tokens)
    row = lambda i: (i, 0)
    const = lambda i: (0, 0)
    return pl.pallas_call(
        _out_ln_kernel,
        grid=(tokens // tm,),
        in_specs=[
            pl.BlockSpec((tm, o.shape[1]), row),
            pl.BlockSpec((tm, D_MODEL), row),
            pl.BlockSpec((o.shape[1], D_MODEL), const),
            pl.BlockSpec((1, D_MODEL), const),
            pl.BlockSpec((1, D_MODEL), const),
        ],
        out_specs=pl.BlockSpec((tm, D_MODEL), row),
        out_shape=jax.ShapeDtypeStruct((tokens, D_MODEL), F32),
        compiler_params=_params(("arbitrary",)),
        name="out_ln",
    )(o, x2, w_out, ln_g, ln_b)


def _rope_slab(x, cos_t, sin_up, sin_dn):
    half = ROPE_DIMS // 2
    return x * cos_t + pltpu.roll(x, LANES - half, axis=1) * sin_up + pltpu.roll(x, half, axis=1) * sin_dn


def _kv_kernel(h_ref, w_ref, cos_ref, sup_ref, sdn_ref, k_ref, v_ref, klast_ref, vlast_ref, *, tm, keep):
    kv = jnp.dot(h_ref[...].astype(BF16), w_ref[...], preferred_element_type=F32)
    cos_t, sin_up, sin_dn = cos_ref[...], sup_ref[...], sdn_ref[...]
    k = jnp.concatenate([_rope_slab(kv[:, s * LANES:(s + 1) * LANES], cos_t, sin_up, sin_dn)
                         for s in range(B_KVW // LANES)], axis=1)
    v = kv[:, B_KVW:]
    k_ref[...] = k.astype(k_ref.dtype)
    v_ref[...] = v.astype(v_ref.dtype)
    klast_ref[0] = k[tm - keep:, :]
    vlast_ref[0] = v[tm - keep:, :]


def _shared_kv(h2, w_kv, tables, *, batch, seq):
    tm = min(WIDE_TILE, seq)
    nl = seq // tm
    keep = min(WINDOW, seq)
    assert keep <= tm
    tokens = batch * seq
    row = lambda b, l: (b * nl + l, 0)
    tab = lambda b, l: (l, 0)
    lastb = lambda b, l: (b, 0, 0)
    out = jax.ShapeDtypeStruct((tokens, B_KVW), BF16)
    last = jax.ShapeDtypeStruct((batch, keep, B_KVW), F32)
    return pl.pallas_call(
        functools.partial(_kv_kernel, tm=tm, keep=keep),
        grid=(batch, nl),
        in_specs=[
            pl.BlockSpec((tm, D_MODEL), row),
            pl.BlockSpec((D_MODEL, 2 * B_KVW), lambda b, l: (0, 0)),
            pl.BlockSpec((tm, LANES), tab),
            pl.BlockSpec((tm, LANES), tab),
            pl.BlockSpec((tm, LANES), tab),
        ],
        out_specs=[pl.BlockSpec((tm, B_KVW), row), pl.BlockSpec((tm, B_KVW), row),
                   pl.BlockSpec((1, keep, B_KVW), lastb), pl.BlockSpec((1, keep, B_KVW), lastb)],
        out_shape=[out, out, last, last],
        compiler_params=_params(("arbitrary", "arbitrary")),
        name="shared_kv",
    )(h2, w_kv, *tables)


def _b_in_kernel(x_ref, w_ref, cos_ref, sup_ref, sdn_ref, qx_ref, z_ref, *, tm):
    xb = x_ref[...].astype(BF16)
    cos_t, sin_up, sin_dn = cos_ref[...], sup_ref[...], sdn_ref[...]
    lane_half = lax.broadcasted_iota(jnp.int32, (tm, LANES), 1) >> int(math.log2(B_HD))
    cb = 512
    for j in range(B_QW // cb):
        proj = jnp.dot(xb, w_ref[:, j * cb:(j + 1) * cb], preferred_element_type=F32)
        for sl in range(cb // LANES):
            slab = j * (cb // LANES) + sl
            rot = _rope_slab(proj[:, sl * LANES:(sl + 1) * LANES], cos_t, sin_up, sin_dn) * (B_HD ** -0.5)
            for p in range(2):
                hq = 2 * slab + p
                x = jnp.where(lane_half == p, rot, 0.0)
                if p != (hq // B_GROUP) % 2:
                    x = pltpu.roll(x, B_HD, axis=1)
                qx_ref[:, hq * LANES:(hq + 1) * LANES] = x.astype(qx_ref.dtype)
    for j in range(B_QW // cb):
        z_ref[:, j * cb:(j + 1) * cb] = jnp.dot(
            xb, w_ref[:, B_QW + j * cb:B_QW + (j + 1) * cb], preferred_element_type=F32).astype(z_ref.dtype)


def _b_in(x2, w_in, tables, *, batch, seq):
    tm = min(WIDE_TILE, seq)
    nl = seq // tm
    tokens = batch * seq
    row = lambda b, l: (b * nl + l, 0)
    tab = lambda b, l: (l, 0)
    return pl.pallas_call(
        functools.partial(_b_in_kernel, tm=tm),
        grid=(batch, nl),
        in_specs=[
            pl.BlockSpec((tm, D_MODEL), row),
            pl.BlockSpec((D_MODEL, 2 * B_QW), lambda b, l: (0, 0)),
            pl.BlockSpec((tm, LANES), tab),
            pl.BlockSpec((tm, LANES), tab),
            pl.BlockSpec((tm, LANES), tab),
        ],
        out_specs=[pl.BlockSpec((tm, B_QX), row), pl.BlockSpec((tm, B_QW), row)],
        out_shape=[jax.ShapeDtypeStruct((tokens, B_QX), BF16), jax.ShapeDtypeStruct((tokens, B_QW), BF16)],
        compiler_params=_params(("arbitrary", "arbitrary")),
        name="b_in",
    )(x2, w_in, *tables)


def _attn_kernel(sink_ref, qx_ref, z_ref, k_ref, v_ref, o_ref, *, lq, lk, nc, banded):
    step = pl.program_id(1)
    lane_half = lax.broadcasted_iota(jnp.int32, (lq, LANES), 1) >> int(math.log2(B_HD))
    kv_slabs = range(B_KVW // LANES)
    kwin, vwin, valid = [], [], []
    for ci in range(nc):
        if banded:
            c = step * nc + ci
            first = jnp.maximum(c - WINDOW_CHUNKS, 0)
            start = pl.multiple_of(first * CHUNK, CHUNK)
            key_chunk = first + (lax.broadcasted_iota(jnp.int32, (lq, lk), 1) >> int(math.log2(CHUNK)))
            valid.append(key_chunk <= c)
        else:
            start = 0
            valid.append(None)
        kwin.append([k_ref[0, pl.ds(start, lk), s * LANES:(s + 1) * LANES] for s in kv_slabs])
        vwin.append([v_ref[0, pl.ds(start, lk), s * LANES:(s + 1) * LANES] for s in kv_slabs])
    units = [(ci, j) for ci in range(nc) for j in range(B_KV_HEADS)]
    qstack = [jnp.concatenate([qx_ref[ci * lq:(ci + 1) * lq, (B_GROUP * j + g) * LANES:(B_GROUP * j + g + 1) * LANES]
                               for g in range(B_GROUP)], axis=0) for ci, j in units]
    scores = [_mm_nt(qstack[u], kwin[ci][j // 2]) for u, (ci, j) in enumerate(units)]
    heads = [(u, ci, j, g) for u, (ci, j) in enumerate(units) for g in range(B_GROUP)]
    sk = [sink_ref[B_GROUP * j + g] for _, _, j, g in heads]
    sc = [scores[u][g * lq:(g + 1) * lq] for u, _, _, g in heads]
    sc = [s if valid[ci] is None else jnp.where(valid[ci], s, -jnp.inf) for s, (_, ci, _, _) in zip(sc, heads)]
    mx = [jnp.maximum(jnp.max(s, axis=-1, keepdims=True), k) for s, k in zip(sc, sk)]
    ex = [jnp.exp(s - m) for s, m in zip(sc, mx)]
    den = [jnp.sum(e, axis=-1, keepdims=True) + jnp.exp(k - m) for e, k, m in zip(ex, sk, mx)]
    pr = [(e / d).astype(BF16) for e, d in zip(ex, den)]
    pstack = [jnp.concatenate(pr[u * B_GROUP:(u + 1) * B_GROUP], axis=0) for u in range(len(units))]
    pv = [_mm(pstack[u], vwin[ci][j // 2]) for u, (ci, j) in enumerate(units)]
    for ci in range(nc):
        rows = slice(ci * lq, (ci + 1) * lq)
        for slab in range(B_QW // LANES):
            halves = []
            for p in range(2):
                hq = 2 * slab + p
                j, g = divmod(hq, B_GROUP)
                x = pv[ci * B_KV_HEADS + j][g * lq:(g + 1) * lq]
                halves.append(pltpu.roll(x, B_HD, axis=1) if p != j % 2 else x)
            both = jnp.where(lane_half == 0, halves[0], halves[1])
            zs = z_ref[rows, slab * LANES:(slab + 1) * LANES].astype(F32)
            o_ref[rows, slab * LANES:(slab + 1) * LANES] = (both * _silu(zs)).astype(o_ref.dtype)


def _attention(qx, z, k3, v3, sinks, *, batch, seq, banded):
    lq = min(CHUNK, seq)
    nc = min(ATTN_CHUNKS, seq // lq)
    nq = seq // (lq * nc)
    ltot = k3.shape[1]
    lk = (WINDOW_CHUNKS + 1) * CHUNK if banded else ltot
    tokens = batch * seq
    row = lambda b, c: (b * nq + c, 0)
    whole = lambda b, c: (b, 0, 0)
    return pl.pallas_call(
        functools.partial(_attn_kernel, lq=lq, lk=lk, nc=nc, banded=banded),
        grid=(batch, nq),
        in_specs=[
            pl.BlockSpec(memory_space=pltpu.SMEM),
            pl.BlockSpec((nc * lq, B_QX), row),
            pl.BlockSpec((nc * lq, B_QW), row),
            pl.BlockSpec((1, ltot, B_KVW), whole),
            pl.BlockSpec((1, ltot, B_KVW), whole),
        ],
        out_specs=pl.BlockSpec((nc * lq, B_QW), row),
        out_shape=jax.ShapeDtypeStruct((tokens, B_QW), BF16),
        compiler_params=_params(("arbitrary", "arbitrary")),
        name="swa_attention",
    )(sinks, qx, z, k3, v3)


def _rope_tables(pos):
    half = ROPE_DIMS // 2
    inv = ROPE_THETA ** (-jnp.arange(half, dtype=F32) * 2.0 / ROPE_DIMS)
    ang = pos.astype(F32)[:, None] * inv[None, :]
    cos, sin = jnp.cos(ang), jnp.sin(ang)
    ones = jnp.ones((pos.shape[0], B_HD - ROPE_DIMS), F32)
    zeros_h = jnp.zeros((pos.shape[0], half), F32)
    zeros_r = jnp.zeros((pos.shape[0], B_HD - ROPE_DIMS), F32)
    cos_head = jnp.concatenate([cos, cos, ones], axis=1)
    up_head = jnp.concatenate([-sin, zeros_h, zeros_r], axis=1)
    dn_head = jnp.concatenate([zeros_h, sin, zeros_r], axis=1)
    rep = LANES // B_HD
    return tuple(jnp.tile(t, (1, rep)) for t in (cos_head, up_head, dn_head))


def _trunk(x, pos, conv_state, delta_state, past_k, past_v, wts):
    batch, seq, _ = x.shape
    tokens = batch * seq
    h = x.reshape(tokens, D_MODEL)
    tables = _rope_tables(pos)
    new_conv, new_delta = [], []
    for i in range(N_A_LAYERS):
        q, k, v, z, gb, cbuf = _a_in(h, wts["a_w_main"][i], wts["a_w_gate"][i], wts["a_conv_w"][i], conv_state[i],
                                     wts["a_log"][i], wts["a_dt"][i], batch=batch, seq=seq)
        o, s_new = _delta(q, k, v, z, gb, delta_state[i], wts["a_norm_w"][i], batch=batch, seq=seq)
        h = _out_ln(o, h, wts["a_w_out"][i], wts["a_ln_g"][i], wts["a_ln_b"][i])
        new_conv.append(cbuf)
        new_delta.append(s_new)
    k2, v2, k_last, v_last = _shared_kv(h, wts["b_w_kv"], tables, batch=batch, seq=seq)
    k3 = k2.reshape(batch, seq, B_KVW)
    v3 = v2.reshape(batch, seq, B_KVW)
    cached = past_k is not None
    if cached:
        pk = past_k.reshape(batch, -1, B_KVW)
        pv = past_v.reshape(batch, -1, B_KVW)
        k3 = jnp.concatenate([pk.astype(BF16), k3], axis=1)
        v3 = jnp.concatenate([pv.astype(BF16), v3], axis=1)
        k_last = jnp.concatenate([pk, k_last], axis=1)[:, -WINDOW:]
        v_last = jnp.concatenate([pv, v_last], axis=1)[:, -WINDOW:]
    new_k = k_last.reshape(batch, WINDOW, B_KV_HEADS, B_HD)
    new_v = v_last.reshape(batch, WINDOW, B_KV_HEADS, B_HD)
    for j in range(N_B_LAYERS):
        qx, z = _b_in(h, wts["b_w_in"][j], tables, batch=batch, seq=seq)
        o = _attention(qx, z, k3, v3, wts["b_sinks"][j], batch=batch, seq=seq, banded=not cached)
        h = _out_ln(o, h, wts["b_w_out"][j], wts["b_ln_g"][j], wts["b_ln_b"][j])
    return h.reshape(batch, seq, D_MODEL), jnp.stack(new_conv), jnp.stack(new_delta), new_k, new_v


def kernel(x_prompt, x_sample, state_delta, state_conv, cache_k, cache_v, a_w_in, a_conv_w, a_log, a_dt_bias,
           a_norm_w, a_w_out, a_ln_g, a_ln_b, b_w_kv, b_w_in, b_sinks, b_w_out, b_ln_g, b_ln_b):
    gate_cols = a_w_in[:, :, A_MAIN:]
    zeros_h = jnp.zeros((N_A_LAYERS, A_HEADS), F32)
    wts = {
        "a_w_main": a_w_in[:, :, :A_MAIN].astype(BF16),
        "a_w_gate": jnp.pad(gate_cols, ((0, 0), (0, 0), (0, LANES - 2 * A_HEADS))).astype(BF16),
        "a_conv_w": a_conv_w,
        "a_log": jnp.concatenate([a_log.astype(F32), zeros_h], axis=1)[:, :, None],
        "a_dt": jnp.concatenate([a_dt_bias.astype(F32), zeros_h], axis=1)[:, :, None],
        "a_norm_w": a_norm_w.reshape(N_A_LAYERS, 1, A_DV),
        "a_w_out": a_w_out.astype(BF16),
        "a_ln_g": a_ln_g.reshape(N_A_LAYERS, 1, D_MODEL),
        "a_ln_b": a_ln_b.reshape(N_A_LAYERS, 1, D_MODEL),
        "b_w_kv": b_w_kv.astype(BF16),
        "b_w_in": b_w_in.astype(BF16),
        "b_sinks": b_sinks,
        "b_w_out": b_w_out.astype(BF16),
        "b_ln_g": b_ln_g.reshape(N_B_LAYERS, 1, D_MODEL),
        "b_ln_b": b_ln_b.reshape(N_B_LAYERS, 1, D_MODEL),
    }
    bp, lp, _ = x_prompt.shape
    bs, ls, _ = x_sample.shape
    pos_prompt = jnp.arange(lp, dtype=jnp.int32)
    pos_sample = PAST_LEN + jnp.arange(ls, dtype=jnp.int32)
    zero_conv = jnp.zeros((N_A_LAYERS, bp, CONV_W - 1, CONV_DIM), F32)
    zero_delta = jnp.zeros((N_A_LAYERS, bp, A_HEADS, A_DK, A_DV), F32)
    y_p, p_conv, p_delta, p_k, p_v = _trunk(x_prompt, pos_prompt, zero_conv, zero_delta, None, None, wts)
    y_s, s_conv, s_delta, s_k, s_v = _trunk(x_sample, pos_sample, state_conv, state_delta, cache_k, cache_v, wts)
    return (y_p, y_s, p_delta, p_conv, p_k, p_v, s_delta, s_conv, s_k, s_v)
```

```python
import functools
import math

import jax
import jax.numpy as jnp
from jax import lax
from jax.experimental import pallas as pl
from jax.experimental.pallas import tpu as pltpu

D_MODEL = 1024
DEPTH = 4
PAST_LEN = 4096
CHUNK = 64
N_A_LAYERS = DEPTH // 2
N_B_LAYERS = DEPTH - N_A_LAYERS
A_HEADS = 8
A_DK = 128
A_DV = 128
A_QK = A_HEADS * A_DK
A_VW = A_HEADS * A_DV
CONV_W = 4
CONV_DIM = 2 * A_QK + A_VW
A_MAIN = CONV_DIM + A_VW
B_Q_HEADS = 16
B_KV_HEADS = 4
B_GROUP = B_Q_HEADS // B_KV_HEADS
B_HD = 64
B_QW = B_Q_HEADS * B_HD
B_KVW = B_KV_HEADS * B_HD
WINDOW = 128
WINDOW_CHUNKS = WINDOW // CHUNK
ROPE_DIMS = B_HD // 4
ROPE_THETA = 500000.0
DN_ALPHA = (2 * DEPTH) ** 0.25
LN_EPS = 1e-5
RMS_EPS = 1e-6

LANES = 128
SUBLANES = 8
VMEM_LIMIT = 48 * 1024 * 1024
ROW_TILE = 256
WIDE_TILE = 512
OUT_TILE = 1024
COL_BLOCK = 512
DELTA_BATCHES = 2
DELTA_LAG = 0
ATTN_CHUNKS = 4
B_QX = B_Q_HEADS * LANES

F32 = jnp.float32
BF16 = jnp.bfloat16


def _mm(a, b):
    return jnp.dot(a.astype(BF16), b.astype(BF16), preferred_element_type=F32)


def _mm_nt(a, b):
    return lax.dot_general(a.astype(BF16), b.astype(BF16), (((1,), (1,)), ((), ())),
                           preferred_element_type=F32)


def _silu(x):
    return x * (1.0 / (1.0 + jnp.exp(-x)))


def _params(semantics):
    return pltpu.CompilerParams(dimension_semantics=semantics, vmem_limit_bytes=VMEM_LIMIT)


def _a_in_kernel(x_ref, w_ref, wg_ref, cw_ref, c0_ref, alog_ref, dt_ref,
                 q_ref, k_ref, v_ref, z_ref, gb_ref, cout_ref, pbuf, *, tm):
    l = pl.program_id(1)
    tail = SUBLANES - (CONV_W - 1)

    @pl.when(l == 0)
    def _():
        pbuf[tail:SUBLANES, :] = c0_ref[0]

    xb = x_ref[...].astype(BF16)
    cb = COL_BLOCK
    for j in range(CONV_DIM // cb):
        pbuf[SUBLANES:SUBLANES + tm, j * cb:(j + 1) * cb] = jnp.dot(xb, w_ref[j], preferred_element_type=F32)
    for j in range(A_VW // cb):
        z_ref[:, j * cb:(j + 1) * cb] = jnp.dot(
            xb, w_ref[CONV_DIM // cb + j], preferred_element_type=F32).astype(z_ref.dtype)

    gates = jnp.dot(xb, wg_ref[...], preferred_element_type=F32)
    gt = gates.T[:2 * A_HEADS]
    rowi = lax.broadcasted_iota(jnp.int32, gt.shape, 0)
    sp_in = gt + dt_ref[...]
    softplus = jnp.maximum(sp_in, 0.0) + jnp.log1p(jnp.exp(-jnp.abs(sp_in)))
    gval = -jnp.exp(alog_ref[...]) * softplus
    bval = 1.0 / (1.0 + jnp.exp(-gt))
    res = jnp.where(rowi < A_HEADS, gval, bval)
    gb_ref[...] = jnp.concatenate([res, jnp.zeros((LANES - 2 * A_HEADS, tm), F32)], axis=0).T

    outs = (q_ref, k_ref, v_ref)
    for blk in range(CONV_DIM // LANES):
        cols = slice(blk * LANES, (blk + 1) * LANES)
        ext = pbuf[0:SUBLANES + tm, cols]
        y = None
        for j in range(CONV_W):
            back = CONV_W - 1 - j
            tap = (pltpu.roll(ext, back, axis=0) if back else ext)[SUBLANES:] * cw_ref[j:j + 1, cols]
            y = tap if y is None else y + tap
        y = _silu(y)
        which, head = divmod(blk, A_HEADS)
        if which < 2:
            y = y * lax.rsqrt(jnp.sum(y * y, axis=-1, keepdims=True) + RMS_EPS)
            if which == 0:
                y = y * (A_DK ** -0.5)
        outs[which][:, head * LANES:(head + 1) * LANES] = y.astype(outs[which].dtype)

    last = pbuf[tm + tail:tm + SUBLANES, :]
    cout_ref[0] = last
    pbuf[tail:SUBLANES, :] = last


def _a_in(x2, w_main, w_gate, conv_w, conv0, alog_col, dt_col, *, batch, seq):
    tm = min(ROW_TILE, seq)
    nl = seq // tm
    tokens = batch * seq
    row = lambda b, l: (b * nl + l, 0)
    const = lambda b, l: (0, 0)
    wide = jax.ShapeDtypeStruct((tokens, A_QK), BF16)
    return pl.pallas_call(
        functools.partial(_a_in_kernel, tm=tm),
        grid=(batch, nl),
        in_specs=[
            pl.BlockSpec((tm, D_MODEL), row),
            pl.BlockSpec((A_MAIN // COL_BLOCK, D_MODEL, COL_BLOCK), lambda b, l: (0, 0, 0)),
            pl.BlockSpec((D_MODEL, LANES), const),
            pl.BlockSpec((CONV_W, CONV_DIM), const),
            pl.BlockSpec((1, CONV_W - 1, CONV_DIM), lambda b, l: (b, 0, 0)),
            pl.BlockSpec((2 * A_HEADS, 1), const),
            pl.BlockSpec((2 * A_HEADS, 1), const),
        ],
        out_specs=[
            pl.BlockSpec((tm, A_QK), row),
            pl.BlockSpec((tm, A_QK), row),
            pl.BlockSpec((tm, A_VW), row),
            pl.BlockSpec((tm, A_VW), row),
            pl.BlockSpec((tm, LANES), row),
            pl.BlockSpec((1, CONV_W - 1, CONV_DIM), lambda b, l: (b, 0, 0)),
        ],
        out_shape=[wide, wide, wide, wide,
                   jax.ShapeDtypeStruct((tokens, LANES), F32),
                   jax.ShapeDtypeStruct((batch, CONV_W - 1, CONV_DIM), F32)],
        scratch_shapes=[pltpu.VMEM((tm + SUBLANES, CONV_DIM), F32)],
        compiler_params=_params(("arbitrary", "arbitrary")),
        name="a_in",
    )(x2, w_main, w_gate, conv_w, conv0, alog_col, dt_col)


def _delta_group(bi, q_ref, k_ref, v_ref, z_ref, gb_ref, nw, o_ref, s_ref, c):
    pk = LANES // c
    lc = int(math.log2(c))
    packs = [list(range(p * pk, (p + 1) * pk)) for p in range(A_HEADS // pk)]
    npk = range(len(packs))
    row = lax.broadcasted_iota(jnp.int32, (c, LANES), 0)
    lane = lax.broadcasted_iota(jnp.int32, (c, LANES), 1)
    colr = lane & (c - 1)
    member = lane >> lc
    eye = (row == colr).astype(F32)
    incl = row >= colr
    strict = row > colr
    diag8 = (row >> 3) == (colr >> 3)
    sq0 = lax.broadcasted_iota(jnp.int32, (LANES, LANES), 0)
    sq1 = lax.broadcasted_iota(jnp.int32, (LANES, LANES), 1)
    bd_mask = (sq0 >> lc) == (sq1 >> lc)
    kr0 = lax.broadcasted_iota(jnp.int32, (LANES, pk * LANES), 0)
    kr1 = lax.broadcasted_iota(jnp.int32, (LANES, pk * LANES), 1)
    k_mask = (kr0 >> lc) == (kr1 >> int(math.log2(LANES)))

    def bd(m):
        return jnp.where(bd_mask, jnp.concatenate([m] * pk, axis=0), 0.0)

    def by_member(vals):
        out = vals[0]
        for r in range(1, pk):
            out = jnp.where(member >= r, vals[r], out)
        return out

    def wide(col, hs):
        return jnp.concatenate([jnp.broadcast_to(col[h], (c, LANES)) for h in hs], axis=1)

    def hcol(r):
        return slice(r * LANES, (r + 1) * LANES)

    g = gb_ref[bi]
    gcum = g
    step = 1
    while step < c:
        gcum = gcum + jnp.where(row >= step, pltpu.roll(gcum, step, axis=0), 0.0)
        step *= 2
    gcum_t = jnp.concatenate([gcum] * pk, axis=0).T
    gc = [gcum[:, h:h + 1] for h in range(A_HEADS)]
    beta = [g[:, A_HEADS + h:A_HEADS + h + 1] for h in range(A_HEADS)]
    glast = [gcum[c - 1:c, h:h + 1] for h in range(A_HEADS)]
    eg = [jnp.exp(x) for x in gc]
    pcols = [slice(hs[0] * LANES, (hs[-1] + 1) * LANES) for hs in packs]
    kp16 = [k_ref[bi, :, pc] for pc in pcols]
    kpf = [x.astype(F32) for x in kp16]
    qp16 = [q_ref[bi, :, pc] for pc in pcols]
    vpf = [v_ref[bi, :, pc].astype(F32) for pc in pcols]
    kbeta = [kpf[p] * wide(beta, hs) for p, hs in enumerate(packs)]
    decay = [jnp.exp(jnp.where(incl, by_member([jnp.broadcast_to(gc[h], (c, LANES)) for h in hs])
                               - by_member([gcum_t[h:h + 1, :] for h in hs]), -jnp.inf))
             for hs in packs]
    yield
    k_bd = [jnp.where(k_mask, jnp.concatenate([x] * pk, axis=0), 0.0).astype(BF16) for x in kpf]
    kq = [_mm_nt(jnp.concatenate([kbeta[p].astype(BF16), qp16[p]], axis=0), k_bd[p])
          for p in npk]
    yield
    a = [jnp.where(strict, kq[p][:c] * decay[p], 0.0) for p in npk]
    d = [jnp.where(diag8, x, 0.0) for x in a]
    yield
    d2 = [_mm(x, bd(x)) for x in d]
    yield
    d4 = [_mm(x, bd(x)) for x in d2]
    p1 = [_mm(eye - x, bd(eye + y)) for x, y in zip(d, d2)]
    yield
    xs = [_mm(p, bd(eye + y)) for p, y in zip(p1, d4)]
    yield
    shift = 3
    while (1 << shift) < c:
        mask = ((row >> (shift + 1)) == (colr >> (shift + 1))) & ((row >> shift) > (colr >> shift))
        xc = [_mm(x, bd(jnp.where(mask, m, 0.0))) for x, m in zip(xs, a)]
        yield
        xs = [x - _mm(y, bd(x)) for x, y in zip(xs, xc)]
        yield
        shift += 1
    uw = []
    for p, hs in enumerate(packs):
        vb = vpf[p] * wide(beta, hs)
        ke = kbeta[p] * wide(eg, hs)
        stacked = jnp.concatenate([jnp.concatenate([vb[:, hcol(r)], ke[:, hcol(r)]], axis=1) for r in range(pk)],
                                  axis=0)
        uw.append(_mm(bd(xs[p]), stacked))
    yield
    heads = [(p, r, hs[r]) for p, hs in enumerate(packs) for r in range(pk)]
    s = [s_ref[bi, h] for _, _, h in heads]
    wq = [_mm(jnp.concatenate([uw[p][r * c:(r + 1) * c, A_DV:], qp16[p][:, hcol(r)].astype(F32) * eg[h]], axis=0),
              s[i]) for i, (p, r, h) in enumerate(heads)]
    yield
    v_new = [uw[p][r * c:(r + 1) * c, :A_DV] - wq[i][:c] for i, (p, r, h) in enumerate(heads)]
    intra = [kq[p][c:] * decay[p] for p in npk]
    kd_t = [(kpf[p][:, hcol(r)] * jnp.exp(glast[h] - gc[h])).T for p, r, h in heads]
    ov = [_mm(jnp.concatenate([intra[p][:, r * c:(r + 1) * c], kd_t[i]], axis=0), v_new[i])
          for i, (p, r, h) in enumerate(heads)]
    yield
    for i, (p, r, h) in enumerate(heads):
        s_ref[bi, h] = s[i] * jnp.exp(glast[h]) + ov[i][c:]
        o = wq[i][c:] + ov[i][:c]
        zf = z_ref[bi, :, h * LANES:(h + 1) * LANES].astype(F32)
        gated = o * lax.rsqrt(jnp.mean(o * o, axis=-1, keepdims=True) + RMS_EPS) * nw * _silu(zf)
        o_ref[bi, :, h * LANES:(h + 1) * LANES] = gated.astype(o_ref.dtype)


def _run_skewed(gens, lag):
    done = [False] * len(gens)
    tick = 0
    while not all(done):
        for i, gen in enumerate(gens):
            if not done[i] and tick >= i * lag:
                try:
                    next(gen)
                except StopIteration:
                    done[i] = True
        tick += 1


def _delta_kernel(q_ref, k_ref, v_ref, z_ref, gb_ref, s0_ref, nw_ref, o_ref, s_ref, *, c, nb):
    n = pl.program_id(1)

    @pl.when(n == 0)
    def _():
        s_ref[...] = s0_ref[...]

    nw = nw_ref[...]
    _run_skewed([_delta_group(bi, q_ref, k_ref, v_ref, z_ref, gb_ref, nw, o_ref, s_ref, c) for bi in range(nb)],
                DELTA_LAG)


def _delta(q, k, v, z, gb, s0, norm_w, *, batch, seq):
    c = min(CHUNK, seq)
    n = seq // c
    nb = DELTA_BATCHES
    seq3 = lambda a: a.reshape(batch, seq, a.shape[-1])
    blk = lambda b, i: (b, i, 0)
    state = lambda b, i: (b, 0, 0, 0)
    o, s_new = pl.pallas_call(
        functools.partial(_delta_kernel, c=c, nb=nb),
        grid=(batch // nb, n),
        in_specs=[
            pl.BlockSpec((nb, c, A_QK), blk),
            pl.BlockSpec((nb, c, A_QK), blk),
            pl.BlockSpec((nb, c, A_VW), blk),
            pl.BlockSpec((nb, c, A_VW), blk),
            pl.BlockSpec((nb, c, LANES), blk),
            pl.BlockSpec((nb, A_HEADS, A_DK, A_DV), state),
            pl.BlockSpec((1, LANES), lambda b, i: (0, 0)),
        ],
        out_specs=[
            pl.BlockSpec((nb, c, A_VW), blk),
            pl.BlockSpec((nb, A_HEADS, A_DK, A_DV), state),
        ],
        out_shape=[jax.ShapeDtypeStruct((batch, seq, A_VW), BF16),
                   jax.ShapeDtypeStruct((batch, A_HEADS, A_DK, A_DV), F32)],
        compiler_params=_params(("arbitrary", "arbitrary")),
        name="delta_rule",
    )(seq3(q), seq3(k), seq3(v), seq3(z), seq3(gb), s0, norm_w)
    return o.reshape(batch * seq, A_VW), s_new


def _out_ln_kernel(o_ref, x_ref, w_ref, g_ref, b_ref, y_ref, *, tm, sub):
    for r0 in range(0, tm, sub):
        rows = slice(r0, r0 + sub)
        o = o_ref[rows, :]
        proj = jnp.concatenate([jnp.dot(o, w_ref[j], preferred_element_type=F32)
                                for j in range(D_MODEL // COL_BLOCK)], axis=1)
        r = DN_ALPHA * x_ref[rows, :] + proj
        mu = jnp.mean(r, axis=-1, keepdims=True)
        d = r - mu
        var = jnp.mean(d * d, axis=-1, keepdims=True)
        y_ref[rows, :] = d * lax.rsqrt(var + LN_EPS) * g_ref[...] + b_ref[...]


def _out_ln(o, x2, w_out, ln_g, ln_b):
    tokens = x2.shape[0]
    tm = min(OUT_TILE, tokens)
    sub = min(ROW_TILE, tm)
    row = lambda i: (i, 0)
    const = lambda i: (0, 0)
    return pl.pallas_call(
        functools.partial(_out_ln_kernel, tm=tm, sub=sub),
        grid=(tokens // tm,),
        in_specs=[
            pl.BlockSpec((tm, o.shape[1]), row),
            pl.BlockSpec((tm, D_MODEL), row),
            pl.BlockSpec((D_MODEL // COL_BLOCK, o.shape[1], COL_BLOCK), lambda i: (0, 0, 0)),
            pl.BlockSpec((1, D_MODEL), const),
            pl.BlockSpec((1, D_MODEL), const),
        ],
        out_specs=pl.BlockSpec((tm, D_MODEL), row),
        out_shape=jax.ShapeDtypeStruct((tokens, D_MODEL), F32),
        compiler_params=_params(("arbitrary",)),
        name="out_ln",
    )(o, x2, w_out, ln_g, ln_b)


def _rope_slab(x, cos_t, sin_up, sin_dn):
    half = ROPE_DIMS // 2
    return x * cos_t + pltpu.roll(x, LANES - half, axis=1) * sin_up + pltpu.roll(x, half, axis=1) * sin_dn


def _kv_kernel(h_ref, w_ref, cos_ref, sup_ref, sdn_ref, k_ref, v_ref, klast_ref, vlast_ref, *, tm, keep):
    kv = jnp.dot(h_ref[...].astype(BF16), w_ref[...], preferred_element_type=F32)
    cos_t, sin_up, sin_dn = cos_ref[...], sup_ref[...], sdn_ref[...]
    k = jnp.concatenate([_rope_slab(kv[:, s * LANES:(s + 1) * LANES], cos_t, sin_up, sin_dn)
                         for s in range(B_KVW // LANES)], axis=1)
    v = kv[:, B_KVW:]
    k_ref[...] = k.astype(k_ref.dtype)
    v_ref[...] = v.astype(v_ref.dtype)
    klast_ref[0] = k[tm - keep:, :]
    vlast_ref[0] = v[tm - keep:, :]


def _shared_kv(h2, w_kv, tables, *, batch, seq):
    tm = min(WIDE_TILE, seq)
    nl = seq // tm
    keep = min(WINDOW, seq)
    assert keep <= tm
    tokens = batch * seq
    row = lambda b, l: (b * nl + l, 0)
    tab = lambda b, l: (l, 0)
    lastb = lambda b, l: (b, 0, 0)
    out = jax.ShapeDtypeStruct((tokens, B_KVW), BF16)
    last = jax.ShapeDtypeStruct((batch, keep, B_KVW), F32)
    return pl.pallas_call(
        functools.partial(_kv_kernel, tm=tm, keep=keep),
        grid=(batch, nl),
        in_specs=[
            pl.BlockSpec((tm, D_MODEL), row),
            pl.BlockSpec((D_MODEL, 2 * B_KVW), lambda b, l: (0, 0)),
            pl.BlockSpec((tm, LANES), tab),
            pl.BlockSpec((tm, LANES), tab),
            pl.BlockSpec((tm, LANES), tab),
        ],
        out_specs=[pl.BlockSpec((tm, B_KVW), row), pl.BlockSpec((tm, B_KVW), row),
                   pl.BlockSpec((1, keep, B_KVW), lastb), pl.BlockSpec((1, keep, B_KVW), lastb)],
        out_shape=[out, out, last, last],
        compiler_params=_params(("arbitrary", "arbitrary")),
        name="shared_kv",
    )(h2, w_kv, *tables)


def _b_in_kernel(x_ref, w_ref, cos_ref, sup_ref, sdn_ref, qx_ref, z_ref, *, tm):
    xb = x_ref[...].astype(BF16)
    cos_t, sin_up, sin_dn = cos_ref[...], sup_ref[...], sdn_ref[...]
    lane_half = lax.broadcasted_iota(jnp.int32, (tm, LANES), 1) >> int(math.log2(B_HD))
    cb = COL_BLOCK
    for j in range(B_QW // cb):
        proj = jnp.dot(xb, w_ref[j], preferred_element_type=F32)
        for sl in range(cb // LANES):
            slab = j * (cb // LANES) + sl
            rot = _rope_slab(proj[:, sl * LANES:(sl + 1) * LANES], cos_t, sin_up, sin_dn) * (B_HD ** -0.5)
            for p in range(2):
                hq = 2 * slab + p
                x = jnp.where(lane_half == p, rot, 0.0)
                if p != (hq // B_GROUP) % 2:
                    x = pltpu.roll(x, B_HD, axis=1)
                qx_ref[:, hq * LANES:(hq + 1) * LANES] = x.astype(qx_ref.dtype)
    for j in range(B_QW // cb):
        z_ref[:, j * cb:(j + 1) * cb] = jnp.dot(
            xb, w_ref[B_QW // cb + j], preferred_element_type=F32).astype(z_ref.dtype)


def _b_in(x2, w_in, tables, *, batch, seq):
    tm = min(WIDE_TILE, seq)
    nl = seq // tm
    tokens = batch * seq
    row = lambda b, l: (b * nl + l, 0)
    tab = lambda b, l: (l, 0)
    return pl.pallas_call(
        functools.partial(_b_in_kernel, tm=tm),
        grid=(batch, nl),
        in_specs=[
            pl.BlockSpec((tm, D_MODEL), row),
            pl.BlockSpec((2 * B_QW // COL_BLOCK, D_MODEL, COL_BLOCK), lambda b, l: (0, 0, 0)),
            pl.BlockSpec((tm, LANES), tab),
            pl.BlockSpec((tm, LANES), tab),
            pl.BlockSpec((tm, LANES), tab),
        ],
        out_specs=[pl.BlockSpec((tm, B_QX), row), pl.BlockSpec((tm, B_QW), row)],
        out_shape=[jax.ShapeDtypeStruct((tokens, B_QX), BF16), jax.ShapeDtypeStruct((tokens, B_QW), BF16)],
        compiler_params=_params(("arbitrary", "arbitrary")),
        name="b_in",
    )(x2, w_in, *tables)


def _attn_kernel(sink_ref, qx_ref, z_ref, k_ref, v_ref, o_ref, *, lq, lk, nc, banded):
    step = pl.program_id(1)
    lane_half = lax.broadcasted_iota(jnp.int32, (lq, LANES), 1) >> int(math.log2(B_HD))
    kv_slabs = range(B_KVW // LANES)
    kwin, vwin, valid = [], [], []
    for ci in range(nc):
        if banded:
            c = step * nc + ci
            first = jnp.maximum(c - WINDOW_CHUNKS, 0)
            start = pl.multiple_of(first * CHUNK, CHUNK)
            key_chunk = first + (lax.broadcasted_iota(jnp.int32, (lq, lk), 1) >> int(math.log2(CHUNK)))
            valid.append(key_chunk <= c)
        else:
            start = 0
            valid.append(None)
        kwin.append([k_ref[0, pl.ds(start, lk), s * LANES:(s + 1) * LANES] for s in kv_slabs])
        vwin.append([v_ref[0, pl.ds(start, lk), s * LANES:(s + 1) * LANES] for s in kv_slabs])
    units = [(ci, j) for ci in range(nc) for j in range(B_KV_HEADS)]
    qstack = [jnp.concatenate([qx_ref[ci * lq:(ci + 1) * lq, (B_GROUP * j + g) * LANES:(B_GROUP * j + g + 1) * LANES]
                               for g in range(B_GROUP)], axis=0) for ci, j in units]
    scores = [_mm_nt(qstack[u], kwin[ci][j // 2]) for u, (ci, j) in enumerate(units)]
    heads = [(u, ci, j, g) for u, (ci, j) in enumerate(units) for g in range(B_GROUP)]
    sk = [sink_ref[B_GROUP * j + g] for _, _, j, g in heads]
    sc = [scores[u][g * lq:(g + 1) * lq] for u, _, _, g in heads]
    sc = [s if valid[ci] is None else jnp.where(valid[ci], s, -jnp.inf) for s, (_, ci, _, _) in zip(sc, heads)]
    mx = [jnp.maximum(jnp.max(s, axis=-1, keepdims=True), k) for s, k in zip(sc, sk)]
    ex = [jnp.exp(s - m) for s, m in zip(sc, mx)]
    den = [jnp.sum(e, axis=-1, keepdims=True) + jnp.exp(k - m) for e, k, m in zip(ex, sk, mx)]
    pr = [(e / d).astype(BF16) for e, d in zip(ex, den)]
    pstack = [jnp.concatenate(pr[u * B_GROUP:(u + 1) * B_GROUP], axis=0) for u in range(len(units))]
    pv = [_mm(pstack[u], vwin[ci][j // 2]) for u, (ci, j) in enumerate(units)]
    for ci in range(nc):
        rows = slice(ci * lq, (ci + 1) * lq)
        for slab in range(B_QW // LANES):
            halves = []
            for p in range(2):
                hq = 2 * slab + p
                j, g = divmod(hq, B_GROUP)
                x = pv[ci * B_KV_HEADS + j][g * lq:(g + 1) * lq]
                halves.append(pltpu.roll(x, B_HD, axis=1) if p != j % 2 else x)
            both = jnp.where(lane_half == 0, halves[0], halves[1])
            zs = z_ref[rows, slab * LANES:(slab + 1) * LANES].astype(F32)
            o_ref[rows, slab * LANES:(slab + 1) * LANES] = (both * _silu(zs)).astype(o_ref.dtype)


def _attention(qx, z, k3, v3, sinks, *, batch, seq, banded):
    lq = min(CHUNK, seq)
    nc = min(ATTN_CHUNKS, seq // lq)
    nq = seq // (lq * nc)
    ltot = k3.shape[1]
    lk = (WINDOW_CHUNKS + 1) * CHUNK if banded else ltot
    tokens = batch * seq
    row = lambda b, c: (b * nq + c, 0)
    whole = lambda b, c: (b, 0, 0)
    return pl.pallas_call(
        functools.partial(_attn_kernel, lq=lq, lk=lk, nc=nc, banded=banded),
        grid=(batch, nq),
        in_specs=[
            pl.BlockSpec(memory_space=pltpu.SMEM),
            pl.BlockSpec((nc * lq, B_QX), row),
            pl.BlockSpec((nc * lq, B_QW), row),
            pl.BlockSpec((1, ltot, B_KVW), whole),
            pl.BlockSpec((1, ltot, B_KVW), whole),
        ],
        out_specs=pl.BlockSpec((nc * lq, B_QW), row),
        out_shape=jax.ShapeDtypeStruct((tokens, B_QW), BF16),
        compiler_params=_params(("arbitrary", "arbitrary")),
        name="swa_attention",
    )(sinks, qx, z, k3, v3)


def _rope_tables(pos):
    half = ROPE_DIMS // 2
    inv = ROPE_THETA ** (-jnp.arange(half, dtype=F32) * 2.0 / ROPE_DIMS)
    ang = pos.astype(F32)[:, None] * inv[None, :]
    cos, sin = jnp.cos(ang), jnp.sin(ang)
    ones = jnp.ones((pos.shape[0], B_HD - ROPE_DIMS), F32)
    zeros_h = jnp.zeros((pos.shape[0], half), F32)
    zeros_r = jnp.zeros((pos.shape[0], B_HD - ROPE_DIMS), F32)
    cos_head = jnp.concatenate([cos, cos, ones], axis=1)
    up_head = jnp.concatenate([-sin, zeros_h, zeros_r], axis=1)
    dn_head = jnp.concatenate([zeros_h, sin, zeros_r], axis=1)
    rep = LANES // B_HD
    return tuple(jnp.tile(t, (1, rep)) for t in (cos_head, up_head, dn_head))


def _trunk(x, pos, conv_state, delta_state, past_k, past_v, wts):
    batch, seq, _ = x.shape
    tokens = batch * seq
    h = x.reshape(tokens, D_MODEL)
    tables = _rope_tables(pos)
    new_conv, new_delta = [], []
    for i in range(N_A_LAYERS):
        q, k, v, z, gb, cbuf = _a_in(h, wts["a_w_main"][i], wts["a_w_gate"][i], wts["a_conv_w"][i], conv_state[i],
                                     wts["a_log"][i], wts["a_dt"][i], batch=batch, seq=seq)
        o, s_new = _delta(q, k, v, z, gb, delta_state[i], wts["a_norm_w"][i], batch=batch, seq=seq)
        h = _out_ln(o, h, wts["a_w_out"][i], wts["a_ln_g"][i], wts["a_ln_b"][i])
        new_conv.append(cbuf)
        new_delta.append(s_new)
    k2, v2, k_last, v_last = _shared_kv(h, wts["b_w_kv"], tables, batch=batch, seq=seq)
    k3 = k2.reshape(batch, seq, B_KVW)
    v3 = v2.reshape(batch, seq, B_KVW)
    cached = past_k is not None
    if cached:
        pk = past_k.reshape(batch, -1, B_KVW)
        pv = past_v.reshape(batch, -1, B_KVW)
        k3 = jnp.concatenate([pk.astype(BF16), k3], axis=1)
        v3 = jnp.concatenate([pv.astype(BF16), v3], axis=1)
        k_last = jnp.concatenate([pk, k_last], axis=1)[:, -WINDOW:]
        v_last = jnp.concatenate([pv, v_last], axis=1)[:, -WINDOW:]
    new_k = k_last.reshape(batch, WINDOW, B_KV_HEADS, B_HD)
    new_v = v_last.reshape(batch, WINDOW, B_KV_HEADS, B_HD)
    for j in range(N_B_LAYERS):
        qx, z = _b_in(h, wts["b_w_in"][j], tables, batch=batch, seq=seq)
        o = _attention(qx, z, k3, v3, wts["b_sinks"][j], batch=batch, seq=seq, banded=not cached)
        h = _out_ln(o, h, wts["b_w_out"][j], wts["b_ln_g"][j], wts["b_ln_b"][j])
    return h.reshape(batch, seq, D_MODEL), jnp.stack(new_conv), jnp.stack(new_delta), new_k, new_v


def _col_blocks(w):
    layers, kdim, ndim = w.shape
    return w.astype(BF16).reshape(layers, kdim, ndim // COL_BLOCK, COL_BLOCK).transpose(0, 2, 1, 3)


def kernel(x_prompt, x_sample, state_delta, state_conv, cache_k, cache_v, a_w_in, a_conv_w, a_log, a_dt_bias,
           a_norm_w, a_w_out, a_ln_g, a_ln_b, b_w_kv, b_w_in, b_sinks, b_w_out, b_ln_g, b_ln_b):
    gate_cols = a_w_in[:, :, A_MAIN:]
    zeros_h = jnp.zeros((N_A_LAYERS, A_HEADS), F32)
    wts = {
        "a_w_main": _col_blocks(a_w_in[:, :, :A_MAIN]),
        "a_w_gate": jnp.pad(gate_cols, ((0, 0), (0, 0), (0, LANES - 2 * A_HEADS))).astype(BF16),
        "a_conv_w": a_conv_w,
        "a_log": jnp.concatenate([a_log.astype(F32), zeros_h], axis=1)[:, :, None],
        "a_dt": jnp.concatenate([a_dt_bias.astype(F32), zeros_h], axis=1)[:, :, None],
        "a_norm_w": a_norm_w.reshape(N_A_LAYERS, 1, A_DV),
        "a_w_out": _col_blocks(a_w_out),
        "a_ln_g": a_ln_g.reshape(N_A_LAYERS, 1, D_MODEL),
        "a_ln_b": a_ln_b.reshape(N_A_LAYERS, 1, D_MODEL),
        "b_w_kv": b_w_kv.astype(BF16),
        "b_w_in": _col_blocks(b_w_in),
        "b_sinks": b_sinks,
        "b_w_out": _col_blocks(b_w_out),
        "b_ln_g": b_ln_g.reshape(N_B_LAYERS, 1, D_MODEL),
        "b_ln_b": b_ln_b.reshape(N_B_LAYERS, 1, D_MODEL),
    }
    bp, lp, _ = x_prompt.shape
    bs, ls, _ = x_sample.shape
    pos_prompt = jnp.arange(lp, dtype=jnp.int32)
    pos_sample = PAST_LEN + jnp.arange(ls, dtype=jnp.int32)
    zero_conv = jnp.zeros((N_A_LAYERS, bp, CONV_W - 1, CONV_DIM), F32)
    zero_delta = jnp.zeros((N_A_LAYERS, bp, A_HEADS, A_DK, A_DV), F32)
    y_p, p_conv, p_delta, p_k, p_v = _trunk(x_prompt, pos_prompt, zero_conv, zero_delta, None, None, wts)
    y_s, s_conv, s_delta, s_k, s_v = _trunk(x_sample, pos_sample, state_conv, state_delta, cache_k, cache_v, wts)
    return (y_p, y_s, p_delta, p_conv, p_k, p_v, s_delta, s_conv, s_k, s_v)
```

```python
import functools
import math

import jax
import jax.numpy as jnp
from jax import lax
from jax.experimental import pallas as pl
from jax.experimental.pallas import tpu as pltpu

D_MODEL = 1024
DEPTH = 4
PAST_LEN = 4096
CHUNK = 64
N_A_LAYERS = DEPTH // 2
N_B_LAYERS = DEPTH - N_A_LAYERS
A_HEADS = 8
A_DK = 128
A_DV = 128
A_QK = A_HEADS * A_DK
A_VW = A_HEADS * A_DV
CONV_W = 4
CONV_DIM = 2 * A_QK + A_VW
A_MAIN = CONV_DIM + A_VW
B_Q_HEADS = 16
B_KV_HEADS = 4
B_GROUP = B_Q_HEADS // B_KV_HEADS
B_HD = 64
B_QW = B_Q_HEADS * B_HD
B_KVW = B_KV_HEADS * B_HD
WINDOW = 128
WINDOW_CHUNKS = WINDOW // CHUNK
ROPE_DIMS = B_HD // 4
ROPE_THETA = 500000.0
DN_ALPHA = (2 * DEPTH) ** 0.25
LN_EPS = 1e-5
RMS_EPS = 1e-6

LANES = 128
SUBLANES = 8
VMEM_LIMIT = 48 * 1024 * 1024
ROW_TILE = 256
WIDE_TILE = 512
OUT_TILE = 1024
COL_BLOCK = 512
DELTA_BATCHES = 2
DELTA_LAG = 0
ATTN_CHUNKS = 4
B_QX = B_Q_HEADS * LANES

F32 = jnp.float32
BF16 = jnp.bfloat16


def _mm(a, b):
    return jnp.dot(a.astype(BF16), b.astype(BF16), preferred_element_type=F32)


def _mm_nt(a, b):
    return lax.dot_general(a.astype(BF16), b.astype(BF16), (((1,), (1,)), ((), ())),
                           preferred_element_type=F32)


def _silu(x):
    return x * (1.0 / (1.0 + jnp.exp(-x)))


def _params(semantics):
    return pltpu.CompilerParams(dimension_semantics=semantics, vmem_limit_bytes=VMEM_LIMIT)


def _weight_specs(layer, kdim, nblocks, grid_rank):
    def spec(j):
        if grid_rank == 1:
            return pl.BlockSpec((None, kdim, COL_BLOCK), lambda i: (layer, 0, j))
        return pl.BlockSpec((None, kdim, COL_BLOCK), lambda b, l: (layer, 0, j))
    return [spec(j) for j in range(nblocks)]


def _a_in_kernel(*refs, ts, ns):
    nw = A_MAIN // COL_BLOCK
    x_ref, w_refs = refs[0], refs[1:1 + nw]
    (wg_ref, cw_ref, c0_ref, alog_ref, dt_ref,
     q_ref, k_ref, v_ref, z_ref, gb_ref, cout_ref, pbuf, carry) = refs[1 + nw:]
    tm = ts * ns
    l = pl.program_id(1)
    tail = SUBLANES - (CONV_W - 1)

    @pl.when(l == 0)
    def _():
        for s in range(ns):
            carry[s, tail:SUBLANES, :] = c0_ref[s]

    xb = x_ref[...].astype(BF16)
    cb = COL_BLOCK
    nconv = CONV_DIM // cb
    for j in range(nconv):
        pbuf[:, j * cb:(j + 1) * cb] = jnp.dot(xb, w_refs[j][...], preferred_element_type=F32)
    for j in range(A_VW // cb):
        z_ref[:, j * cb:(j + 1) * cb] = jnp.dot(
            xb, w_refs[nconv + j][...], preferred_element_type=F32).astype(z_ref.dtype)

    gates = jnp.dot(xb, wg_ref[...], preferred_element_type=F32)
    gt = gates.T[:2 * A_HEADS]
    rowi = lax.broadcasted_iota(jnp.int32, gt.shape, 0)
    sp_in = gt + dt_ref[...]
    softplus = jnp.maximum(sp_in, 0.0) + jnp.log1p(jnp.exp(-jnp.abs(sp_in)))
    gval = -jnp.exp(alog_ref[...]) * softplus
    bval = 1.0 / (1.0 + jnp.exp(-gt))
    res = jnp.where(rowi < A_HEADS, gval, bval)
    gb_ref[...] = jnp.concatenate([res, jnp.zeros((LANES - 2 * A_HEADS, tm), F32)], axis=0).T

    outs = (q_ref, k_ref, v_ref)
    for s in range(ns):
        rows = slice(s * ts, (s + 1) * ts)
        for blk in range(CONV_DIM // LANES):
            cols = slice(blk * LANES, (blk + 1) * LANES)
            ext = jnp.concatenate([carry[s, :, cols], pbuf[rows, cols]], axis=0)
            y = None
            for j in range(CONV_W):
                back = CONV_W - 1 - j
                tap = (pltpu.roll(ext, back, axis=0) if back else ext)[SUBLANES:] * cw_ref[j:j + 1, cols]
                y = tap if y is None else y + tap
            y = _silu(y)
            which, head = divmod(blk, A_HEADS)
            if which < 2:
                y = y * lax.rsqrt(jnp.sum(y * y, axis=-1, keepdims=True) + RMS_EPS)
                if which == 0:
                    y = y * (A_DK ** -0.5)
            outs[which][rows, head * LANES:(head + 1) * LANES] = y.astype(outs[which].dtype)

    for s in range(ns):
        last = pbuf[(s + 1) * ts - (CONV_W - 1):(s + 1) * ts, :]
        cout_ref[s] = last
        carry[s, tail:SUBLANES, :] = last


def _a_in(x2, w_in16, w_gate, conv_w, conv0, alog_col, dt_col, *, layer, batch, seq):
    if seq >= ROW_TILE:
        ts, ns = ROW_TILE, 1
    else:
        ts, ns = seq, min(batch, WIDE_TILE // seq)
    tm = ts * ns
    nl = seq // ts
    tokens = batch * seq
    row = lambda b, l: (b * nl + l, 0)
    const = lambda b, l: (0, 0)
    perseq = lambda b, l: (b, 0, 0)
    wide = jax.ShapeDtypeStruct((tokens, A_QK), BF16)
    nw = A_MAIN // COL_BLOCK
    return pl.pallas_call(
        functools.partial(_a_in_kernel, ts=ts, ns=ns),
        grid=(batch // ns, nl),
        in_specs=[pl.BlockSpec((tm, D_MODEL), row)] + _weight_specs(layer, D_MODEL, nw, 2) + [
            pl.BlockSpec((D_MODEL, LANES), const),
            pl.BlockSpec((CONV_W, CONV_DIM), const),
            pl.BlockSpec((ns, CONV_W - 1, CONV_DIM), perseq),
            pl.BlockSpec((2 * A_HEADS, 1), const),
            pl.BlockSpec((2 * A_HEADS, 1), const),
        ],
        out_specs=[
            pl.BlockSpec((tm, A_QK), row),
            pl.BlockSpec((tm, A_QK), row),
            pl.BlockSpec((tm, A_VW), row),
            pl.BlockSpec((tm, A_VW), row),
            pl.BlockSpec((tm, LANES), row),
            pl.BlockSpec((ns, CONV_W - 1, CONV_DIM), perseq),
        ],
        out_shape=[wide, wide, wide, wide,
                   jax.ShapeDtypeStruct((tokens, LANES), F32),
                   jax.ShapeDtypeStruct((batch, CONV_W - 1, CONV_DIM), F32)],
        scratch_shapes=[pltpu.VMEM((tm, CONV_DIM), F32), pltpu.VMEM((ns, SUBLANES, CONV_DIM), F32)],
        compiler_params=_params(("arbitrary", "arbitrary")),
        name="a_in",
    )(x2, *([w_in16] * nw), w_gate, conv_w, conv0, alog_col, dt_col)


def _delta_group(bi, q_ref, k_ref, v_ref, z_ref, gb_ref, nw, o_ref, s_ref, c):
    pk = LANES // c
    lc = int(math.log2(c))
    packs = [list(range(p * pk, (p + 1) * pk)) for p in range(A_HEADS // pk)]
    npk = range(len(packs))
    row = lax.broadcasted_iota(jnp.int32, (c, LANES), 0)
    lane = lax.broadcasted_iota(jnp.int32, (c, LANES), 1)
    colr = lane & (c - 1)
    member = lane >> lc
    eye = (row == colr).astype(F32)
    incl = row >= colr
    strict = row > colr
    diag8 = (row >> 3) == (colr >> 3)
    sq0 = lax.broadcasted_iota(jnp.int32, (LANES, LANES), 0)
    sq1 = lax.broadcasted_iota(jnp.int32, (LANES, LANES), 1)
    bd_mask = (sq0 >> lc) == (sq1 >> lc)
    kr0 = lax.broadcasted_iota(jnp.int32, (LANES, pk * LANES), 0)
    kr1 = lax.broadcasted_iota(jnp.int32, (LANES, pk * LANES), 1)
    k_mask = (kr0 >> lc) == (kr1 >> int(math.log2(LANES)))

    def bd(m):
        return jnp.where(bd_mask, jnp.concatenate([m] * pk, axis=0), 0.0)

    def by_member(vals):
        out = vals[0]
        for r in range(1, pk):
            out = jnp.where(member >= r, vals[r], out)
        return out

    def wide(col, hs):
        return jnp.concatenate([jnp.broadcast_to(col[h], (c, LANES)) for h in hs], axis=1)

    def hcol(r):
        return slice(r * LANES, (r + 1) * LANES)

    g = gb_ref[bi]
    gcum = g
    step = 1
    while step < c:
        gcum = gcum + jnp.where(row >= step, pltpu.roll(gcum, step, axis=0), 0.0)
        step *= 2
    gcum_t = jnp.concatenate([gcum] * pk, axis=0).T
    gc = [gcum[:, h:h + 1] for h in range(A_HEADS)]
    beta = [g[:, A_HEADS + h:A_HEADS + h + 1] for h in range(A_HEADS)]
    glast = [gcum[c - 1:c, h:h + 1] for h in range(A_HEADS)]
    eg = [jnp.exp(x) for x in gc]
    pcols = [slice(hs[0] * LANES, (hs[-1] + 1) * LANES) for hs in packs]
    kp16 = [k_ref[bi, :, pc] for pc in pcols]
    kpf = [x.astype(F32) for x in kp16]
    qp16 = [q_ref[bi, :, pc] for pc in pcols]
    vpf = [v_ref[bi, :, pc].astype(F32) for pc in pcols]
    kbeta = [kpf[p] * wide(beta, hs) for p, hs in enumerate(packs)]
    decay = [jnp.exp(jnp.where(incl, by_member([jnp.broadcast_to(gc[h], (c, LANES)) for h in hs])
                               - by_member([gcum_t[h:h + 1, :] for h in hs]), -jnp.inf))
             for hs in packs]
    yield
    k_bd = [jnp.where(k_mask, jnp.concatenate([x] * pk, axis=0), 0.0).astype(BF16) for x in kpf]
    kq = [_mm_nt(jnp.concatenate([kbeta[p].astype(BF16), qp16[p]], axis=0), k_bd[p])
          for p in npk]
    yield
    a = [jnp.where(strict, kq[p][:c] * decay[p], 0.0) for p in npk]
    d = [jnp.where(diag8, x, 0.0) for x in a]
    yield
    d2 = [_mm(x, bd(x)) for x in d]
    yield
    d4 = [_mm(x, bd(x)) for x in d2]
    p1 = [_mm(eye - x, bd(eye + y)) for x, y in zip(d, d2)]
    yield
    xs = [_mm(p, bd(eye + y)) for p, y in zip(p1, d4)]
    yield
    shift = 3
    while (1 << shift) < c:
        mask = ((row >> (shift + 1)) == (colr >> (shift + 1))) & ((row >> shift) > (colr >> shift))
        xc = [_mm(x, bd(jnp.where(mask, m, 0.0))) for x, m in zip(xs, a)]
        yield
        xs = [x - _mm(y, bd(x)) for x, y in zip(xs, xc)]
        yield
        shift += 1
    uw = []
    for p, hs in enumerate(packs):
        vb = vpf[p] * wide(beta, hs)
        ke = kbeta[p] * wide(eg, hs)
        stacked = jnp.concatenate([jnp.concatenate([vb[:, hcol(r)], ke[:, hcol(r)]], axis=1) for r in range(pk)],
                                  axis=0)
        uw.append(_mm(bd(xs[p]), stacked))
    yield
    heads = [(p, r, hs[r]) for p, hs in enumerate(packs) for r in range(pk)]
    s = [s_ref[bi, h] for _, _, h in heads]
    wq = [_mm(jnp.concatenate([uw[p][r * c:(r + 1) * c, A_DV:], qp16[p][:, hcol(r)].astype(F32) * eg[h]], axis=0),
              s[i]) for i, (p, r, h) in enumerate(heads)]
    yield
    v_new = [uw[p][r * c:(r + 1) * c, :A_DV] - wq[i][:c] for i, (p, r, h) in enumerate(heads)]
    intra = [kq[p][c:] * decay[p] for p in npk]
    kd_t = [(kpf[p][:, hcol(r)] * jnp.exp(glast[h] - gc[h])).T for p, r, h in heads]
    ov = [_mm(jnp.concatenate([intra[p][:, r * c:(r + 1) * c], kd_t[i]], axis=0), v_new[i])
          for i, (p, r, h) in enumerate(heads)]
    yield
    for i, (p, r, h) in enumerate(heads):
        s_ref[bi, h] = s[i] * jnp.exp(glast[h]) + ov[i][c:]
        o = wq[i][c:] + ov[i][:c]
        zf = z_ref[bi, :, h * LANES:(h + 1) * LANES].astype(F32)
        gated = o * lax.rsqrt(jnp.mean(o * o, axis=-1, keepdims=True) + RMS_EPS) * nw * _silu(zf)
        o_ref[bi, :, h * LANES:(h + 1) * LANES] = gated.astype(o_ref.dtype)


def _run_skewed(gens, lag):
    done = [False] * len(gens)
    tick = 0
    while not all(done):
        for i, gen in enumerate(gens):
            if not done[i] and tick >= i * lag:
                try:
                    next(gen)
                except StopIteration:
                    done[i] = True
        tick += 1


def _delta_kernel(q_ref, k_ref, v_ref, z_ref, gb_ref, s0_ref, nw_ref, o_ref, s_ref, *, c, nb):
    n = pl.program_id(1)

    @pl.when(n == 0)
    def _():
        s_ref[...] = s0_ref[...]

    nw = nw_ref[...]
    _run_skewed([_delta_group(bi, q_ref, k_ref, v_ref, z_ref, gb_ref, nw, o_ref, s_ref, c) for bi in range(nb)],
                DELTA_LAG)


def _delta(q, k, v, z, gb, s0_all, norm_w, *, layer, batch, seq):
    c = min(CHUNK, seq)
    n = seq // c
    nb = DELTA_BATCHES
    seq3 = lambda a: a.reshape(batch, seq, a.shape[-1])
    blk = lambda b, i: (b, i, 0)
    state = lambda b, i: (b, 0, 0, 0)
    o, s_new = pl.pallas_call(
        functools.partial(_delta_kernel, c=c, nb=nb),
        grid=(batch // nb, n),
        in_specs=[
            pl.BlockSpec((nb, c, A_QK), blk),
            pl.BlockSpec((nb, c, A_QK), blk),
            pl.BlockSpec((nb, c, A_VW), blk),
            pl.BlockSpec((nb, c, A_VW), blk),
            pl.BlockSpec((nb, c, LANES), blk),
            pl.BlockSpec((None, nb, A_HEADS, A_DK, A_DV), lambda b, i: (layer, b, 0, 0, 0)),
            pl.BlockSpec((1, LANES), lambda b, i: (0, 0)),
        ],
        out_specs=[
            pl.BlockSpec((nb, c, A_VW), blk),
            pl.BlockSpec((nb, A_HEADS, A_DK, A_DV), state),
        ],
        out_shape=[jax.ShapeDtypeStruct((batch, seq, A_VW), BF16),
                   jax.ShapeDtypeStruct((batch, A_HEADS, A_DK, A_DV), F32)],
        compiler_params=_params(("arbitrary", "arbitrary")),
        name="delta_rule",
    )(seq3(q), seq3(k), seq3(v), seq3(z), seq3(gb), s0_all, norm_w)
    return o.reshape(batch * seq, A_VW), s_new


def _out_ln_kernel(*refs, tm, sub):
    nw = D_MODEL // COL_BLOCK
    o_ref, x_ref, w_refs = refs[0], refs[1], refs[2:2 + nw]
    g_ref, b_ref, y_ref = refs[2 + nw:]
    for r0 in range(0, tm, sub):
        rows = slice(r0, r0 + sub)
        o = o_ref[rows, :]
        proj = jnp.concatenate([jnp.dot(o, w[...], preferred_element_type=F32) for w in w_refs], axis=1)
        r = DN_ALPHA * x_ref[rows, :] + proj
        mu = jnp.mean(r, axis=-1, keepdims=True)
        d = r - mu
        var = jnp.mean(d * d, axis=-1, keepdims=True)
        y_ref[rows, :] = d * lax.rsqrt(var + LN_EPS) * g_ref[...] + b_ref[...]


def _out_ln(o, x2, w_out16, ln_g, ln_b, *, layer):
    tokens = x2.shape[0]
    tm = min(OUT_TILE, tokens)
    sub = min(ROW_TILE, tm)
    row = lambda i: (i, 0)
    const = lambda i: (0, 0)
    nw = D_MODEL // COL_BLOCK
    return pl.pallas_call(
        functools.partial(_out_ln_kernel, tm=tm, sub=sub),
        grid=(tokens // tm,),
        in_specs=[pl.BlockSpec((tm, o.shape[1]), row), pl.BlockSpec((tm, D_MODEL), row)]
        + _weight_specs(layer, o.shape[1], nw, 1)
        + [pl.BlockSpec((1, D_MODEL), const), pl.BlockSpec((1, D_MODEL), const)],
        out_specs=pl.BlockSpec((tm, D_MODEL), row),
        out_shape=jax.ShapeDtypeStruct((tokens, D_MODEL), F32),
        compiler_params=_params(("arbitrary",)),
        name="out_ln",
    )(o, x2, *([w_out16] * nw), ln_g, ln_b)


def _rope_slab(x, cos_t, sin_up, sin_dn):
    half = ROPE_DIMS // 2
    return x * cos_t + pltpu.roll(x, LANES - half, axis=1) * sin_up + pltpu.roll(x, half, axis=1) * sin_dn


def _row_tiling(tokens, seq):
    tm = min(WIDE_TILE, tokens)
    if seq >= tm:
        per_seq = seq // tm
        return tm, 1, (lambda i: (i % per_seq, 0))
    return tm, tm // seq, (lambda i: (0, 0))


def _tile_tables(tables, reps):
    return tables if reps == 1 else tuple(jnp.tile(t, (reps, 1)) for t in tables)


def _kv_kernel(h_ref, w_ref, cos_ref, sup_ref, sdn_ref, k_ref, v_ref, klast_ref, vlast_ref, *, rows_per_seq, ns, keep):
    kv = jnp.dot(h_ref[...].astype(BF16), w_ref[...], preferred_element_type=F32)
    cos_t, sin_up, sin_dn = cos_ref[...], sup_ref[...], sdn_ref[...]
    k = jnp.concatenate([_rope_slab(kv[:, s * LANES:(s + 1) * LANES], cos_t, sin_up, sin_dn)
                         for s in range(B_KVW // LANES)], axis=1)
    v = kv[:, B_KVW:]
    k_ref[...] = k.astype(k_ref.dtype)
    v_ref[...] = v.astype(v_ref.dtype)
    for s in range(ns):
        end = (s + 1) * rows_per_seq
        klast_ref[s] = k[end - keep:end, :]
        vlast_ref[s] = v[end - keep:end, :]


def _shared_kv(h2, w_kv16, tables, *, batch, seq):
    tokens = batch * seq
    tm, ns, tab = _row_tiling(tokens, seq)
    keep = min(WINDOW, seq)
    rows_per_seq = tm // ns
    assert keep <= rows_per_seq
    per_seq = max(1, seq // tm)
    row = lambda i: (i, 0)
    lastb = lambda i: (i // per_seq, 0, 0)
    out = jax.ShapeDtypeStruct((tokens, B_KVW), BF16)
    last = jax.ShapeDtypeStruct((batch, keep, B_KVW), F32)
    return pl.pallas_call(
        functools.partial(_kv_kernel, rows_per_seq=rows_per_seq, ns=ns, keep=keep),
        grid=(tokens // tm,),
        in_specs=[
            pl.BlockSpec((tm, D_MODEL), row),
            pl.BlockSpec((D_MODEL, 2 * B_KVW), lambda i: (0, 0)),
            pl.BlockSpec((tm, LANES), tab),
            pl.BlockSpec((tm, LANES), tab),
            pl.BlockSpec((tm, LANES), tab),
        ],
        out_specs=[pl.BlockSpec((tm, B_KVW), row), pl.BlockSpec((tm, B_KVW), row),
                   pl.BlockSpec((ns, keep, B_KVW), lastb), pl.BlockSpec((ns, keep, B_KVW), lastb)],
        out_shape=[out, out, last, last],
        compiler_params=_params(("arbitrary",)),
        name="shared_kv",
    )(h2, w_kv16, *_tile_tables(tables, ns))


def _b_in_kernel(*refs, tm):
    nw = 2 * B_QW // COL_BLOCK
    x_ref, w_refs = refs[0], refs[1:1 + nw]
    cos_ref, sup_ref, sdn_ref, qx_ref, z_ref = refs[1 + nw:]
    xb = x_ref[...].astype(BF16)
    cos_t, sin_up, sin_dn = cos_ref[...], sup_ref[...], sdn_ref[...]
    lane_half = lax.broadcasted_iota(jnp.int32, (tm, LANES), 1) >> int(math.log2(B_HD))
    cb = COL_BLOCK
    nq = B_QW // cb
    for j in range(nq):
        proj = jnp.dot(xb, w_refs[j][...], preferred_element_type=F32)
        for sl in range(cb // LANES):
            slab = j * (cb // LANES) + sl
            rot = _rope_slab(proj[:, sl * LANES:(sl + 1) * LANES], cos_t, sin_up, sin_dn) * (B_HD ** -0.5)
            for p in range(2):
                hq = 2 * slab + p
                x = jnp.where(lane_half == p, rot, 0.0)
                if p != (hq // B_GROUP) % 2:
                    x = pltpu.roll(x, B_HD, axis=1)
                qx_ref[:, hq * LANES:(hq + 1) * LANES] = x.astype(qx_ref.dtype)
    for j in range(nq):
        z_ref[:, j * cb:(j + 1) * cb] = jnp.dot(
            xb, w_refs[nq + j][...], preferred_element_type=F32).astype(z_ref.dtype)


def _b_in(x2, w_in16, tables, *, layer, batch, seq):
    tokens = batch * seq
    tm, ns, tab = _row_tiling(tokens, seq)
    row = lambda i: (i, 0)
    nw = 2 * B_QW // COL_BLOCK
    return pl.pallas_call(
        functools.partial(_b_in_kernel, tm=tm),
        grid=(tokens // tm,),
        in_specs=[pl.BlockSpec((tm, D_MODEL), row)] + _weight_specs(layer, D_MODEL, nw, 1) + [
            pl.BlockSpec((tm, LANES), tab),
            pl.BlockSpec((tm, LANES), tab),
            pl.BlockSpec((tm, LANES), tab),
        ],
        out_specs=[pl.BlockSpec((tm, B_QX), row), pl.BlockSpec((tm, B_QW), row)],
        out_shape=[jax.ShapeDtypeStruct((tokens, B_QX), BF16), jax.ShapeDtypeStruct((tokens, B_QW), BF16)],
        compiler_params=_params(("arbitrary",)),
        name="b_in",
    )(x2, *([w_in16] * nw), *_tile_tables(tables, ns))


def _attn_kernel(sink_ref, qx_ref, z_ref, k_ref, v_ref, o_ref, *, lq, lk, nc, banded):
    step = pl.program_id(1)
    lane_half = lax.broadcasted_iota(jnp.int32, (lq, LANES), 1) >> int(math.log2(B_HD))
    kv_slabs = range(B_KVW // LANES)
    kwin, vwin, valid = [], [], []
    for ci in range(nc):
        if banded:
            c = step * nc + ci
            first = jnp.maximum(c - WINDOW_CHUNKS, 0)
            start = pl.multiple_of(first * CHUNK, CHUNK)
            key_chunk = first + (lax.broadcasted_iota(jnp.int32, (lq, lk), 1) >> int(math.log2(CHUNK)))
            valid.append(key_chunk <= c)
        else:
            start = 0
            valid.append(None)
        kwin.append([k_ref[0, pl.ds(start, lk), s * LANES:(s + 1) * LANES] for s in kv_slabs])
        vwin.append([v_ref[0, pl.ds(start, lk), s * LANES:(s + 1) * LANES] for s in kv_slabs])
    units = [(ci, j) for ci in range(nc) for j in range(B_KV_HEADS)]
    qstack = [jnp.concatenate([qx_ref[ci * lq:(ci + 1) * lq, (B_GROUP * j + g) * LANES:(B_GROUP * j + g + 1) * LANES]
                               for g in range(B_GROUP)], axis=0) for ci, j in units]
    scores = [_mm_nt(qstack[u], kwin[ci][j // 2]) for u, (ci, j) in enumerate(units)]
    heads = [(u, ci, j, g) for u, (ci, j) in enumerate(units) for g in range(B_GROUP)]
    sk = [sink_ref[B_GROUP * j + g] for _, _, j, g in heads]
    sc = [scores[u][g * lq:(g + 1) * lq] for u, _, _, g in heads]
    sc = [s if valid[ci] is None else jnp.where(valid[ci], s, -jnp.inf) for s, (_, ci, _, _) in zip(sc, heads)]
    mx = [jnp.maximum(jnp.max(s, axis=-1, keepdims=True), k) for s, k in zip(sc, sk)]
    ex = [jnp.exp(s - m) for s, m in zip(sc, mx)]
    den = [jnp.sum(e, axis=-1, keepdims=True) + jnp.exp(k - m) for e, k, m in zip(ex, sk, mx)]
    pr = [(e / d).astype(BF16) for e, d in zip(ex, den)]
    pstack = [jnp.concatenate(pr[u * B_GROUP:(u + 1) * B_GROUP], axis=0) for u in range(len(units))]
    pv = [_mm(pstack[u], vwin[ci][j // 2]) for u, (ci, j) in enumerate(units)]
    for ci in range(nc):
        rows = slice(ci * lq, (ci + 1) * lq)
        for slab in range(B_QW // LANES):
            halves = []
            for p in range(2):
                hq = 2 * slab + p
                j, g = divmod(hq, B_GROUP)
                x = pv[ci * B_KV_HEADS + j][g * lq:(g + 1) * lq]
                halves.append(pltpu.roll(x, B_HD, axis=1) if p != j % 2 else x)
            both = jnp.where(lane_half == 0, halves[0], halves[1])
            zs = z_ref[rows, slab * LANES:(slab + 1) * LANES].astype(F32)
            o_ref[rows, slab * LANES:(slab + 1) * LANES] = (both * _silu(zs)).astype(o_ref.dtype)


def _attention(qx, z, k3, v3, sinks, *, batch, seq, banded):
    lq = min(CHUNK, seq)
    nc = min(ATTN_CHUNKS, seq // lq)
    nq = seq // (lq * nc)
    ltot = k3.shape[1]
    lk = (WINDOW_CHUNKS + 1) * CHUNK if banded else ltot
    tokens = batch * seq
    row = lambda b, c: (b * nq + c, 0)
    whole = lambda b, c: (b, 0, 0)
    return pl.pallas_call(
        functools.partial(_attn_kernel, lq=lq, lk=lk, nc=nc, banded=banded),
        grid=(batch, nq),
        in_specs=[
            pl.BlockSpec(memory_space=pltpu.SMEM),
            pl.BlockSpec((nc * lq, B_QX), row),
            pl.BlockSpec((nc * lq, B_QW), row),
            pl.BlockSpec((1, ltot, B_KVW), whole),
            pl.BlockSpec((1, ltot, B_KVW), whole),
        ],
        out_specs=pl.BlockSpec((nc * lq, B_QW), row),
        out_shape=jax.ShapeDtypeStruct((tokens, B_QW), BF16),
        compiler_params=_params(("arbitrary", "arbitrary")),
        name="swa_attention",
    )(sinks, qx, z, k3, v3)


def _rope_tables(pos):
    half = ROPE_DIMS // 2
    inv = ROPE_THETA ** (-jnp.arange(half, dtype=F32) * 2.0 / ROPE_DIMS)
    ang = pos.astype(F32)[:, None] * inv[None, :]
    cos, sin = jnp.cos(ang), jnp.sin(ang)
    ones = jnp.ones((pos.shape[0], B_HD - ROPE_DIMS), F32)
    zeros_h = jnp.zeros((pos.shape[0], half), F32)
    zeros_r = jnp.zeros((pos.shape[0], B_HD - ROPE_DIMS), F32)
    cos_head = jnp.concatenate([cos, cos, ones], axis=1)
    up_head = jnp.concatenate([-sin, zeros_h, zeros_r], axis=1)
    dn_head = jnp.concatenate([zeros_h, sin, zeros_r], axis=1)
    rep = LANES // B_HD
    return tuple(jnp.tile(t, (1, rep)) for t in (cos_head, up_head, dn_head))


def _trunk(x, pos, conv_state, delta_state, past_k, past_v, wts):
    batch, seq, _ = x.shape
    tokens = batch * seq
    h = x.reshape(tokens, D_MODEL)
    tables = _rope_tables(pos)
    new_conv, new_delta = [], []
    for i in range(N_A_LAYERS):
        q, k, v, z, gb, cbuf = _a_in(h, wts["a_w_in"], wts["a_w_gate"][i], wts["a_conv_w"][i], conv_state[i],
                                     wts["a_log"][i], wts["a_dt"][i], layer=i, batch=batch, seq=seq)
        o, s_new = _delta(q, k, v, z, gb, delta_state, wts["a_norm_w"][i], layer=i, batch=batch, seq=seq)
        h = _out_ln(o, h, wts["a_w_out"], wts["a_ln_g"][i], wts["a_ln_b"][i], layer=i)
        new_conv.append(cbuf)
        new_delta.append(s_new)
    k2, v2, k_last, v_last = _shared_kv(h, wts["b_w_kv"], tables, batch=batch, seq=seq)
    k3 = k2.reshape(batch, seq, B_KVW)
    v3 = v2.reshape(batch, seq, B_KVW)
    cached = past_k is not None
    if cached:
        pk = past_k.reshape(batch, -1, B_KVW)
        pv = past_v.reshape(batch, -1, B_KVW)
        k3 = jnp.concatenate([pk.astype(BF16), k3], axis=1)
        v3 = jnp.concatenate([pv.astype(BF16), v3], axis=1)
        k_last = jnp.concatenate([pk, k_last], axis=1)[:, -WINDOW:]
        v_last = jnp.concatenate([pv, v_last], axis=1)[:, -WINDOW:]
    new_k = k_last.reshape(batch, WINDOW, B_KV_HEADS, B_HD)
    new_v = v_last.reshape(batch, WINDOW, B_KV_HEADS, B_HD)
    for j in range(N_B_LAYERS):
        qx, z = _b_in(h, wts["b_w_in"], tables, layer=j, batch=batch, seq=seq)
        o = _attention(qx, z, k3, v3, wts["b_sinks"][j], batch=batch, seq=seq, banded=not cached)
        h = _out_ln(o, h, wts["b_w_out"], wts["b_ln_g"][j], wts["b_ln_b"][j], layer=j)
    return h.reshape(batch, seq, D_MODEL), jnp.stack(new_conv), jnp.stack(new_delta), new_k, new_v


def kernel(x_prompt, x_sample, state_delta, state_conv, cache_k, cache_v, a_w_in, a_conv_w, a_log, a_dt_bias,
           a_norm_w, a_w_out, a_ln_g, a_ln_b, b_w_kv, b_w_in, b_sinks, b_w_out, b_ln_g, b_ln_b):
    a_w_in16 = a_w_in.astype(BF16)
    zeros_h = jnp.zeros((N_A_LAYERS, A_HEADS), F32)
    wts = {
        "a_w_in": a_w_in16,
        "a_w_gate": jnp.pad(a_w_in16[:, :, A_MAIN:], ((0, 0), (0, 0), (0, LANES - 2 * A_HEADS))),
        "a_conv_w": a_conv_w,
        "a_log": jnp.concatenate([a_log.astype(F32), zeros_h], axis=1)[:, :, None],
        "a_dt": jnp.concatenate([a_dt_bias.astype(F32), zeros_h], axis=1)[:, :, None],
        "a_norm_w": a_norm_w.reshape(N_A_LAYERS, 1, A_DV),
        "a_w_out": a_w_out.astype(BF16),
        "a_ln_g": a_ln_g.reshape(N_A_LAYERS, 1, D_MODEL),
        "a_ln_b": a_ln_b.reshape(N_A_LAYERS, 1, D_MODEL),
        "b_w_kv": b_w_kv.astype(BF16),
        "b_w_in": b_w_in.astype(BF16),
        "b_sinks": b_sinks,
        "b_w_out": b_w_out.astype(BF16),
        "b_ln_g": b_ln_g.reshape(N_B_LAYERS, 1, D_MODEL),
        "b_ln_b": b_ln_b.reshape(N_B_LAYERS, 1, D_MODEL),
    }
    bp, lp, _ = x_prompt.shape
    bs, ls, _ = x_sample.shape
    pos_prompt = jnp.arange(lp, dtype=jnp.int32)
    pos_sample = PAST_LEN + jnp.arange(ls, dtype=jnp.int32)
    zero_conv = jnp.zeros((N_A_LAYERS, bp, CONV_W - 1, CONV_DIM), F32)
    zero_delta = jnp.zeros((N_A_LAYERS, bp, A_HEADS, A_DK, A_DV), F32)
    y_p, p_conv, p_delta, p_k, p_v = _trunk(x_prompt, pos_prompt, zero_conv, zero_delta, None, None, wts)
    y_s, s_conv, s_delta, s_k, s_v = _trunk(x_sample, pos_sample, state_conv, state_delta, cache_k, cache_v, wts)
    return (y_p, y_s, p_delta, p_conv, p_k, p_v, s_delta, s_conv, s_k, s_v)
```

```python
import functools
import math

import jax
import jax.numpy as jnp
from jax import lax
from jax.experimental import pallas as pl
from jax.experimental.pallas import tpu as pltpu

D_MODEL = 1024
DEPTH = 4
PAST_LEN = 4096
CHUNK = 64
N_A_LAYERS = DEPTH // 2
N_B_LAYERS = DEPTH - N_A_LAYERS
A_HEADS = 8
A_DK = 128
A_DV = 128
A_QK = A_HEADS * A_DK
A_VW = A_HEADS * A_DV
CONV_W = 4
CONV_DIM = 2 * A_QK + A_VW
A_MAIN = CONV_DIM + A_VW
B_Q_HEADS = 16
B_KV_HEADS = 4
B_GROUP = B_Q_HEADS // B_KV_HEADS
B_HD = 64
B_QW = B_Q_HEADS * B_HD
B_KVW = B_KV_HEADS * B_HD
WINDOW = 128
WINDOW_CHUNKS = WINDOW // CHUNK
ROPE_DIMS = B_HD // 4
ROPE_THETA = 500000.0
DN_ALPHA = (2 * DEPTH) ** 0.25
LN_EPS = 1e-5
RMS_EPS = 1e-6

LANES = 128
SUBLANES = 8
VMEM_LIMIT = 48 * 1024 * 1024
ROW_TILE = 256
WIDE_TILE = 512
OUT_TILE = 1024
COL_BLOCK = 512
DELTA_BATCHES = 2
ATTN_CHUNKS = 4
B_QX = B_Q_HEADS * LANES
LOG2E = math.log2(math.e)

F32 = jnp.float32
BF16 = jnp.bfloat16


def _mm(a, b):
    return jnp.dot(a.astype(BF16), b.astype(BF16), preferred_element_type=F32)


def _mm_nt(a, b):
    return lax.dot_general(a.astype(BF16), b.astype(BF16), (((1,), (1,)), ((), ())),
                           preferred_element_type=F32)


def _silu(x):
    return x * (1.0 / (1.0 + jnp.exp(-x)))


def _params(semantics):
    return pltpu.CompilerParams(dimension_semantics=semantics, vmem_limit_bytes=VMEM_LIMIT)


def _weight_specs(layer, kdim, nblocks, grid_rank):
    def spec(j):
        if grid_rank == 1:
            return pl.BlockSpec((None, kdim, COL_BLOCK), lambda i: (layer, 0, j))
        return pl.BlockSpec((None, kdim, COL_BLOCK), lambda b, l: (layer, 0, j))
    return [spec(j) for j in range(nblocks)]


def _a_in_kernel(*refs, ts, ns):
    nw = A_MAIN // COL_BLOCK
    x_ref, w_refs = refs[0], refs[1:1 + nw]
    (wg_ref, cw_ref, c0_ref, alog_ref, dt_ref,
     q_ref, k_ref, v_ref, z_ref, gb_ref, cout_ref, pbuf, carry) = refs[1 + nw:]
    tm = ts * ns
    l = pl.program_id(1)
    tail = SUBLANES - (CONV_W - 1)

    @pl.when(l == 0)
    def _():
        for s in range(ns):
            carry[s, tail:SUBLANES, :] = c0_ref[s]

    xb = x_ref[...].astype(BF16)
    cb = COL_BLOCK
    nconv = CONV_DIM // cb
    for j in range(nconv):
        pbuf[:, j * cb:(j + 1) * cb] = jnp.dot(xb, w_refs[j][...], preferred_element_type=F32)
    for j in range(A_VW // cb):
        z_ref[:, j * cb:(j + 1) * cb] = jnp.dot(
            xb, w_refs[nconv + j][...], preferred_element_type=F32).astype(z_ref.dtype)

    gates = jnp.dot(xb, wg_ref[...], preferred_element_type=F32)
    gt = gates.T[:2 * A_HEADS]
    rowi = lax.broadcasted_iota(jnp.int32, gt.shape, 0)
    sp_in = gt + dt_ref[...]
    softplus = jnp.maximum(sp_in, 0.0) + jnp.log1p(jnp.exp(-jnp.abs(sp_in)))
    gval = -jnp.exp(alog_ref[...]) * softplus
    bval = 1.0 / (1.0 + jnp.exp(-gt))
    res = jnp.where(rowi < A_HEADS, gval, bval)
    gb_ref[...] = jnp.concatenate([res, jnp.zeros((LANES - 2 * A_HEADS, tm), F32)], axis=0).T

    outs = (q_ref, k_ref, v_ref)
    for s in range(ns):
        rows = slice(s * ts, (s + 1) * ts)
        for blk in range(CONV_DIM // LANES):
            cols = slice(blk * LANES, (blk + 1) * LANES)
            ext = jnp.concatenate([carry[s, :, cols], pbuf[rows, cols]], axis=0)
            y = None
            for j in range(CONV_W):
                back = CONV_W - 1 - j
                tap = (pltpu.roll(ext, back, axis=0) if back else ext)[SUBLANES:] * cw_ref[j:j + 1, cols]
                y = tap if y is None else y + tap
            y = _silu(y)
            which, head = divmod(blk, A_HEADS)
            if which < 2:
                y = y * lax.rsqrt(jnp.sum(y * y, axis=-1, keepdims=True) + RMS_EPS)
                if which == 0:
                    y = y * (A_DK ** -0.5)
            outs[which][rows, head * LANES:(head + 1) * LANES] = y.astype(outs[which].dtype)

    for s in range(ns):
        last = pbuf[(s + 1) * ts - (CONV_W - 1):(s + 1) * ts, :]
        cout_ref[s] = last
        carry[s, tail:SUBLANES, :] = last


def _a_in(x2, w_in16, w_gate, conv_w, conv0, alog_col, dt_col, *, layer, batch, seq):
    if seq >= ROW_TILE:
        ts, ns = ROW_TILE, 1
    else:
        ts, ns = seq, min(batch, WIDE_TILE // seq)
    tm = ts * ns
    nl = seq // ts
    tokens = batch * seq
    row = lambda b, l: (b * nl + l, 0)
    const = lambda b, l: (0, 0)
    perseq = lambda b, l: (b, 0, 0)
    wide = jax.ShapeDtypeStruct((tokens, A_QK), BF16)
    nw = A_MAIN // COL_BLOCK
    return pl.pallas_call(
        functools.partial(_a_in_kernel, ts=ts, ns=ns),
        grid=(batch // ns, nl),
        in_specs=[pl.BlockSpec((tm, D_MODEL), row)] + _weight_specs(layer, D_MODEL, nw, 2) + [
            pl.BlockSpec((D_MODEL, LANES), const),
            pl.BlockSpec((CONV_W, CONV_DIM), const),
            pl.BlockSpec((ns, CONV_W - 1, CONV_DIM), perseq),
            pl.BlockSpec((2 * A_HEADS, 1), const),
            pl.BlockSpec((2 * A_HEADS, 1), const),
        ],
        out_specs=[
            pl.BlockSpec((tm, A_QK), row),
            pl.BlockSpec((tm, A_QK), row),
            pl.BlockSpec((tm, A_VW), row),
            pl.BlockSpec((tm, A_VW), row),
            pl.BlockSpec((tm, LANES), row),
            pl.BlockSpec((ns, CONV_W - 1, CONV_DIM), perseq),
        ],
        out_shape=[wide, wide, wide, wide,
                   jax.ShapeDtypeStruct((tokens, LANES), F32),
                   jax.ShapeDtypeStruct((batch, CONV_W - 1, CONV_DIM), F32)],
        scratch_shapes=[pltpu.VMEM((tm, CONV_DIM), F32), pltpu.VMEM((ns, SUBLANES, CONV_DIM), F32)],
        compiler_params=_params(("arbitrary", "arbitrary")),
        name="a_in",
    )(x2, *([w_in16] * nw), w_gate, conv_w, conv0, alog_col, dt_col)


def _delta_group(bi, q_ref, k_ref, v_ref, z_ref, gb_ref, nw, o_ref, s_ref, c):
    pk = LANES // c
    lc = int(math.log2(c))
    packs = [list(range(p * pk, (p + 1) * pk)) for p in range(A_HEADS // pk)]
    npk = range(len(packs))
    row = lax.broadcasted_iota(jnp.int32, (c, LANES), 0)
    lane = lax.broadcasted_iota(jnp.int32, (c, LANES), 1)
    colr = lane & (c - 1)
    member = lane >> lc
    eye = (row == colr).astype(F32)
    incl = row >= colr
    strict = row > colr
    diag8 = (row >> 3) == (colr >> 3)
    sq0 = lax.broadcasted_iota(jnp.int32, (LANES, LANES), 0)
    sq1 = lax.broadcasted_iota(jnp.int32, (LANES, LANES), 1)
    bd_mask = (sq0 >> lc) == (sq1 >> lc)
    kr0 = lax.broadcasted_iota(jnp.int32, (LANES, pk * LANES), 0)
    kr1 = lax.broadcasted_iota(jnp.int32, (LANES, pk * LANES), 1)
    k_mask = (kr0 >> lc) == (kr1 >> int(math.log2(LANES)))

    def bd(m):
        return jnp.where(bd_mask, jnp.concatenate([m] * pk, axis=0), 0.0)

    def by_member(vals):
        out = vals[0]
        for r in range(1, pk):
            out = jnp.where(member >= r, vals[r], out)
        return out

    def wide(col, hs):
        return jnp.concatenate([jnp.broadcast_to(col[h], (c, LANES)) for h in hs], axis=1)

    def hcol(r):
        return slice(r * LANES, (r + 1) * LANES)

    g = gb_ref[bi]
    gcum = g
    step = 1
    while step < c:
        gcum = gcum + jnp.where(row >= step, pltpu.roll(gcum, step, axis=0), 0.0)
        step *= 2
    gcum_t = jnp.concatenate([gcum] * pk, axis=0).T
    gc = [gcum[:, h:h + 1] for h in range(A_HEADS)]
    beta = [g[:, A_HEADS + h:A_HEADS + h + 1] for h in range(A_HEADS)]
    glast = [gcum[c - 1:c, h:h + 1] for h in range(A_HEADS)]
    eg = [jnp.exp(x) for x in gc]
    pcols = [slice(hs[0] * LANES, (hs[-1] + 1) * LANES) for hs in packs]
    kpf = [k_ref[bi, :, pc].astype(F32) for pc in pcols]
    qp16 = [q_ref[bi, :, pc] for pc in pcols]
    vpf = [v_ref[bi, :, pc].astype(F32) for pc in pcols]
    kbeta = [kpf[p] * wide(beta, hs) for p, hs in enumerate(packs)]
    decay = [jnp.exp(jnp.where(incl, by_member([jnp.broadcast_to(gc[h], (c, LANES)) for h in hs])
                               - by_member([gcum_t[h:h + 1, :] for h in hs]), -jnp.inf))
             for hs in packs]
    yield
    k_bd = [jnp.where(k_mask, jnp.concatenate([x] * pk, axis=0), 0.0).astype(BF16) for x in kpf]
    kq = [_mm_nt(jnp.concatenate([kbeta[p].astype(BF16), qp16[p]], axis=0), k_bd[p])
          for p in npk]
    yield
    a = [jnp.where(strict, kq[p][:c] * decay[p], 0.0) for p in npk]
    d = [jnp.where(diag8, x, 0.0) for x in a]
    yield
    d2 = [_mm(x, bd(x)) for x in d]
    yield
    d4 = [_mm(x, bd(x)) for x in d2]
    p1 = [_mm(eye - x, bd(eye + y)) for x, y in zip(d, d2)]
    yield
    xs = [_mm(p, bd(eye + y)) for p, y in zip(p1, d4)]
    yield
    shift = 3
    while (1 << shift) < c:
        mask = ((row >> (shift + 1)) == (colr >> (shift + 1))) & ((row >> shift) > (colr >> shift))
        xc = [_mm(x, bd(jnp.where(mask, m, 0.0))) for x, m in zip(xs, a)]
        yield
        xs = [x - _mm(y, bd(x)) for x, y in zip(xs, xc)]
        yield
        shift += 1
    uw = []
    for p, hs in enumerate(packs):
        vb = vpf[p] * wide(beta, hs)
        ke = kbeta[p] * wide(eg, hs)
        stacked = jnp.concatenate([jnp.concatenate([vb[:, hcol(r)], ke[:, hcol(r)]], axis=1) for r in range(pk)],
                                  axis=0)
        uw.append(_mm(bd(xs[p]), stacked))
    yield
    heads = [(p, r, hs[r]) for p, hs in enumerate(packs) for r in range(pk)]
    s = [s_ref[bi, h] for _, _, h in heads]
    wq = [_mm(jnp.concatenate([uw[p][r * c:(r + 1) * c, A_DV:], qp16[p][:, hcol(r)].astype(F32) * eg[h]], axis=0),
              s[i]) for i, (p, r, h) in enumerate(heads)]
    yield
    v_new = [uw[p][r * c:(r + 1) * c, :A_DV] - wq[i][:c] for i, (p, r, h) in enumerate(heads)]
    intra = [kq[p][c:] * decay[p] for p in npk]
    kd_t = [(kpf[p][:, hcol(r)] * jnp.exp(glast[h] - gc[h])).T for p, r, h in heads]
    ov = [_mm(jnp.concatenate([intra[p][:, r * c:(r + 1) * c], kd_t[i]], axis=0), v_new[i])
          for i, (p, r, h) in enumerate(heads)]
    yield
    for i, (p, r, h) in enumerate(heads):
        s_ref[bi, h] = s[i] * jnp.exp(glast[h]) + ov[i][c:]
        o = wq[i][c:] + ov[i][:c]
        zf = z_ref[bi, :, h * LANES:(h + 1) * LANES].astype(F32)
        gated = o * lax.rsqrt(jnp.mean(o * o, axis=-1, keepdims=True) + RMS_EPS) * nw * _silu(zf)
        o_ref[bi, :, h * LANES:(h + 1) * LANES] = gated.astype(o_ref.dtype)


def _run_interleaved(gens):
    live = list(gens)
    while live:
        for gen in list(live):
            try:
                next(gen)
            except StopIteration:
                live.remove(gen)


def _delta_kernel(q_ref, k_ref, v_ref, z_ref, gb_ref, s0_ref, nw_ref, o_ref, s_ref, *, c, nb):
    n = pl.program_id(1)

    @pl.when(n == 0)
    def _():
        s_ref[...] = s0_ref[...]

    nw = nw_ref[...]
    _run_interleaved([_delta_group(bi, q_ref, k_ref, v_ref, z_ref, gb_ref, nw, o_ref, s_ref, c) for bi in range(nb)])


def _delta(q, k, v, z, gb, s0_all, norm_w, *, layer, batch, seq):
    c = min(CHUNK, seq)
    n = seq // c
    nb = DELTA_BATCHES
    seq3 = lambda a: a.reshape(batch, seq, a.shape[-1])
    blk = lambda b, i: (b, i, 0)
    state = lambda b, i: (b, 0, 0, 0)
    o, s_new = pl.pallas_call(
        functools.partial(_delta_kernel, c=c, nb=nb),
        grid=(batch // nb, n),
        in_specs=[
            pl.BlockSpec((nb, c, A_QK), blk),
            pl.BlockSpec((nb, c, A_QK), blk),
            pl.BlockSpec((nb, c, A_VW), blk),
            pl.BlockSpec((nb, c, A_VW), blk),
            pl.BlockSpec((nb, c, LANES), blk),
            pl.BlockSpec((None, nb, A_HEADS, A_DK, A_DV), lambda b, i: (layer, b, 0, 0, 0)),
            pl.BlockSpec((1, LANES), lambda b, i: (0, 0)),
        ],
        out_specs=[
            pl.BlockSpec((nb, c, A_VW), blk),
            pl.BlockSpec((nb, A_HEADS, A_DK, A_DV), state),
        ],
        out_shape=[jax.ShapeDtypeStruct((batch, seq, A_VW), BF16),
                   jax.ShapeDtypeStruct((batch, A_HEADS, A_DK, A_DV), F32)],
        compiler_params=_params(("arbitrary", "arbitrary")),
        name="delta_rule",
    )(seq3(q), seq3(k), seq3(v), seq3(z), seq3(gb), s0_all, norm_w)
    return o.reshape(batch * seq, A_VW), s_new


def _out_ln_kernel(*refs, tm, sub):
    nw = D_MODEL // COL_BLOCK
    o_ref, x_ref, w_refs = refs[0], refs[1], refs[2:2 + nw]
    g_ref, b_ref, y_ref = refs[2 + nw:]
    for r0 in range(0, tm, sub):
        rows = slice(r0, r0 + sub)
        o = o_ref[rows, :]
        proj = jnp.concatenate([jnp.dot(o, w[...], preferred_element_type=F32) for w in w_refs], axis=1)
        r = DN_ALPHA * x_ref[rows, :] + proj
        mu = jnp.mean(r, axis=-1, keepdims=True)
        d = r - mu
        var = jnp.mean(d * d, axis=-1, keepdims=True)
        y_ref[rows, :] = d * lax.rsqrt(var + LN_EPS) * g_ref[...] + b_ref[...]


def _out_ln(o, x2, w_out16, ln_g, ln_b, *, layer):
    tokens = x2.shape[0]
    tm = min(OUT_TILE, tokens)
    sub = min(ROW_TILE, tm)
    row = lambda i: (i, 0)
    const = lambda i: (0, 0)
    nw = D_MODEL // COL_BLOCK
    return pl.pallas_call(
        functools.partial(_out_ln_kernel, tm=tm, sub=sub),
        grid=(tokens // tm,),
        in_specs=[pl.BlockSpec((tm, o.shape[1]), row), pl.BlockSpec((tm, D_MODEL), row)]
        + _weight_specs(layer, o.shape[1], nw, 1)
        + [pl.BlockSpec((1, D_MODEL), const), pl.BlockSpec((1, D_MODEL), const)],
        out_specs=pl.BlockSpec((tm, D_MODEL), row),
        out_shape=jax.ShapeDtypeStruct((tokens, D_MODEL), F32),
        compiler_params=_params(("arbitrary",)),
        name="out_ln",
    )(o, x2, *([w_out16] * nw), ln_g, ln_b)


def _rope_slab(x, cos_t, sin_up, sin_dn):
    half = ROPE_DIMS // 2
    return x * cos_t + pltpu.roll(x, LANES - half, axis=1) * sin_up + pltpu.roll(x, half, axis=1) * sin_dn


def _row_tiling(tokens, seq):
    tm = min(WIDE_TILE, tokens)
    if seq >= tm:
        per_seq = seq // tm
        return tm, 1, (lambda i: (i % per_seq, 0))
    return tm, tm // seq, (lambda i: (0, 0))


def _tile_tables(tables, reps):
    return tables if reps == 1 else tuple(jnp.tile(t, (reps, 1)) for t in tables)


def _kv_kernel(h_ref, w_ref, cos_ref, sup_ref, sdn_ref, k_ref, v_ref, klast_ref, vlast_ref, *, rows_per_seq, ns, keep):
    kv = jnp.dot(h_ref[...].astype(BF16), w_ref[...], preferred_element_type=F32)
    cos_t, sin_up, sin_dn = cos_ref[...], sup_ref[...], sdn_ref[...]
    k = jnp.concatenate([_rope_slab(kv[:, s * LANES:(s + 1) * LANES], cos_t, sin_up, sin_dn)
                         for s in range(B_KVW // LANES)], axis=1)
    v = kv[:, B_KVW:]
    k_ref[...] = k.astype(k_ref.dtype)
    v_ref[...] = v.astype(v_ref.dtype)
    for s in range(ns):
        end = (s + 1) * rows_per_seq
        klast_ref[s] = k[end - keep:end, :]
        vlast_ref[s] = v[end - keep:end, :]


def _shared_kv(h2, w_kv16, tables, *, batch, seq):
    tokens = batch * seq
    tm, ns, tab = _row_tiling(tokens, seq)
    keep = min(WINDOW, seq)
    rows_per_seq = tm // ns
    assert keep <= rows_per_seq
    per_seq = max(1, seq // tm)
    row = lambda i: (i, 0)
    lastb = lambda i: (i // per_seq, 0, 0)
    out = jax.ShapeDtypeStruct((tokens, B_KVW), BF16)
    last = jax.ShapeDtypeStruct((batch, keep, B_KVW), F32)
    return pl.pallas_call(
        functools.partial(_kv_kernel, rows_per_seq=rows_per_seq, ns=ns, keep=keep),
        grid=(tokens // tm,),
        in_specs=[
            pl.BlockSpec((tm, D_MODEL), row),
            pl.BlockSpec((D_MODEL, 2 * B_KVW), lambda i: (0, 0)),
            pl.BlockSpec((tm, LANES), tab),
            pl.BlockSpec((tm, LANES), tab),
            pl.BlockSpec((tm, LANES), tab),
        ],
        out_specs=[pl.BlockSpec((tm, B_KVW), row), pl.BlockSpec((tm, B_KVW), row),
                   pl.BlockSpec((ns, keep, B_KVW), lastb), pl.BlockSpec((ns, keep, B_KVW), lastb)],
        out_shape=[out, out, last, last],
        compiler_params=_params(("arbitrary",)),
        name="shared_kv",
    )(h2, w_kv16, *_tile_tables(tables, ns))


def _b_in_kernel(*refs, tm):
    nw = 2 * B_QW // COL_BLOCK
    x_ref, w_refs = refs[0], refs[1:1 + nw]
    cos_ref, sup_ref, sdn_ref, qx_ref, z_ref = refs[1 + nw:]
    xb = x_ref[...].astype(BF16)
    cos_t, sin_up, sin_dn = cos_ref[...], sup_ref[...], sdn_ref[...]
    lane_half = lax.broadcasted_iota(jnp.int32, (tm, LANES), 1) >> int(math.log2(B_HD))
    cb = COL_BLOCK
    nq = B_QW // cb
    for j in range(nq):
        proj = jnp.dot(xb, w_refs[j][...], preferred_element_type=F32)
        for sl in range(cb // LANES):
            slab = j * (cb // LANES) + sl
            rot = _rope_slab(proj[:, sl * LANES:(sl + 1) * LANES], cos_t, sin_up, sin_dn) * (B_HD ** -0.5 * LOG2E)
            for p in range(2):
                hq = 2 * slab + p
                x = jnp.where(lane_half == p, rot, 0.0)
                if p != (hq // B_GROUP) % 2:
                    x = pltpu.roll(x, B_HD, axis=1)
                qx_ref[:, hq * LANES:(hq + 1) * LANES] = x.astype(qx_ref.dtype)
    for j in range(nq):
        z_ref[:, j * cb:(j + 1) * cb] = jnp.dot(
            xb, w_refs[nq + j][...], preferred_element_type=F32).astype(z_ref.dtype)


def _b_in(x2, w_in16, tables, *, layer, batch, seq):
    tokens = batch * seq
    tm, ns, tab = _row_tiling(tokens, seq)
    row = lambda i: (i, 0)
    nw = 2 * B_QW // COL_BLOCK
    return pl.pallas_call(
        functools.partial(_b_in_kernel, tm=tm),
        grid=(tokens // tm,),
        in_specs=[pl.BlockSpec((tm, D_MODEL), row)] + _weight_specs(layer, D_MODEL, nw, 1) + [
            pl.BlockSpec((tm, LANES), tab),
            pl.BlockSpec((tm, LANES), tab),
            pl.BlockSpec((tm, LANES), tab),
        ],
        out_specs=[pl.BlockSpec((tm, B_QX), row), pl.BlockSpec((tm, B_QW), row)],
        out_shape=[jax.ShapeDtypeStruct((tokens, B_QX), BF16), jax.ShapeDtypeStruct((tokens, B_QW), BF16)],
        compiler_params=_params(("arbitrary",)),
        name="b_in",
    )(x2, *([w_in16] * nw), *_tile_tables(tables, ns))


def _attn_body(sink_ref, qx_ref, z_ref, k_ref, v_ref, o_ref, *, lq, lk, nc, masked):
    step = pl.program_id(1)
    lane_half = lax.broadcasted_iota(jnp.int32, (lq, LANES), 1) >> int(math.log2(B_HD))
    kv_slabs = range(B_KVW // LANES)
    kwin, vwin, valid = [], [], []
    for ci in range(nc):
        if masked is None:
            start = 0
            valid.append(None)
        else:
            c = step * nc + ci
            first = jnp.maximum(c - WINDOW_CHUNKS, 0)
            start = pl.multiple_of(first * CHUNK, CHUNK)
            key_chunk = first + (lax.broadcasted_iota(jnp.int32, (lq, lk), 1) >> int(math.log2(CHUNK)))
            valid.append((key_chunk <= c) if masked else None)
        kwin.append([k_ref[0, pl.ds(start, lk), s * LANES:(s + 1) * LANES] for s in kv_slabs])
        vwin.append([v_ref[0, pl.ds(start, lk), s * LANES:(s + 1) * LANES] for s in kv_slabs])
    units = [(ci, j) for ci in range(nc) for j in range(B_KV_HEADS)]
    qstack = [jnp.concatenate([qx_ref[ci * lq:(ci + 1) * lq, (B_GROUP * j + g) * LANES:(B_GROUP * j + g + 1) * LANES]
                               for g in range(B_GROUP)], axis=0) for ci, j in units]
    scores = [_mm_nt(qstack[u], kwin[ci][j // 2]) for u, (ci, j) in enumerate(units)]
    heads = [(u, ci, j, g) for u, (ci, j) in enumerate(units) for g in range(B_GROUP)]
    sk = [sink_ref[B_GROUP * j + g] * LOG2E for _, _, j, g in heads]
    sc = [scores[u][g * lq:(g + 1) * lq] for u, _, _, g in heads]
    sc = [s if valid[ci] is None else jnp.where(valid[ci], s, -jnp.inf) for s, (_, ci, _, _) in zip(sc, heads)]
    mx = [jnp.maximum(jnp.max(s, axis=-1, keepdims=True), k) for s, k in zip(sc, sk)]
    pstack = [jnp.concatenate([jnp.exp2(sc[h] - mx[h]).astype(BF16) for h in range(u * B_GROUP, (u + 1) * B_GROUP)],
                              axis=0) for u in range(len(units))]
    ones = jnp.ones((lk, LANES), BF16)
    pv = [_mm(pstack[u], jnp.concatenate([vwin[ci][j // 2], ones], axis=1))
          for u, (ci, j) in enumerate(units)]
    out = []
    for h, (u, _, _, g) in enumerate(heads):
        part = pv[u][g * lq:(g + 1) * lq]
        out.append(part[:, :LANES] / (part[:, LANES:] + jnp.exp2(sk[h] - mx[h])))
    for ci in range(nc):
        rows = slice(ci * lq, (ci + 1) * lq)
        for slab in range(B_QW // LANES):
            pair, g = divmod(slab, B_GROUP)
            lo = out[(ci * B_KV_HEADS + 2 * pair) * B_GROUP + g]
            hi = out[(ci * B_KV_HEADS + 2 * pair + 1) * B_GROUP + g]
            both = jnp.where(lane_half == 0, lo, hi)
            zs = z_ref[rows, slab * LANES:(slab + 1) * LANES].astype(F32)
            o_ref[rows, slab * LANES:(slab + 1) * LANES] = (both * _silu(zs)).astype(o_ref.dtype)


def _attn_kernel(sink_ref, qx_ref, z_ref, k_ref, v_ref, o_ref, *, lq, lk, nc, banded):
    body = functools.partial(_attn_body, sink_ref, qx_ref, z_ref, k_ref, v_ref, o_ref, lq=lq, lk=lk, nc=nc)
    if not banded:
        body(masked=None)
        return
    step = pl.program_id(1)
    clamped_steps = -(-WINDOW_CHUNKS // nc)

    @pl.when(step < clamped_steps)
    def _():
        body(masked=True)

    @pl.when(step >= clamped_steps)
    def _():
        body(masked=False)


def _attention(qx, z, k3, v3, sinks, *, batch, seq, banded):
    lq = min(CHUNK, seq)
    nc = min(ATTN_CHUNKS, seq // lq)
    nq = seq // (lq * nc)
    ltot = k3.shape[1]
    lk = (WINDOW_CHUNKS + 1) * CHUNK if banded else ltot
    tokens = batch * seq
    row = lambda b, c: (b * nq + c, 0)
    whole = lambda b, c: (b, 0, 0)
    return pl.pallas_call(
        functools.partial(_attn_kernel, lq=lq, lk=lk, nc=nc, banded=banded),
        grid=(batch, nq),
        in_specs=[
            pl.BlockSpec(memory_space=pltpu.SMEM),
            pl.BlockSpec((nc * lq, B_QX), row),
            pl.BlockSpec((nc * lq, B_QW), row),
            pl.BlockSpec((1, ltot, B_KVW), whole),
            pl.BlockSpec((1, ltot, B_KVW), whole),
        ],
        out_specs=pl.BlockSpec((nc * lq, B_QW), row),
        out_shape=jax.ShapeDtypeStruct((tokens, B_QW), BF16),
        compiler_params=_params(("arbitrary", "arbitrary")),
        name="swa_attention",
    )(sinks, qx, z, k3, v3)


def _rope_tables(pos):
    half = ROPE_DIMS // 2
    inv = ROPE_THETA ** (-jnp.arange(half, dtype=F32) * 2.0 / ROPE_DIMS)
    ang = pos.astype(F32)[:, None] * inv[None, :]
    cos, sin = jnp.cos(ang), jnp.sin(ang)
    ones = jnp.ones((pos.shape[0], B_HD - ROPE_DIMS), F32)
    zeros_h = jnp.zeros((pos.shape[0], half), F32)
    zeros_r = jnp.zeros((pos.shape[0], B_HD - ROPE_DIMS), F32)
    cos_head = jnp.concatenate([cos, cos, ones], axis=1)
    up_head = jnp.concatenate([-sin, zeros_h, zeros_r], axis=1)
    dn_head = jnp.concatenate([zeros_h, sin, zeros_r], axis=1)
    rep = LANES // B_HD
    return tuple(jnp.tile(t, (1, rep)) for t in (cos_head, up_head, dn_head))


def _trunk(x, pos, conv_state, delta_state, past_k, past_v, wts):
    batch, seq, _ = x.shape
    tokens = batch * seq
    h = x.reshape(tokens, D_MODEL)
    tables = _rope_tables(pos)
    new_conv, new_delta = [], []
    for i in range(N_A_LAYERS):
        q, k, v, z, gb, cbuf = _a_in(h, wts["a_w_in"], wts["a_w_gate"][i], wts["a_conv_w"][i], conv_state[i],
                                     wts["a_log"][i], wts["a_dt"][i], layer=i, batch=batch, seq=seq)
        o, s_new = _delta(q, k, v, z, gb, delta_state, wts["a_norm_w"][i], layer=i, batch=batch, seq=seq)
        h = _out_ln(o, h, wts["a_w_out"], wts["a_ln_g"][i], wts["a_ln_b"][i], layer=i)
        new_conv.append(cbuf)
        new_delta.append(s_new)
    k2, v2, k_last, v_last = _shared_kv(h, wts["b_w_kv"], tables, batch=batch, seq=seq)
    k3 = k2.reshape(batch, seq, B_KVW)
    v3 = v2.reshape(batch, seq, B_KVW)
    cached = past_k is not None
    if cached:
        pk = past_k.reshape(batch, -1, B_KVW)
        pv = past_v.reshape(batch, -1, B_KVW)
        k3 = jnp.concatenate([pk.astype(BF16), k3], axis=1)
        v3 = jnp.concatenate([pv.astype(BF16), v3], axis=1)
        k_last = jnp.concatenate([pk, k_last], axis=1)[:, -WINDOW:]
        v_last = jnp.concatenate([pv, v_last], axis=1)[:, -WINDOW:]
    new_k = k_last.reshape(batch, WINDOW, B_KV_HEADS, B_HD)
    new_v = v_last.reshape(batch, WINDOW, B_KV_HEADS, B_HD)
    for j in range(N_B_LAYERS):
        qx, z = _b_in(h, wts["b_w_in"], tables, layer=j, batch=batch, seq=seq)
        o = _attention(qx, z, k3, v3, wts["b_sinks"][j], batch=batch, seq=seq, banded=not cached)
        h = _out_ln(o, h, wts["b_w_out"], wts["b_ln_g"][j], wts["b_ln_b"][j], layer=j)
    return h.reshape(batch, seq, D_MODEL), jnp.stack(new_conv), jnp.stack(new_delta), new_k, new_v


def _paired_heads(w, axis):
    order = [B_GROUP * (2 * pair + odd) + g
             for pair in range(B_KV_HEADS // 2) for g in range(B_GROUP) for odd in range(2)]
    shape = w.shape
    blocks = w.reshape(shape[:axis] + (B_Q_HEADS, B_HD) + shape[axis + 1:])
    return jnp.take(blocks, jnp.array(order, jnp.int32), axis=axis).reshape(shape)


def kernel(x_prompt, x_sample, state_delta, state_conv, cache_k, cache_v, a_w_in, a_conv_w, a_log, a_dt_bias,
           a_norm_w, a_w_out, a_ln_g, a_ln_b, b_w_kv, b_w_in, b_sinks, b_w_out, b_ln_g, b_ln_b):
    a_w_in16 = a_w_in.astype(BF16)
    zeros_h = jnp.zeros((N_A_LAYERS, A_HEADS), F32)
    wts = {
        "a_w_in": a_w_in16,
        "a_w_gate": jnp.pad(a_w_in16[:, :, A_MAIN:], ((0, 0), (0, 0), (0, LANES - 2 * A_HEADS))),
        "a_conv_w": a_conv_w,
        "a_log": jnp.concatenate([a_log.astype(F32), zeros_h], axis=1)[:, :, None],
        "a_dt": jnp.concatenate([a_dt_bias.astype(F32), zeros_h], axis=1)[:, :, None],
        "a_norm_w": a_norm_w.reshape(N_A_LAYERS, 1, A_DV),
        "a_w_out": a_w_out.astype(BF16),
        "a_ln_g": a_ln_g.reshape(N_A_LAYERS, 1, D_MODEL),
        "a_ln_b": a_ln_b.reshape(N_A_LAYERS, 1, D_MODEL),
        "b_w_kv": b_w_kv.astype(BF16),
        "b_w_in": jnp.concatenate([b_w_in[:, :, :B_QW], _paired_heads(b_w_in[:, :, B_QW:], axis=2)],
                                  axis=2).astype(BF16),
        "b_sinks": b_sinks,
        "b_w_out": _paired_heads(b_w_out, axis=1).astype(BF16),
        "b_ln_g": b_ln_g.reshape(N_B_LAYERS, 1, D_MODEL),
        "b_ln_b": b_ln_b.reshape(N_B_LAYERS, 1, D_MODEL),
    }
    bp, lp, _ = x_prompt.shape
    bs, ls, _ = x_sample.shape
    pos_prompt = jnp.arange(lp, dtype=jnp.int32)
    pos_sample = PAST_LEN + jnp.arange(ls, dtype=jnp.int32)
    zero_conv = jnp.zeros((N_A_LAYERS, bp, CONV_W - 1, CONV_DIM), F32)
    zero_delta = jnp.zeros((N_A_LAYERS, bp, A_HEADS, A_DK, A_DV), F32)
    y_p, p_conv, p_delta, p_k, p_v = _trunk(x_prompt, pos_prompt, zero_conv, zero_delta, None, None, wts)
    y_s, s_conv, s_delta, s_k, s_v = _trunk(x_sample, pos_sample, state_conv, state_delta, cache_k, cache_v, wts)
    return (y_p, y_s, p_delta, p_conv, p_k, p_v, s_delta, s_conv, s_k, s_v)
```

```python
import functools
import math

import jax
import jax.numpy as jnp
from jax import lax
from jax.experimental import pallas as pl
from jax.experimental.pallas import tpu as pltpu

D_MODEL = 1024
DEPTH = 4
PAST_LEN = 4096
CHUNK = 64
N_A_LAYERS = DEPTH // 2
N_B_LAYERS = DEPTH - N_A_LAYERS
A_HEADS = 8
A_DK = 128
A_DV = 128
A_QK = A_HEADS * A_DK
A_VW = A_HEADS * A_DV
CONV_W = 4
CONV_DIM = 2 * A_QK + A_VW
A_MAIN = CONV_DIM + A_VW
B_Q_HEADS = 16
B_KV_HEADS = 4
B_GROUP = B_Q_HEADS // B_KV_HEADS
B_HD = 64
B_QW = B_Q_HEADS * B_HD
B_KVW = B_KV_HEADS * B_HD
WINDOW = 128
WINDOW_CHUNKS = WINDOW // CHUNK
ROPE_DIMS = B_HD // 4
ROPE_THETA = 500000.0
DN_ALPHA = (2 * DEPTH) ** 0.25
LN_EPS = 1e-5
RMS_EPS = 1e-6

LANES = 128
SUBLANES = 8
VMEM_LIMIT = 48 * 1024 * 1024
ROW_TILE = 256
WIDE_TILE = 512
OUT_TILE = 1024
COL_BLOCK = 512
DELTA_BATCHES = 2
ATTN_CHUNKS = 4
B_QX = B_Q_HEADS * LANES
LOG2E = math.log2(math.e)

F32 = jnp.float32
BF16 = jnp.bfloat16


def _mm(a, b):
    return jnp.dot(a.astype(BF16), b.astype(BF16), preferred_element_type=F32)


def _mm_nt(a, b):
    return lax.dot_general(a.astype(BF16), b.astype(BF16), (((1,), (1,)), ((), ())),
                           preferred_element_type=F32)


def _silu(x):
    return x * (1.0 / (1.0 + jnp.exp(-x)))


def _params(semantics):
    return pltpu.CompilerParams(dimension_semantics=semantics, vmem_limit_bytes=VMEM_LIMIT)


def _weight_specs(layer, kdim, nblocks, grid_rank):
    def spec(j):
        if grid_rank == 1:
            return pl.BlockSpec((None, kdim, COL_BLOCK), lambda i: (layer, 0, j))
        return pl.BlockSpec((None, kdim, COL_BLOCK), lambda b, l: (layer, 0, j))
    return [spec(j) for j in range(nblocks)]


def _a_in_kernel(*refs, ts, ns):
    nw = A_MAIN // COL_BLOCK
    x_ref, w_refs = refs[0], refs[1:1 + nw]
    (wg_ref, cw_ref, c0_ref, alog_ref, dt_ref,
     q_ref, k_ref, v_ref, z_ref, gb_ref, cout_ref, pbuf, carry) = refs[1 + nw:]
    tm = ts * ns
    l = pl.program_id(1)
    tail = SUBLANES - (CONV_W - 1)

    @pl.when(l == 0)
    def _():
        for s in range(ns):
            carry[s, tail:SUBLANES, :] = c0_ref[s]

    xb = x_ref[...].astype(BF16)
    cb = COL_BLOCK
    nconv = CONV_DIM // cb
    outs = (q_ref, k_ref, v_ref)

    def conv_block(s, blk):
        rows = slice(s * ts, (s + 1) * ts)
        cols = slice(blk * LANES, (blk + 1) * LANES)
        ext = jnp.concatenate([carry[s, :, cols], pbuf[rows, cols]], axis=0)
        y = None
        for j in range(CONV_W):
            back = CONV_W - 1 - j
            tap = (pltpu.roll(ext, back, axis=0) if back else ext)[SUBLANES:] * cw_ref[j:j + 1, cols]
            y = tap if y is None else y + tap
        y = _silu(y)
        which, head = divmod(blk, A_HEADS)
        if which < 2:
            y = y * lax.rsqrt(jnp.sum(y * y, axis=-1, keepdims=True) + RMS_EPS)
            if which == 0:
                y = y * (A_DK ** -0.5)
        outs[which][rows, head * LANES:(head + 1) * LANES] = y.astype(outs[which].dtype)

    for j in range(nconv):
        pbuf[:, j * cb:(j + 1) * cb] = jnp.dot(xb, w_refs[j][...], preferred_element_type=F32)
        for s in range(ns):
            for blk in range(j * cb // LANES, (j + 1) * cb // LANES):
                conv_block(s, blk)
    for j in range(A_VW // cb):
        z_ref[:, j * cb:(j + 1) * cb] = jnp.dot(
            xb, w_refs[nconv + j][...], preferred_element_type=F32).astype(z_ref.dtype)

    gates = jnp.dot(xb, wg_ref[...], preferred_element_type=F32)
    gt = gates.T[:2 * A_HEADS]
    rowi = lax.broadcasted_iota(jnp.int32, gt.shape, 0)
    sp_in = gt + dt_ref[...]
    softplus = jnp.maximum(sp_in, 0.0) + jnp.log1p(jnp.exp(-jnp.abs(sp_in)))
    gval = -jnp.exp(alog_ref[...]) * softplus
    bval = 1.0 / (1.0 + jnp.exp(-gt))
    res = jnp.where(rowi < A_HEADS, gval, bval)
    gb_ref[...] = jnp.concatenate([res, jnp.zeros((LANES - 2 * A_HEADS, tm), F32)], axis=0).T

    for s in range(ns):
        last = pbuf[(s + 1) * ts - (CONV_W - 1):(s + 1) * ts, :]
        cout_ref[s] = last
        carry[s, tail:SUBLANES, :] = last


def _a_in(x2, w_in16, w_gate, conv_w, conv0, alog_col, dt_col, *, layer, batch, seq):
    if seq >= WIDE_TILE:
        ts, ns = WIDE_TILE, 1
    else:
        ts, ns = seq, min(batch, WIDE_TILE // seq)
    tm = ts * ns
    nl = seq // ts
    tokens = batch * seq
    row = lambda b, l: (b * nl + l, 0)
    const = lambda b, l: (0, 0)
    perseq = lambda b, l: (b, 0, 0)
    wide = jax.ShapeDtypeStruct((tokens, A_QK), BF16)
    nw = A_MAIN // COL_BLOCK
    return pl.pallas_call(
        functools.partial(_a_in_kernel, ts=ts, ns=ns),
        grid=(batch // ns, nl),
        in_specs=[pl.BlockSpec((tm, D_MODEL), row)] + _weight_specs(layer, D_MODEL, nw, 2) + [
            pl.BlockSpec((D_MODEL, LANES), const),
            pl.BlockSpec((CONV_W, CONV_DIM), const),
            pl.BlockSpec((ns, CONV_W - 1, CONV_DIM), perseq),
            pl.BlockSpec((2 * A_HEADS, 1), const),
            pl.BlockSpec((2 * A_HEADS, 1), const),
        ],
        out_specs=[
            pl.BlockSpec((tm, A_QK), row),
            pl.BlockSpec((tm, A_QK), row),
            pl.BlockSpec((tm, A_VW), row),
            pl.BlockSpec((tm, A_VW), row),
            pl.BlockSpec((tm, LANES), row),
            pl.BlockSpec((ns, CONV_W - 1, CONV_DIM), perseq),
        ],
        out_shape=[wide, wide, wide, wide,
                   jax.ShapeDtypeStruct((tokens, LANES), F32),
                   jax.ShapeDtypeStruct((batch, CONV_W - 1, CONV_DIM), F32)],
        scratch_shapes=[pltpu.VMEM((tm, CONV_DIM), F32), pltpu.VMEM((ns, SUBLANES, CONV_DIM), F32)],
        compiler_params=_params(("arbitrary", "arbitrary")),
        name="a_in",
    )(x2, *([w_in16] * nw), w_gate, conv_w, conv0, alog_col, dt_col)


def _delta_group(bi, q_ref, k_ref, v_ref, z_ref, gb_ref, nw, o_ref, s_ref, c):
    pk = LANES // c
    lc = int(math.log2(c))
    packs = [list(range(p * pk, (p + 1) * pk)) for p in range(A_HEADS // pk)]
    npk = range(len(packs))
    row = lax.broadcasted_iota(jnp.int32, (c, LANES), 0)
    lane = lax.broadcasted_iota(jnp.int32, (c, LANES), 1)
    colr = lane & (c - 1)
    member = lane >> lc
    eye = (row == colr).astype(F32)
    incl = row >= colr
    strict = row > colr
    diag8 = (row >> 3) == (colr >> 3)
    sq0 = lax.broadcasted_iota(jnp.int32, (LANES, LANES), 0)
    sq1 = lax.broadcasted_iota(jnp.int32, (LANES, LANES), 1)
    bd_mask = (sq0 >> lc) == (sq1 >> lc)
    kr0 = lax.broadcasted_iota(jnp.int32, (LANES, pk * LANES), 0)
    kr1 = lax.broadcasted_iota(jnp.int32, (LANES, pk * LANES), 1)
    k_mask = (kr0 >> lc) == (kr1 >> int(math.log2(LANES)))

    def bd(m):
        return jnp.where(bd_mask, jnp.concatenate([m] * pk, axis=0), 0.0)

    def by_member(vals):
        out = vals[0]
        for r in range(1, pk):
            out = jnp.where(member >= r, vals[r], out)
        return out

    def wide(col, hs):
        return jnp.concatenate([jnp.broadcast_to(col[h], (c, LANES)) for h in hs], axis=1)

    def hcol(r):
        return slice(r * LANES, (r + 1) * LANES)

    g = gb_ref[bi]
    gcum = g
    step = 1
    while step < c:
        gcum = gcum + jnp.where(row >= step, pltpu.roll(gcum, step, axis=0), 0.0)
        step *= 2
    gcum_t = jnp.concatenate([gcum] * pk, axis=0).T
    gc = [gcum[:, h:h + 1] for h in range(A_HEADS)]
    beta = [g[:, A_HEADS + h:A_HEADS + h + 1] for h in range(A_HEADS)]
    glast = [gcum[c - 1:c, h:h + 1] for h in range(A_HEADS)]
    eg = [jnp.exp(x) for x in gc]
    pcols = [slice(hs[0] * LANES, (hs[-1] + 1) * LANES) for hs in packs]
    kpf = [k_ref[bi, :, pc].astype(F32) for pc in pcols]
    qp16 = [q_ref[bi, :, pc] for pc in pcols]
    vpf = [v_ref[bi, :, pc].astype(F32) for pc in pcols]
    kbeta = [kpf[p] * wide(beta, hs) for p, hs in enumerate(packs)]
    decay = [jnp.exp(jnp.where(incl, by_member([jnp.broadcast_to(gc[h], (c, LANES)) for h in hs])
                               - by_member([gcum_t[h:h + 1, :] for h in hs]), -jnp.inf))
             for hs in packs]
    yield
    k_bd = [jnp.where(k_mask, jnp.concatenate([x] * pk, axis=0), 0.0).astype(BF16) for x in kpf]
    kq = [_mm_nt(jnp.concatenate([kbeta[p].astype(BF16), qp16[p]], axis=0), k_bd[p])
          for p in npk]
    yield
    a = [jnp.where(strict, kq[p][:c] * decay[p], 0.0) for p in npk]
    d = [jnp.where(diag8, x, 0.0) for x in a]
    heads = [(p, r, hs[r]) for p, hs in enumerate(packs) for r in range(pk)]
    s = [s_ref[bi, h] for _, _, h in heads]
    ks = [_mm(jnp.concatenate([kbeta[p][:, hcol(r)] * eg[h], qp16[p][:, hcol(r)].astype(F32) * eg[h]], axis=0), s[i])
          for i, (p, r, h) in enumerate(heads)]
    yield
    d2 = [_mm(x, bd(x)) for x in d]
    yield
    d4 = [_mm(x, bd(x)) for x in d2]
    p1 = [_mm(eye - x, bd(eye + y)) for x, y in zip(d, d2)]
    yield
    xs = [_mm(p, bd(eye + y)) for p, y in zip(p1, d4)]
    yield
    shift = 3
    while (1 << shift) < c:
        mask = ((row >> (shift + 1)) == (colr >> (shift + 1))) & ((row >> shift) > (colr >> shift))
        xc = [_mm(x, bd(jnp.where(mask, m, 0.0))) for x, m in zip(xs, a)]
        yield
        xs = [x - _mm(y, bd(x)) for x, y in zip(xs, xc)]
        yield
        shift += 1
    vn = []
    for p, hs in enumerate(packs):
        vb = vpf[p] * wide(beta, hs)
        stacked = jnp.concatenate([vb[:, hcol(r)] - ks[p * pk + r][:c] for r in range(pk)], axis=0)
        vn.append(_mm(bd(xs[p]), stacked))
    yield
    v_new = [vn[p][r * c:(r + 1) * c] for p, r, _ in heads]
    intra = [kq[p][c:] * decay[p] for p in npk]
    kd_t = [(kpf[p][:, hcol(r)] * jnp.exp(glast[h] - gc[h])).T for p, r, h in heads]
    ov = [_mm(jnp.concatenate([intra[p][:, r * c:(r + 1) * c], kd_t[i]], axis=0), v_new[i])
          for i, (p, r, h) in enumerate(heads)]
    yield
    for i, (p, r, h) in enumerate(heads):
        s_ref[bi, h] = s[i] * jnp.exp(glast[h]) + ov[i][c:]
        o = ks[i][c:] + ov[i][:c]
        zf = z_ref[bi, :, h * LANES:(h + 1) * LANES].astype(F32)
        gated = o * lax.rsqrt(jnp.mean(o * o, axis=-1, keepdims=True) + RMS_EPS) * nw * _silu(zf)
        o_ref[bi, :, h * LANES:(h + 1) * LANES] = gated.astype(o_ref.dtype)


def _run_interleaved(gens):
    live = list(gens)
    while live:
        for gen in list(live):
            try:
                next(gen)
            except StopIteration:
                live.remove(gen)


def _delta_kernel(q_ref, k_ref, v_ref, z_ref, gb_ref, s0_ref, nw_ref, o_ref, s_ref, *, c, nb):
    n = pl.program_id(1)

    @pl.when(n == 0)
    def _():
        s_ref[...] = s0_ref[...]

    nw = nw_ref[...]
    _run_interleaved([_delta_group(bi, q_ref, k_ref, v_ref, z_ref, gb_ref, nw, o_ref, s_ref, c) for bi in range(nb)])


def _delta(q, k, v, z, gb, s0_all, norm_w, *, layer, batch, seq):
    c = min(CHUNK, seq)
    n = seq // c
    nb = DELTA_BATCHES
    seq3 = lambda a: a.reshape(batch, seq, a.shape[-1])
    blk = lambda b, i: (b, i, 0)
    state = lambda b, i: (b, 0, 0, 0)
    o, s_new = pl.pallas_call(
        functools.partial(_delta_kernel, c=c, nb=nb),
        grid=(batch // nb, n),
        in_specs=[
            pl.BlockSpec((nb, c, A_QK), blk),
            pl.BlockSpec((nb, c, A_QK), blk),
            pl.BlockSpec((nb, c, A_VW), blk),
            pl.BlockSpec((nb, c, A_VW), blk),
            pl.BlockSpec((nb, c, LANES), blk),
            pl.BlockSpec((None, nb, A_HEADS, A_DK, A_DV), lambda b, i: (layer, b, 0, 0, 0)),
            pl.BlockSpec((1, LANES), lambda b, i: (0, 0)),
        ],
        out_specs=[
            pl.BlockSpec((nb, c, A_VW), blk),
            pl.BlockSpec((nb, A_HEADS, A_DK, A_DV), state),
        ],
        out_shape=[jax.ShapeDtypeStruct((batch, seq, A_VW), BF16),
                   jax.ShapeDtypeStruct((batch, A_HEADS, A_DK, A_DV), F32)],
        compiler_params=_params(("arbitrary", "arbitrary")),
        name="delta_rule",
    )(seq3(q), seq3(k), seq3(v), seq3(z), seq3(gb), s0_all, norm_w)
    return o.reshape(batch * seq, A_VW), s_new


def _out_ln_kernel(*refs, tm, sub):
    nw = D_MODEL // COL_BLOCK
    o_ref, x_ref, w_refs = refs[0], refs[1], refs[2:2 + nw]
    g_ref, b_ref, y_ref = refs[2 + nw:]
    for r0 in range(0, tm, sub):
        rows = slice(r0, r0 + sub)
        o = o_ref[rows, :]
        proj = jnp.concatenate([jnp.dot(o, w[...], preferred_element_type=F32) for w in w_refs], axis=1)
        r = DN_ALPHA * x_ref[rows, :] + proj
        mu = jnp.mean(r, axis=-1, keepdims=True)
        d = r - mu
        var = jnp.mean(d * d, axis=-1, keepdims=True)
        y_ref[rows, :] = d * lax.rsqrt(var + LN_EPS) * g_ref[...] + b_ref[...]


def _out_ln(o, x2, w_out16, ln_g, ln_b, *, layer):
    tokens = x2.shape[0]
    tm = min(OUT_TILE, tokens)
    sub = min(ROW_TILE, tm)
    row = lambda i: (i, 0)
    const = lambda i: (0, 0)
    nw = D_MODEL // COL_BLOCK
    return pl.pallas_call(
        functools.partial(_out_ln_kernel, tm=tm, sub=sub),
        grid=(tokens // tm,),
        in_specs=[pl.BlockSpec((tm, o.shape[1]), row), pl.BlockSpec((tm, D_MODEL), row)]
        + _weight_specs(layer, o.shape[1], nw, 1)
        + [pl.BlockSpec((1, D_MODEL), const), pl.BlockSpec((1, D_MODEL), const)],
        out_specs=pl.BlockSpec((tm, D_MODEL), row),
        out_shape=jax.ShapeDtypeStruct((tokens, D_MODEL), F32),
        compiler_params=_params(("arbitrary",)),
        name="out_ln",
    )(o, x2, *([w_out16] * nw), ln_g, ln_b)


def _rope_slab(x, cos_t, sin_up, sin_dn):
    half = ROPE_DIMS // 2
    return x * cos_t + pltpu.roll(x, LANES - half, axis=1) * sin_up + pltpu.roll(x, half, axis=1) * sin_dn


def _row_tiling(tokens, seq):
    tm = min(WIDE_TILE, tokens)
    if seq >= tm:
        per_seq = seq // tm
        return tm, 1, (lambda i: (i % per_seq, 0))
    return tm, tm // seq, (lambda i: (0, 0))


def _tile_tables(tables, reps):
    return tables if reps == 1 else tuple(jnp.tile(t, (reps, 1)) for t in tables)


def _kv_kernel(h_ref, w_ref, cos_ref, sup_ref, sdn_ref, k_ref, v_ref, klast_ref, vlast_ref, *, rows_per_seq, ns, keep):
    kv = jnp.dot(h_ref[...].astype(BF16), w_ref[...], preferred_element_type=F32)
    cos_t, sin_up, sin_dn = cos_ref[...], sup_ref[...], sdn_ref[...]
    k = jnp.concatenate([_rope_slab(kv[:, s * LANES:(s + 1) * LANES], cos_t, sin_up, sin_dn)
                         for s in range(B_KVW // LANES)], axis=1)
    v = kv[:, B_KVW:]
    k_ref[...] = k.astype(k_ref.dtype)
    v_ref[...] = v.astype(v_ref.dtype)
    for s in range(ns):
        end = (s + 1) * rows_per_seq
        klast_ref[s] = k[end - keep:end, :]
        vlast_ref[s] = v[end - keep:end, :]


def _shared_kv(h2, w_kv16, tables, *, batch, seq):
    tokens = batch * seq
    tm, ns, tab = _row_tiling(tokens, seq)
    keep = min(WINDOW, seq)
    rows_per_seq = tm // ns
    assert keep <= rows_per_seq
    per_seq = max(1, seq // tm)
    row = lambda i: (i, 0)
    lastb = lambda i: (i // per_seq, 0, 0)
    out = jax.ShapeDtypeStruct((tokens, B_KVW), BF16)
    last = jax.ShapeDtypeStruct((batch, keep, B_KVW), F32)
    return pl.pallas_call(
        functools.partial(_kv_kernel, rows_per_seq=rows_per_seq, ns=ns, keep=keep),
        grid=(tokens // tm,),
        in_specs=[
            pl.BlockSpec((tm, D_MODEL), row),
            pl.BlockSpec((D_MODEL, 2 * B_KVW), lambda i: (0, 0)),
            pl.BlockSpec((tm, LANES), tab),
            pl.BlockSpec((tm, LANES), tab),
            pl.BlockSpec((tm, LANES), tab),
        ],
        out_specs=[pl.BlockSpec((tm, B_KVW), row), pl.BlockSpec((tm, B_KVW), row),
                   pl.BlockSpec((ns, keep, B_KVW), lastb), pl.BlockSpec((ns, keep, B_KVW), lastb)],
        out_shape=[out, out, last, last],
        compiler_params=_params(("arbitrary",)),
        name="shared_kv",
    )(h2, w_kv16, *_tile_tables(tables, ns))


def _b_in_kernel(*refs, tm):
    nw = 2 * B_QW // COL_BLOCK
    x_ref, w_refs = refs[0], refs[1:1 + nw]
    cos_ref, sup_ref, sdn_ref, qx_ref, z_ref = refs[1 + nw:]
    xb = x_ref[...].astype(BF16)
    cos_t, sin_up, sin_dn = cos_ref[...], sup_ref[...], sdn_ref[...]
    lane_half = lax.broadcasted_iota(jnp.int32, (tm, LANES), 1) >> int(math.log2(B_HD))
    cb = COL_BLOCK
    nq = B_QW // cb
    for j in range(nq):
        proj = jnp.dot(xb, w_refs[j][...], preferred_element_type=F32)
        for sl in range(cb // LANES):
            slab = j * (cb // LANES) + sl
            rot = _rope_slab(proj[:, sl * LANES:(sl + 1) * LANES], cos_t, sin_up, sin_dn) * (B_HD ** -0.5 * LOG2E)
            for p in range(2):
                hq = 2 * slab + p
                x = jnp.where(lane_half == p, rot, 0.0)
                if p != (hq // B_GROUP) % 2:
                    x = pltpu.roll(x, B_HD, axis=1)
                qx_ref[:, hq * LANES:(hq + 1) * LANES] = x.astype(qx_ref.dtype)
    for j in range(nq):
        z_ref[:, j * cb:(j + 1) * cb] = jnp.dot(
            xb, w_refs[nq + j][...], preferred_element_type=F32).astype(z_ref.dtype)


def _b_in(x2, w_in16, tables, *, layer, batch, seq):
    tokens = batch * seq
    tm, ns, tab = _row_tiling(tokens, seq)
    row = lambda i: (i, 0)
    nw = 2 * B_QW // COL_BLOCK
    return pl.pallas_call(
        functools.partial(_b_in_kernel, tm=tm),
        grid=(tokens // tm,),
        in_specs=[pl.BlockSpec((tm, D_MODEL), row)] + _weight_specs(layer, D_MODEL, nw, 1) + [
            pl.BlockSpec((tm, LANES), tab),
            pl.BlockSpec((tm, LANES), tab),
            pl.BlockSpec((tm, LANES), tab),
        ],
        out_specs=[pl.BlockSpec((tm, B_QX), row), pl.BlockSpec((tm, B_QW), row)],
        out_shape=[jax.ShapeDtypeStruct((tokens, B_QX), BF16), jax.ShapeDtypeStruct((tokens, B_QW), BF16)],
        compiler_params=_params(("arbitrary",)),
        name="b_in",
    )(x2, *([w_in16] * nw), *_tile_tables(tables, ns))


def _attn_body(sink_ref, qx_ref, z_ref, k_ref, v_ref, o_ref, *, lq, lk, nc, masked):
    step = pl.program_id(1)
    lane_half = lax.broadcasted_iota(jnp.int32, (lq, LANES), 1) >> int(math.log2(B_HD))
    kv_slabs = range(B_KVW // LANES)
    kwin, vwin, valid = [], [], []
    for ci in range(nc):
        if masked is None:
            start = 0
            valid.append(None)
        else:
            c = step * nc + ci
            first = jnp.maximum(c - WINDOW_CHUNKS, 0)
            start = pl.multiple_of(first * CHUNK, CHUNK)
            key_chunk = first + (lax.broadcasted_iota(jnp.int32, (lq, lk), 1) >> int(math.log2(CHUNK)))
            valid.append((key_chunk <= c) if masked else None)
        kwin.append([k_ref[0, pl.ds(start, lk), s * LANES:(s + 1) * LANES] for s in kv_slabs])
        vwin.append([v_ref[0, pl.ds(start, lk), s * LANES:(s + 1) * LANES] for s in kv_slabs])
    units = [(ci, j) for ci in range(nc) for j in range(B_KV_HEADS)]
    qstack = [jnp.concatenate([qx_ref[ci * lq:(ci + 1) * lq, (B_GROUP * j + g) * LANES:(B_GROUP * j + g + 1) * LANES]
                               for g in range(B_GROUP)], axis=0) for ci, j in units]
    scores = [_mm_nt(qstack[u], kwin[ci][j // 2]) for u, (ci, j) in enumerate(units)]
    heads = [(u, ci, j, g) for u, (ci, j) in enumerate(units) for g in range(B_GROUP)]
    sk = [sink_ref[B_GROUP * j + g] * LOG2E for _, _, j, g in heads]
    sc = [scores[u][g * lq:(g + 1) * lq] for u, _, _, g in heads]
    sc = [s if valid[ci] is None else jnp.where(valid[ci], s, -jnp.inf) for s, (_, ci, _, _) in zip(sc, heads)]
    mx = [jnp.maximum(jnp.max(s, axis=-1, keepdims=True), k) for s, k in zip(sc, sk)]
    pstack = [jnp.concatenate([jnp.exp2(sc[h] - mx[h]).astype(BF16) for h in range(u * B_GROUP, (u + 1) * B_GROUP)],
                              axis=0) for u in range(len(units))]
    ones = jnp.ones((lk, LANES), BF16)
    pv = [_mm(pstack[u], jnp.concatenate([vwin[ci][j // 2], ones], axis=1))
          for u, (ci, j) in enumerate(units)]
    out = []
    for h, (u, _, _, g) in enumerate(heads):
        part = pv[u][g * lq:(g + 1) * lq]
        out.append(part[:, :LANES] / (part[:, LANES:] + jnp.exp2(sk[h] - mx[h])))
    for ci in range(nc):
        rows = slice(ci * lq, (ci + 1) * lq)
        for slab in range(B_QW // LANES):
            pair, g = divmod(slab, B_GROUP)
            lo = out[(ci * B_KV_HEADS + 2 * pair) * B_GROUP + g]
            hi = out[(ci * B_KV_HEADS + 2 * pair + 1) * B_GROUP + g]
            both = jnp.where(lane_half == 0, lo, hi)
            zs = z_ref[rows, slab * LANES:(slab + 1) * LANES].astype(F32)
            o_ref[rows, slab * LANES:(slab + 1) * LANES] = (both * _silu(zs)).astype(o_ref.dtype)


def _attn_kernel(sink_ref, qx_ref, z_ref, k_ref, v_ref, o_ref, *, lq, lk, nc, banded):
    body = functools.partial(_attn_body, sink_ref, qx_ref, z_ref, k_ref, v_ref, o_ref, lq=lq, lk=lk, nc=nc)
    if not banded:
        body(masked=None)
        return
    step = pl.program_id(1)
    clamped_steps = -(-WINDOW_CHUNKS // nc)

    @pl.when(step < clamped_steps)
    def _():
        body(masked=True)

    @pl.when(step >= clamped_steps)
    def _():
        body(masked=False)


def _attention(qx, z, k3, v3, sinks, *, batch, seq, banded):
    lq = min(CHUNK, seq)
    nc = min(ATTN_CHUNKS, seq // lq)
    nq = seq // (lq * nc)
    ltot = k3.shape[1]
    lk = (WINDOW_CHUNKS + 1) * CHUNK if banded else ltot
    tokens = batch * seq
    row = lambda b, c: (b * nq + c, 0)
    whole = lambda b, c: (b, 0, 0)
    return pl.pallas_call(
        functools.partial(_attn_kernel, lq=lq, lk=lk, nc=nc, banded=banded),
        grid=(batch, nq),
        in_specs=[
            pl.BlockSpec(memory_space=pltpu.SMEM),
            pl.BlockSpec((nc * lq, B_QX), row),
            pl.BlockSpec((nc * lq, B_QW), row),
            pl.BlockSpec((1, ltot, B_KVW), whole),
            pl.BlockSpec((1, ltot, B_KVW), whole),
        ],
        out_specs=pl.BlockSpec((nc * lq, B_QW), row),
        out_shape=jax.ShapeDtypeStruct((tokens, B_QW), BF16),
        compiler_params=_params(("arbitrary", "arbitrary")),
        name="swa_attention",
    )(sinks, qx, z, k3, v3)


def _rope_tables(pos):
    half = ROPE_DIMS // 2
    inv = ROPE_THETA ** (-jnp.arange(half, dtype=F32) * 2.0 / ROPE_DIMS)
    ang = pos.astype(F32)[:, None] * inv[None, :]
    cos, sin = jnp.cos(ang), jnp.sin(ang)
    ones = jnp.ones((pos.shape[0], B_HD - ROPE_DIMS), F32)
    zeros_h = jnp.zeros((pos.shape[0], half), F32)
    zeros_r = jnp.zeros((pos.shape[0], B_HD - ROPE_DIMS), F32)
    cos_head = jnp.concatenate([cos, cos, ones], axis=1)
    up_head = jnp.concatenate([-sin, zeros_h, zeros_r], axis=1)
    dn_head = jnp.concatenate([zeros_h, sin, zeros_r], axis=1)
    rep = LANES // B_HD
    return tuple(jnp.tile(t, (1, rep)) for t in (cos_head, up_head, dn_head))


def _trunk(x, pos, conv_state, delta_state, past_k, past_v, wts):
    batch, seq, _ = x.shape
    tokens = batch * seq
    h = x.reshape(tokens, D_MODEL)
    tables = _rope_tables(pos)
    new_conv, new_delta = [], []
    for i in range(N_A_LAYERS):
        q, k, v, z, gb, cbuf = _a_in(h, wts["a_w_in"], wts["a_w_gate"][i], wts["a_conv_w"][i], conv_state[i],
                                     wts["a_log"][i], wts["a_dt"][i], layer=i, batch=batch, seq=seq)
        o, s_new = _delta(q, k, v, z, gb, delta_state, wts["a_norm_w"][i], layer=i, batch=batch, seq=seq)
        h = _out_ln(o, h, wts["a_w_out"], wts["a_ln_g"][i], wts["a_ln_b"][i], layer=i)
        new_conv.append(cbuf)
        new_delta.append(s_new)
    k2, v2, k_last, v_last = _shared_kv(h, wts["b_w_kv"], tables, batch=batch, seq=seq)
    k3 = k2.reshape(batch, seq, B_KVW)
    v3 = v2.reshape(batch, seq, B_KVW)
    cached = past_k is not None
    if cached:
        pk = past_k.reshape(batch, -1, B_KVW)
        pv = past_v.reshape(batch, -1, B_KVW)
        k3 = jnp.concatenate([pk.astype(BF16), k3], axis=1)
        v3 = jnp.concatenate([pv.astype(BF16), v3], axis=1)
        k_last = jnp.concatenate([pk, k_last], axis=1)[:, -WINDOW:]
        v_last = jnp.concatenate([pv, v_last], axis=1)[:, -WINDOW:]
    new_k = k_last.reshape(batch, WINDOW, B_KV_HEADS, B_HD)
    new_v = v_last.reshape(batch, WINDOW, B_KV_HEADS, B_HD)
    for j in range(N_B_LAYERS):
        qx, z = _b_in(h, wts["b_w_in"], tables, layer=j, batch=batch, seq=seq)
        o = _attention(qx, z, k3, v3, wts["b_sinks"][j], batch=batch, seq=seq, banded=not cached)
        h = _out_ln(o, h, wts["b_w_out"], wts["b_ln_g"][j], wts["b_ln_b"][j], layer=j)
    return h.reshape(batch, seq, D_MODEL), jnp.stack(new_conv), jnp.stack(new_delta), new_k, new_v


def _paired_heads(w, axis):
    order = [B_GROUP * (2 * pair + odd) + g
             for pair in range(B_KV_HEADS // 2) for g in range(B_GROUP) for odd in range(2)]
    shape = w.shape
    blocks = w.reshape(shape[:axis] + (B_Q_HEADS, B_HD) + shape[axis + 1:])
    return jnp.take(blocks, jnp.array(order, jnp.int32), axis=axis).reshape(shape)


def kernel(x_prompt, x_sample, state_delta, state_conv, cache_k, cache_v, a_w_in, a_conv_w, a_log, a_dt_bias,
           a_norm_w, a_w_out, a_ln_g, a_ln_b, b_w_kv, b_w_in, b_sinks, b_w_out, b_ln_g, b_ln_b):
    a_w_in16 = a_w_in.astype(BF16)
    zeros_h = jnp.zeros((N_A_LAYERS, A_HEADS), F32)
    wts = {
        "a_w_in": a_w_in16,
        "a_w_gate": jnp.pad(a_w_in16[:, :, A_MAIN:], ((0, 0), (0, 0), (0, LANES - 2 * A_HEADS))),
        "a_conv_w": a_conv_w,
        "a_log": jnp.concatenate([a_log.astype(F32), zeros_h], axis=1)[:, :, None],
        "a_dt": jnp.concatenate([a_dt_bias.astype(F32), zeros_h], axis=1)[:, :, None],
        "a_norm_w": a_norm_w.reshape(N_A_LAYERS, 1, A_DV),
        "a_w_out": a_w_out.astype(BF16),
        "a_ln_g": a_ln_g.reshape(N_A_LAYERS, 1, D_MODEL),
        "a_ln_b": a_ln_b.reshape(N_A_LAYERS, 1, D_MODEL),
        "b_w_kv": b_w_kv.astype(BF16),
        "b_w_in": jnp.concatenate([b_w_in[:, :, :B_QW], _paired_heads(b_w_in[:, :, B_QW:], axis=2)],
                                  axis=2).astype(BF16),
        "b_sinks": b_sinks,
        "b_w_out": _paired_heads(b_w_out, axis=1).astype(BF16),
        "b_ln_g": b_ln_g.reshape(N_B_LAYERS, 1, D_MODEL),
        "b_ln_b": b_ln_b.reshape(N_B_LAYERS, 1, D_MODEL),
    }
    bp, lp, _ = x_prompt.shape
    bs, ls, _ = x_sample.shape
    pos_prompt = jnp.arange(lp, dtype=jnp.int32)
    pos_sample = PAST_LEN + jnp.arange(ls, dtype=jnp.int32)
    zero_conv = jnp.zeros((N_A_LAYERS, bp, CONV_W - 1, CONV_DIM), F32)
    zero_delta = jnp.zeros((N_A_LAYERS, bp, A_HEADS, A_DK, A_DV), F32)
    y_p, p_conv, p_delta, p_k, p_v = _trunk(x_prompt, pos_prompt, zero_conv, zero_delta, None, None, wts)
    y_s, s_conv, s_delta, s_k, s_v = _trunk(x_sample, pos_sample, state_conv, state_delta, cache_k, cache_v, wts)
    return (y_p, y_s, p_delta, p_conv, p_k, p_v, s_delta, s_conv, s_k, s_v)
```

```python
import functools
import math

import jax
import jax.numpy as jnp
from jax import lax
from jax.experimental import pallas as pl
from jax.experimental.pallas import tpu as pltpu

D_MODEL = 1024
DEPTH = 4
PAST_LEN = 4096
CHUNK = 64
N_A_LAYERS = DEPTH // 2
N_B_LAYERS = DEPTH - N_A_LAYERS
A_HEADS = 8
A_DK = 128
A_DV = 128
A_QK = A_HEADS * A_DK
A_VW = A_HEADS * A_DV
CONV_W = 4
CONV_DIM = 2 * A_QK + A_VW
A_MAIN = CONV_DIM + A_VW
B_Q_HEADS = 16
B_KV_HEADS = 4
B_GROUP = B_Q_HEADS // B_KV_HEADS
B_HD = 64
B_QW = B_Q_HEADS * B_HD
B_KVW = B_KV_HEADS * B_HD
WINDOW = 128
WINDOW_CHUNKS = WINDOW // CHUNK
ROPE_DIMS = B_HD // 4
ROPE_THETA = 500000.0
DN_ALPHA = (2 * DEPTH) ** 0.25
LN_EPS = 1e-5
RMS_EPS = 1e-6

LANES = 128
SUBLANES = 8
VMEM_LIMIT = 48 * 1024 * 1024
ROW_TILE = 256
WIDE_TILE = 512
OUT_TILE = 1024
CONV_PIECE = 512
COL_BLOCK = 512
DELTA_BATCHES = 2
ATTN_CHUNKS = 4
B_QX = B_Q_HEADS * LANES
LOG2E = math.log2(math.e)

F32 = jnp.float32
BF16 = jnp.bfloat16


def _mm(a, b):
    return jnp.dot(a.astype(BF16), b.astype(BF16), preferred_element_type=F32)


def _mm_nt(a, b):
    return lax.dot_general(a.astype(BF16), b.astype(BF16), (((1,), (1,)), ((), ())),
                           preferred_element_type=F32)


def _silu(x):
    h = 0.5 * x
    return h + h * jnp.tanh(h)


def _params(semantics):
    return pltpu.CompilerParams(dimension_semantics=semantics, vmem_limit_bytes=VMEM_LIMIT)


def _weight_specs(layer, kdim, nblocks, grid_rank):
    def spec(j):
        if grid_rank == 1:
            return pl.BlockSpec((None, kdim, COL_BLOCK), lambda i: (layer, 0, j))
        return pl.BlockSpec((None, kdim, COL_BLOCK), lambda b, l: (layer, 0, j))
    return [spec(j) for j in range(nblocks)]


def _a_in_kernel(*refs, ts, ns):
    nw = A_MAIN // COL_BLOCK
    x_ref, w_refs = refs[0], refs[1:1 + nw]
    (wg_ref, cw_ref, c0_ref, alog_ref, dt_ref,
     q_ref, k_ref, v_ref, z_ref, gb_ref, cout_ref, pbuf, carry) = refs[1 + nw:]
    tm = ts * ns
    l = pl.program_id(1)
    tail = SUBLANES - (CONV_W - 1)

    @pl.when(l == 0)
    def _():
        for s in range(ns):
            carry[s, tail:SUBLANES, :] = c0_ref[s]

    xb = x_ref[...].astype(BF16)
    cb = COL_BLOCK
    nconv = CONV_DIM // cb
    outs = (q_ref, k_ref, v_ref)

    def conv_block(s, blk):
        rows = slice(s * ts, (s + 1) * ts)
        cols = slice(blk * LANES, (blk + 1) * LANES)
        ext = jnp.concatenate([carry[s, :, cols], pbuf[rows, cols]], axis=0)
        y = None
        for j in range(CONV_W):
            back = CONV_W - 1 - j
            tap = (pltpu.roll(ext, back, axis=0) if back else ext)[SUBLANES:] * cw_ref[j:j + 1, cols]
            y = tap if y is None else y + tap
        y = _silu(y)
        which, head = divmod(blk, A_HEADS)
        if which < 2:
            y = y * lax.rsqrt(jnp.sum(y * y, axis=-1, keepdims=True) + RMS_EPS)
            if which == 0:
                y = y * (A_DK ** -0.5)
        outs[which][rows, head * LANES:(head + 1) * LANES] = y.astype(outs[which].dtype)

    pc = CONV_PIECE
    npieces = CONV_DIM // pc

    def project(i):
        j, off = divmod(i * pc, cb)
        pbuf[:, i * pc:(i + 1) * pc] = jnp.dot(xb, w_refs[j][:, off:off + pc], preferred_element_type=F32)

    def gate_z(j):
        z_ref[:, j * cb:(j + 1) * cb] = jnp.dot(
            xb, w_refs[nconv + j][...], preferred_element_type=F32).astype(z_ref.dtype)

    def decay_beta():
        gates = jnp.dot(xb, wg_ref[...], preferred_element_type=F32)
        gt = gates.T[:2 * A_HEADS]
        rowi = lax.broadcasted_iota(jnp.int32, gt.shape, 0)
        sp_in = gt + dt_ref[...]
        softplus = jnp.maximum(sp_in, 0.0) + jnp.log1p(jnp.exp(-jnp.abs(sp_in)))
        gval = -jnp.exp(alog_ref[...]) * softplus
        bval = 1.0 / (1.0 + jnp.exp(-gt))
        res = jnp.where(rowi < A_HEADS, gval, bval)
        gb_ref[...] = jnp.concatenate([res, jnp.zeros((LANES - 2 * A_HEADS, tm), F32)], axis=0).T

    light = [functools.partial(gate_z, j) for j in range(A_VW // cb)] + [decay_beta]
    project(0)
    for i in range(npieces):
        if i + 1 < npieces:
            project(i + 1)
        if light and i % (npieces // 4) == 0:
            light.pop(0)()
        for s in range(ns):
            for blk in range(i * pc // LANES, (i + 1) * pc // LANES):
                conv_block(s, blk)
    assert not light

    for s in range(ns):
        last = pbuf[(s + 1) * ts - (CONV_W - 1):(s + 1) * ts, :]
        cout_ref[s] = last
        carry[s, tail:SUBLANES, :] = last


def _a_in(x2, w_in16, w_gate, conv_w, conv0, alog_col, dt_col, *, layer, batch, seq):
    if seq >= WIDE_TILE:
        ts, ns = WIDE_TILE, 1
    else:
        ts, ns = seq, min(batch, WIDE_TILE // seq)
    tm = ts * ns
    nl = seq // ts
    tokens = batch * seq
    row = lambda b, l: (b * nl + l, 0)
    const = lambda b, l: (0, 0)
    perseq = lambda b, l: (b, 0, 0)
    wide = jax.ShapeDtypeStruct((tokens, A_QK), BF16)
    nw = A_MAIN // COL_BLOCK
    return pl.pallas_call(
        functools.partial(_a_in_kernel, ts=ts, ns=ns),
        grid=(batch // ns, nl),
        in_specs=[pl.BlockSpec((tm, D_MODEL), row)] + _weight_specs(layer, D_MODEL, nw, 2) + [
            pl.BlockSpec((D_MODEL, LANES), const),
            pl.BlockSpec((CONV_W, CONV_DIM), const),
            pl.BlockSpec((ns, CONV_W - 1, CONV_DIM), perseq),
            pl.BlockSpec((2 * A_HEADS, 1), const),
            pl.BlockSpec((2 * A_HEADS, 1), const),
        ],
        out_specs=[
            pl.BlockSpec((tm, A_QK), row),
            pl.BlockSpec((tm, A_QK), row),
            pl.BlockSpec((tm, A_VW), row),
            pl.BlockSpec((tm, A_VW), row),
            pl.BlockSpec((tm, LANES), row),
            pl.BlockSpec((ns, CONV_W - 1, CONV_DIM), perseq),
        ],
        out_shape=[wide, wide, wide, wide,
                   jax.ShapeDtypeStruct((tokens, LANES), F32),
                   jax.ShapeDtypeStruct((batch, CONV_W - 1, CONV_DIM), F32)],
        scratch_shapes=[pltpu.VMEM((tm, CONV_DIM), F32), pltpu.VMEM((ns, SUBLANES, CONV_DIM), F32)],
        compiler_params=_params(("arbitrary", "arbitrary")),
        name="a_in",
    )(x2, *([w_in16] * nw), w_gate, conv_w, conv0, alog_col, dt_col)


def _delta_group(bi, q_ref, k_ref, v_ref, z_ref, gb_ref, nw, o_ref, s_ref, c):
    pk = LANES // c
    lc = int(math.log2(c))
    packs = [list(range(p * pk, (p + 1) * pk)) for p in range(A_HEADS // pk)]
    npk = range(len(packs))
    row = lax.broadcasted_iota(jnp.int32, (c, LANES), 0)
    lane = lax.broadcasted_iota(jnp.int32, (c, LANES), 1)
    colr = lane & (c - 1)
    member = lane >> lc
    eye = (row == colr).astype(F32)
    incl = row >= colr
    strict = row > colr
    diag8 = (row >> 3) == (colr >> 3)
    sq0 = lax.broadcasted_iota(jnp.int32, (LANES, LANES), 0)
    sq1 = lax.broadcasted_iota(jnp.int32, (LANES, LANES), 1)
    bd_mask = (sq0 >> lc) == (sq1 >> lc)
    kr0 = lax.broadcasted_iota(jnp.int32, (LANES, pk * LANES), 0)
    kr1 = lax.broadcasted_iota(jnp.int32, (LANES, pk * LANES), 1)
    k_mask = (kr0 >> lc) == (kr1 >> int(math.log2(LANES)))

    def bd(m):
        return jnp.where(bd_mask, jnp.concatenate([m] * pk, axis=0), 0.0)

    def by_member(vals):
        out = vals[0]
        for r in range(1, pk):
            out = jnp.where(member >= r, vals[r], out)
        return out

    def wide(col, hs):
        return jnp.concatenate([jnp.broadcast_to(col[h], (c, LANES)) for h in hs], axis=1)

    def hcol(r):
        return slice(r * LANES, (r + 1) * LANES)

    g = gb_ref[bi]
    gcum = g
    step = 1
    while step < c:
        gcum = gcum + jnp.where(row >= step, pltpu.roll(gcum, step, axis=0), 0.0)
        step *= 2
    gcum_t = jnp.concatenate([gcum] * pk, axis=0).T
    gc = [gcum[:, h:h + 1] for h in range(A_HEADS)]
    beta = [g[:, A_HEADS + h:A_HEADS + h + 1] for h in range(A_HEADS)]
    glast = [gcum[c - 1:c, h:h + 1] for h in range(A_HEADS)]
    eg = [jnp.exp(x) for x in gc]
    pcols = [slice(hs[0] * LANES, (hs[-1] + 1) * LANES) for hs in packs]
    kpf = [k_ref[bi, :, pc].astype(F32) for pc in pcols]
    qp16 = [q_ref[bi, :, pc] for pc in pcols]
    vpf = [v_ref[bi, :, pc].astype(F32) for pc in pcols]
    kbeta = [kpf[p] * wide(beta, hs) for p, hs in enumerate(packs)]
    decay = [jnp.exp(jnp.where(incl, by_member([jnp.broadcast_to(gc[h], (c, LANES)) for h in hs])
                               - by_member([gcum_t[h:h + 1, :] for h in hs]), -jnp.inf))
             for hs in packs]
    yield
    k_bd = [jnp.where(k_mask, jnp.concatenate([x] * pk, axis=0), 0.0).astype(BF16) for x in kpf]
    kq = [_mm_nt(jnp.concatenate([kbeta[p].astype(BF16), qp16[p]], axis=0), k_bd[p])
          for p in npk]
    yield
    a = [jnp.where(strict, kq[p][:c] * decay[p], 0.0) for p in npk]
    d = [jnp.where(diag8, x, 0.0) for x in a]
    heads = [(p, r, hs[r]) for p, hs in enumerate(packs) for r in range(pk)]
    s = [s_ref[bi, h] for _, _, h in heads]
    ks = [_mm(jnp.concatenate([kbeta[p][:, hcol(r)] * eg[h], qp16[p][:, hcol(r)].astype(F32) * eg[h]], axis=0), s[i])
          for i, (p, r, h) in enumerate(heads)]
    yield
    d2 = [_mm(x, bd(x)) for x in d]
    yield
    d4 = [_mm(x, bd(x)) for x in d2]
    p1 = [_mm(eye - x, bd(eye + y)) for x, y in zip(d, d2)]
    yield
    xs = [_mm(p, bd(eye + y)) for p, y in zip(p1, d4)]
    yield
    shift = 3
    while (1 << shift) < c:
        mask = ((row >> (shift + 1)) == (colr >> (shift + 1))) & ((row >> shift) > (colr >> shift))
        xc = [_mm(x, bd(jnp.where(mask, m, 0.0))) for x, m in zip(xs, a)]
        yield
        xs = [x - _mm(y, bd(x)) for x, y in zip(xs, xc)]
        yield
        shift += 1
    vn = []
    for p, hs in enumerate(packs):
        vb = vpf[p] * wide(beta, hs)
        stacked = jnp.concatenate([vb[:, hcol(r)] - ks[p * pk + r][:c] for r in range(pk)], axis=0)
        vn.append(_mm(bd(xs[p]), stacked))
    yield
    v_new = [vn[p][r * c:(r + 1) * c] for p, r, _ in heads]
    intra = [kq[p][c:] * decay[p] for p in npk]
    kd_t = [(kpf[p][:, hcol(r)] * jnp.exp(glast[h] - gc[h])).T for p, r, h in heads]
    ov = [_mm(jnp.concatenate([intra[p][:, r * c:(r + 1) * c], kd_t[i]], axis=0), v_new[i])
          for i, (p, r, h) in enumerate(heads)]
    yield
    for i, (p, r, h) in enumerate(heads):
        s_ref[bi, h] = s[i] * jnp.exp(glast[h]) + ov[i][c:]
        o = ks[i][c:] + ov[i][:c]
        zf = z_ref[bi, :, h * LANES:(h + 1) * LANES].astype(F32)
        gated = o * lax.rsqrt(jnp.mean(o * o, axis=-1, keepdims=True) + RMS_EPS) * nw * _silu(zf)
        o_ref[bi, :, h * LANES:(h + 1) * LANES] = gated.astype(o_ref.dtype)


def _run_interleaved(gens):
    live = list(gens)
    while live:
        for gen in list(live):
            try:
                next(gen)
            except StopIteration:
                live.remove(gen)


def _delta_kernel(q_ref, k_ref, v_ref, z_ref, gb_ref, s0_ref, nw_ref, o_ref, s_ref, *, c, nb):
    n = pl.program_id(1)

    @pl.when(n == 0)
    def _():
        s_ref[...] = s0_ref[...]

    nw = nw_ref[...]
    _run_interleaved([_delta_group(bi, q_ref, k_ref, v_ref, z_ref, gb_ref, nw, o_ref, s_ref, c) for bi in range(nb)])


def _delta(q, k, v, z, gb, s0_all, norm_w, *, layer, batch, seq):
    c = min(CHUNK, seq)
    n = seq // c
    nb = DELTA_BATCHES
    seq3 = lambda a: a.reshape(batch, seq, a.shape[-1])
    blk = lambda b, i: (b, i, 0)
    state = lambda b, i: (b, 0, 0, 0)
    o, s_new = pl.pallas_call(
        functools.partial(_delta_kernel, c=c, nb=nb),
        grid=(batch // nb, n),
        in_specs=[
            pl.BlockSpec((nb, c, A_QK), blk),
            pl.BlockSpec((nb, c, A_QK), blk),
            pl.BlockSpec((nb, c, A_VW), blk),
            pl.BlockSpec((nb, c, A_VW), blk),
            pl.BlockSpec((nb, c, LANES), blk),
            pl.BlockSpec((None, nb, A_HEADS, A_DK, A_DV), lambda b, i: (layer, b, 0, 0, 0)),
            pl.BlockSpec((1, LANES), lambda b, i: (0, 0)),
        ],
        out_specs=[
            pl.BlockSpec((nb, c, A_VW), blk),
            pl.BlockSpec((nb, A_HEADS, A_DK, A_DV), state),
        ],
        out_shape=[jax.ShapeDtypeStruct((batch, seq, A_VW), BF16),
                   jax.ShapeDtypeStruct((batch, A_HEADS, A_DK, A_DV), F32)],
        compiler_params=_params(("arbitrary", "arbitrary")),
        name="delta_rule",
    )(seq3(q), seq3(k), seq3(v), seq3(z), seq3(gb), s0_all, norm_w)
    return o.reshape(batch * seq, A_VW), s_new


def _out_ln_kernel(*refs, tm, sub):
    nw = D_MODEL // COL_BLOCK
    o_ref, x_ref, w_refs = refs[0], refs[1], refs[2:2 + nw]
    g_ref, b_ref, y_ref = refs[2 + nw:]
    for r0 in range(0, tm, sub):
        rows = slice(r0, r0 + sub)
        o = o_ref[rows, :]
        proj = jnp.concatenate([jnp.dot(o, w[...], preferred_element_type=F32) for w in w_refs], axis=1)
        r = DN_ALPHA * x_ref[rows, :] + proj
        mu = jnp.mean(r, axis=-1, keepdims=True)
        d = r - mu
        var = jnp.mean(d * d, axis=-1, keepdims=True)
        y_ref[rows, :] = d * lax.rsqrt(var + LN_EPS) * g_ref[...] + b_ref[...]


def _out_ln(o, x2, w_out16, ln_g, ln_b, *, layer):
    tokens = x2.shape[0]
    tm = min(OUT_TILE, tokens)
    sub = min(ROW_TILE, tm)
    row = lambda i: (i, 0)
    const = lambda i: (0, 0)
    nw = D_MODEL // COL_BLOCK
    return pl.pallas_call(
        functools.partial(_out_ln_kernel, tm=tm, sub=sub),
        grid=(tokens // tm,),
        in_specs=[pl.BlockSpec((tm, o.shape[1]), row), pl.BlockSpec((tm, D_MODEL), row)]
        + _weight_specs(layer, o.shape[1], nw, 1)
        + [pl.BlockSpec((1, D_MODEL), const), pl.BlockSpec((1, D_MODEL), const)],
        out_specs=pl.BlockSpec((tm, D_MODEL), row),
        out_shape=jax.ShapeDtypeStruct((tokens, D_MODEL), F32),
        compiler_params=_params(("arbitrary",)),
        name="out_ln",
    )(o, x2, *([w_out16] * nw), ln_g, ln_b)


def _rope_slab(x, cos_t, sin_up, sin_dn):
    half = ROPE_DIMS // 2
    return x * cos_t + pltpu.roll(x, LANES - half, axis=1) * sin_up + pltpu.roll(x, half, axis=1) * sin_dn


def _row_tiling(tokens, seq):
    tm = min(WIDE_TILE, tokens)
    if seq >= tm:
        per_seq = seq // tm
        return tm, 1, (lambda i: (i % per_seq, 0))
    return tm, tm // seq, (lambda i: (0, 0))


def _tile_tables(tables, reps):
    return tables if reps == 1 else tuple(jnp.tile(t, (reps, 1)) for t in tables)


def _kv_kernel(h_ref, w_ref, cos_ref, sup_ref, sdn_ref, k_ref, v_ref, klast_ref, vlast_ref, *, rows_per_seq, ns, keep):
    kv = jnp.dot(h_ref[...].astype(BF16), w_ref[...], preferred_element_type=F32)
    cos_t, sin_up, sin_dn = cos_ref[...], sup_ref[...], sdn_ref[...]
    k = jnp.concatenate([_rope_slab(kv[:, s * LANES:(s + 1) * LANES], cos_t, sin_up, sin_dn)
                         for s in range(B_KVW // LANES)], axis=1)
    v = kv[:, B_KVW:]
    k_ref[...] = k.astype(k_ref.dtype)
    v_ref[...] = v.astype(v_ref.dtype)
    for s in range(ns):
        end = (s + 1) * rows_per_seq
        klast_ref[s] = k[end - keep:end, :]
        vlast_ref[s] = v[end - keep:end, :]


def _shared_kv(h2, w_kv16, tables, *, batch, seq):
    tokens = batch * seq
    tm, ns, tab = _row_tiling(tokens, seq)
    keep = min(WINDOW, seq)
    rows_per_seq = tm // ns
    assert keep <= rows_per_seq
    per_seq = max(1, seq // tm)
    row = lambda i: (i, 0)
    lastb = lambda i: (i // per_seq, 0, 0)
    out = jax.ShapeDtypeStruct((tokens, B_KVW), BF16)
    last = jax.ShapeDtypeStruct((batch, keep, B_KVW), F32)
    return pl.pallas_call(
        functools.partial(_kv_kernel, rows_per_seq=rows_per_seq, ns=ns, keep=keep),
        grid=(tokens // tm,),
        in_specs=[
            pl.BlockSpec((tm, D_MODEL), row),
            pl.BlockSpec((D_MODEL, 2 * B_KVW), lambda i: (0, 0)),
            pl.BlockSpec((tm, LANES), tab),
            pl.BlockSpec((tm, LANES), tab),
            pl.BlockSpec((tm, LANES), tab),
        ],
        out_specs=[pl.BlockSpec((tm, B_KVW), row), pl.BlockSpec((tm, B_KVW), row),
                   pl.BlockSpec((ns, keep, B_KVW), lastb), pl.BlockSpec((ns, keep, B_KVW), lastb)],
        out_shape=[out, out, last, last],
        compiler_params=_params(("arbitrary",)),
        name="shared_kv",
    )(h2, w_kv16, *_tile_tables(tables, ns))


def _b_in_kernel(*refs, tm):
    nw = 2 * B_QW // COL_BLOCK
    x_ref, w_refs = refs[0], refs[1:1 + nw]
    cos_ref, sup_ref, sdn_ref, qx_ref, z_ref = refs[1 + nw:]
    xb = x_ref[...].astype(BF16)
    cos_t, sin_up, sin_dn = cos_ref[...], sup_ref[...], sdn_ref[...]
    lane_half = lax.broadcasted_iota(jnp.int32, (tm, LANES), 1) >> int(math.log2(B_HD))
    cb = COL_BLOCK
    nq = B_QW // cb
    for j in range(nq):
        proj = jnp.dot(xb, w_refs[j][...], preferred_element_type=F32)
        for sl in range(cb // LANES):
            slab = j * (cb // LANES) + sl
            rot = _rope_slab(proj[:, sl * LANES:(sl + 1) * LANES], cos_t, sin_up, sin_dn) * (B_HD ** -0.5 * LOG2E)
            for p in range(2):
                hq = 2 * slab + p
                x = jnp.where(lane_half == p, rot, 0.0)
                if p != (hq // B_GROUP) % 2:
                    x = pltpu.roll(x, B_HD, axis=1)
                qx_ref[:, hq * LANES:(hq + 1) * LANES] = x.astype(qx_ref.dtype)
    for j in range(nq):
        z_ref[:, j * cb:(j + 1) * cb] = jnp.dot(
            xb, w_refs[nq + j][...], preferred_element_type=F32).astype(z_ref.dtype)


def _b_in(x2, w_in16, tables, *, layer, batch, seq):
    tokens = batch * seq
    tm, ns, tab = _row_tiling(tokens, seq)
    row = lambda i: (i, 0)
    nw = 2 * B_QW // COL_BLOCK
    return pl.pallas_call(
        functools.partial(_b_in_kernel, tm=tm),
        grid=(tokens // tm,),
        in_specs=[pl.BlockSpec((tm, D_MODEL), row)] + _weight_specs(layer, D_MODEL, nw, 1) + [
            pl.BlockSpec((tm, LANES), tab),
            pl.BlockSpec((tm, LANES), tab),
            pl.BlockSpec((tm, LANES), tab),
        ],
        out_specs=[pl.BlockSpec((tm, B_QX), row), pl.BlockSpec((tm, B_QW), row)],
        out_shape=[jax.ShapeDtypeStruct((tokens, B_QX), BF16), jax.ShapeDtypeStruct((tokens, B_QW), BF16)],
        compiler_params=_params(("arbitrary",)),
        name="b_in",
    )(x2, *([w_in16] * nw), *_tile_tables(tables, ns))


def _attn_body(sink_ref, qx_ref, z_ref, k_ref, v_ref, o_ref, *, lq, lk, nc, masked):
    step = pl.program_id(1)
    lane_half = lax.broadcasted_iota(jnp.int32, (lq, LANES), 1) >> int(math.log2(B_HD))
    kv_slabs = range(B_KVW // LANES)
    kwin, vwin, valid = [], [], []
    for ci in range(nc):
        if masked is None:
            start = 0
            valid.append(None)
        else:
            c = step * nc + ci
            first = jnp.maximum(c - WINDOW_CHUNKS, 0)
            start = pl.multiple_of(first * CHUNK, CHUNK)
            key_chunk = first + (lax.broadcasted_iota(jnp.int32, (lq, lk), 1) >> int(math.log2(CHUNK)))
            valid.append((key_chunk <= c) if masked else None)
        kwin.append([k_ref[0, pl.ds(start, lk), s * LANES:(s + 1) * LANES] for s in kv_slabs])
        vwin.append([v_ref[0, pl.ds(start, lk), s * LANES:(s + 1) * LANES] for s in kv_slabs])
    units = [(ci, j) for ci in range(nc) for j in range(B_KV_HEADS)]
    qstack = [jnp.concatenate([qx_ref[ci * lq:(ci + 1) * lq, (B_GROUP * j + g) * LANES:(B_GROUP * j + g + 1) * LANES]
                               for g in range(B_GROUP)], axis=0) for ci, j in units]
    scores = [_mm_nt(qstack[u], kwin[ci][j // 2]) for u, (ci, j) in enumerate(units)]
    heads = [(u, ci, j, g) for u, (ci, j) in enumerate(units) for g in range(B_GROUP)]
    sk = [sink_ref[B_GROUP * j + g] * LOG2E for _, _, j, g in heads]
    sc = [scores[u][g * lq:(g + 1) * lq] for u, _, _, g in heads]
    sc = [s if valid[ci] is None else jnp.where(valid[ci], s, -jnp.inf) for s, (_, ci, _, _) in zip(sc, heads)]
    mx = [jnp.maximum(jnp.max(s, axis=-1, keepdims=True), k) for s, k in zip(sc, sk)]
    pstack = [jnp.concatenate([jnp.exp2(sc[h] - mx[h]).astype(BF16) for h in range(u * B_GROUP, (u + 1) * B_GROUP)],
                              axis=0) for u in range(len(units))]
    ones = jnp.ones((lk, LANES), BF16)
    pv = [_mm(pstack[u], jnp.concatenate([vwin[ci][j // 2], ones], axis=1))
          for u, (ci, j) in enumerate(units)]
    out = []
    for h, (u, _, _, g) in enumerate(heads):
        part = pv[u][g * lq:(g + 1) * lq]
        out.append(part[:, :LANES] / (part[:, LANES:] + jnp.exp2(sk[h] - mx[h])))
    for ci in range(nc):
        rows = slice(ci * lq, (ci + 1) * lq)
        for slab in range(B_QW // LANES):
            pair, g = divmod(slab, B_GROUP)
            lo = out[(ci * B_KV_HEADS + 2 * pair) * B_GROUP + g]
            hi = out[(ci * B_KV_HEADS + 2 * pair + 1) * B_GROUP + g]
            both = jnp.where(lane_half == 0, lo, hi)
            zs = z_ref[rows, slab * LANES:(slab + 1) * LANES].astype(F32)
            o_ref[rows, slab * LANES:(slab + 1) * LANES] = (both * _silu(zs)).astype(o_ref.dtype)


def _attn_kernel(sink_ref, qx_ref, z_ref, k_ref, v_ref, o_ref, *, lq, lk, nc, banded):
    body = functools.partial(_attn_body, sink_ref, qx_ref, z_ref, k_ref, v_ref, o_ref, lq=lq, lk=lk, nc=nc)
    if not banded:
        body(masked=None)
        return
    step = pl.program_id(1)
    clamped_steps = -(-WINDOW_CHUNKS // nc)

    @pl.when(step < clamped_steps)
    def _():
        body(masked=True)

    @pl.when(step >= clamped_steps)
    def _():
        body(masked=False)


def _attention(qx, z, k3, v3, sinks, *, batch, seq, banded):
    lq = min(CHUNK, seq)
    nc = min(ATTN_CHUNKS, seq // lq)
    nq = seq // (lq * nc)
    ltot = k3.shape[1]
    lk = (WINDOW_CHUNKS + 1) * CHUNK if banded else ltot
    tokens = batch * seq
    row = lambda b, c: (b * nq + c, 0)
    whole = lambda b, c: (b, 0, 0)
    return pl.pallas_call(
        functools.partial(_attn_kernel, lq=lq, lk=lk, nc=nc, banded=banded),
        grid=(batch, nq),
        in_specs=[
            pl.BlockSpec(memory_space=pltpu.SMEM),
            pl.BlockSpec((nc * lq, B_QX), row),
            pl.BlockSpec((nc * lq, B_QW), row),
            pl.BlockSpec((1, ltot, B_KVW), whole),
            pl.BlockSpec((1, ltot, B_KVW), whole),
        ],
        out_specs=pl.BlockSpec((nc * lq, B_QW), row),
        out_shape=jax.ShapeDtypeStruct((tokens, B_QW), BF16),
        compiler_params=_params(("arbitrary", "arbitrary")),
        name="swa_attention",
    )(sinks, qx, z, k3, v3)


def _rope_tables(pos):
    half = ROPE_DIMS // 2
    inv = ROPE_THETA ** (-jnp.arange(half, dtype=F32) * 2.0 / ROPE_DIMS)
    ang = pos.astype(F32)[:, None] * inv[None, :]
    cos, sin = jnp.cos(ang), jnp.sin(ang)
    ones = jnp.ones((pos.shape[0], B_HD - ROPE_DIMS), F32)
    zeros_h = jnp.zeros((pos.shape[0], half), F32)
    zeros_r = jnp.zeros((pos.shape[0], B_HD - ROPE_DIMS), F32)
    cos_head = jnp.concatenate([cos, cos, ones], axis=1)
    up_head = jnp.concatenate([-sin, zeros_h, zeros_r], axis=1)
    dn_head = jnp.concatenate([zeros_h, sin, zeros_r], axis=1)
    rep = LANES // B_HD
    return tuple(jnp.tile(t, (1, rep)) for t in (cos_head, up_head, dn_head))


def _trunk(x, pos, conv_state, delta_state, past_k, past_v, wts):
    batch, seq, _ = x.shape
    tokens = batch * seq
    h = x.reshape(tokens, D_MODEL)
    tables = _rope_tables(pos)
    new_conv, new_delta = [], []
    for i in range(N_A_LAYERS):
        q, k, v, z, gb, cbuf = _a_in(h, wts["a_w_in"], wts["a_w_gate"][i], wts["a_conv_w"][i], conv_state[i],
                                     wts["a_log"][i], wts["a_dt"][i], layer=i, batch=batch, seq=seq)
        o, s_new = _delta(q, k, v, z, gb, delta_state, wts["a_norm_w"][i], layer=i, batch=batch, seq=seq)
        h = _out_ln(o, h, wts["a_w_out"], wts["a_ln_g"][i], wts["a_ln_b"][i], layer=i)
        new_conv.append(cbuf)
        new_delta.append(s_new)
    k2, v2, k_last, v_last = _shared_kv(h, wts["b_w_kv"], tables, batch=batch, seq=seq)
    k3 = k2.reshape(batch, seq, B_KVW)
    v3 = v2.reshape(batch, seq, B_KVW)
    cached = past_k is not None
    if cached:
        pk = past_k.reshape(batch, -1, B_KVW)
        pv = past_v.reshape(batch, -1, B_KVW)
        k3 = jnp.concatenate([pk.astype(BF16), k3], axis=1)
        v3 = jnp.concatenate([pv.astype(BF16), v3], axis=1)
        k_last = jnp.concatenate([pk, k_last], axis=1)[:, -WINDOW:]
        v_last = jnp.concatenate([pv, v_last], axis=1)[:, -WINDOW:]
    new_k = k_last.reshape(batch, WINDOW, B_KV_HEADS, B_HD)
    new_v = v_last.reshape(batch, WINDOW, B_KV_HEADS, B_HD)
    for j in range(N_B_LAYERS):
        qx, z = _b_in(h, wts["b_w_in"], tables, layer=j, batch=batch, seq=seq)
        o = _attention(qx, z, k3, v3, wts["b_sinks"][j], batch=batch, seq=seq, banded=not cached)
        h = _out_ln(o, h, wts["b_w_out"], wts["b_ln_g"][j], wts["b_ln_b"][j], layer=j)
    return h.reshape(batch, seq, D_MODEL), jnp.stack(new_conv), jnp.stack(new_delta), new_k, new_v


def _paired_heads(w, axis):
    order = [B_GROUP * (2 * pair + odd) + g
             for pair in range(B_KV_HEADS // 2) for g in range(B_GROUP) for odd in range(2)]
    shape = w.shape
    blocks = w.reshape(shape[:axis] + (B_Q_HEADS, B_HD) + shape[axis + 1:])
    return jnp.take(blocks, jnp.array(order, jnp.int32), axis=axis).reshape(shape)


def kernel(x_prompt, x_sample, state_delta, state_conv, cache_k, cache_v, a_w_in, a_conv_w, a_log, a_dt_bias,
           a_norm_w, a_w_out, a_ln_g, a_ln_b, b_w_kv, b_w_in, b_sinks, b_w_out, b_ln_g, b_ln_b):
    a_w_in16 = a_w_in.astype(BF16)
    zeros_h = jnp.zeros((N_A_LAYERS, A_HEADS), F32)
    wts = {
        "a_w_in": a_w_in16,
        "a_w_gate": jnp.pad(a_w_in16[:, :, A_MAIN:], ((0, 0), (0, 0), (0, LANES - 2 * A_HEADS))),
        "a_conv_w": a_conv_w,
        "a_log": jnp.concatenate([a_log.astype(F32), zeros_h], axis=1)[:, :, None],
        "a_dt": jnp.concatenate([a_dt_bias.astype(F32), zeros_h], axis=1)[:, :, None],
        "a_norm_w": a_norm_w.reshape(N_A_LAYERS, 1, A_DV),
        "a_w_out": a_w_out.astype(BF16),
        "a_ln_g": a_ln_g.reshape(N_A_LAYERS, 1, D_MODEL),
        "a_ln_b": a_ln_b.reshape(N_A_LAYERS, 1, D_MODEL),
        "b_w_kv": b_w_kv.astype(BF16),
        "b_w_in": jnp.concatenate([b_w_in[:, :, :B_QW], _paired_heads(b_w_in[:, :, B_QW:], axis=2)],
                                  axis=2).astype(BF16),
        "b_sinks": b_sinks,
        "b_w_out": _paired_heads(b_w_out, axis=1).astype(BF16),
        "b_ln_g": b_ln_g.reshape(N_B_LAYERS, 1, D_MODEL),
        "b_ln_b": b_ln_b.reshape(N_B_LAYERS, 1, D_MODEL),
    }
    bp, lp, _ = x_prompt.shape
    bs, ls, _ = x_sample.shape
    pos_prompt = jnp.arange(lp, dtype=jnp.int32)
    pos_sample = PAST_LEN + jnp.arange(ls, dtype=jnp.int32)
    zero_conv = jnp.zeros((N_A_LAYERS, bp, CONV_W - 1, CONV_DIM), F32)
    zero_delta = jnp.zeros((N_A_LAYERS, bp, A_HEADS, A_DK, A_DV), F32)
    y_p, p_conv, p_delta, p_k, p_v = _trunk(x_prompt, pos_prompt, zero_conv, zero_delta, None, None, wts)
    y_s, s_conv, s_delta, s_k, s_v = _trunk(x_sample, pos_sample, state_conv, state_delta, cache_k, cache_v, wts)
    return (y_p, y_s, p_delta, p_conv, p_k, p_v, s_delta, s_conv, s_k, s_v)
```

```python
import functools
import math

import jax
import jax.numpy as jnp
from jax import lax
from jax.experimental import pallas as pl
from jax.experimental.pallas import tpu as pltpu

D_MODEL = 1024
DEPTH = 4
PAST_LEN = 4096
CHUNK = 64
N_A_LAYERS = DEPTH // 2
N_B_LAYERS = DEPTH - N_A_LAYERS
A_HEADS = 8
A_DK = 128
A_DV = 128
A_QK = A_HEADS * A_DK
A_VW = A_HEADS * A_DV
CONV_W = 4
CONV_DIM = 2 * A_QK + A_VW
A_MAIN = CONV_DIM + A_VW
B_Q_HEADS = 16
B_KV_HEADS = 4
B_GROUP = B_Q_HEADS // B_KV_HEADS
B_HD = 64
B_QW = B_Q_HEADS * B_HD
B_KVW = B_KV_HEADS * B_HD
WINDOW = 128
WINDOW_CHUNKS = WINDOW // CHUNK
ROPE_DIMS = B_HD // 4
ROPE_THETA = 500000.0
DN_ALPHA = (2 * DEPTH) ** 0.25
LN_EPS = 1e-5
RMS_EPS = 1e-6

LANES = 128
SUBLANES = 8
VMEM_LIMIT = 48 * 1024 * 1024
ROW_TILE = 256
WIDE_TILE = 512
OUT_TILE = 1024
CONV_PIECE = 512
COL_BLOCK = 512
DELTA_BATCHES = 2
ATTN_CHUNKS = 8
B_QX = B_Q_HEADS * LANES
LOG2E = math.log2(math.e)

F32 = jnp.float32
BF16 = jnp.bfloat16


def _mm(a, b):
    return jnp.dot(a.astype(BF16), b.astype(BF16), preferred_element_type=F32)


def _mm_nt(a, b):
    return lax.dot_general(a.astype(BF16), b.astype(BF16), (((1,), (1,)), ((), ())),
                           preferred_element_type=F32)


def _silu(x):
    h = 0.5 * x
    return h + h * jnp.tanh(h)


def _params(semantics):
    return pltpu.CompilerParams(dimension_semantics=semantics, vmem_limit_bytes=VMEM_LIMIT)


def _weight_specs(layer, kdim, nblocks, grid_rank):
    def spec(j):
        if grid_rank == 1:
            return pl.BlockSpec((None, kdim, COL_BLOCK), lambda i: (layer, 0, j))
        return pl.BlockSpec((None, kdim, COL_BLOCK), lambda b, l: (layer, 0, j))
    return [spec(j) for j in range(nblocks)]


def _a_in_kernel(*refs, ts, ns):
    nw = A_MAIN // COL_BLOCK
    x_ref, w_refs = refs[0], refs[1:1 + nw]
    (wg_ref, cw_ref, c0_ref, alog_ref, dt_ref,
     q_ref, k_ref, v_ref, z_ref, gb_ref, cout_ref, pbuf, carry) = refs[1 + nw:]
    tm = ts * ns
    l = pl.program_id(1)
    tail = SUBLANES - (CONV_W - 1)

    @pl.when(l == 0)
    def _():
        for s in range(ns):
            carry[s, tail:SUBLANES, :] = c0_ref[s]

    xb = x_ref[...].astype(BF16)
    cb = COL_BLOCK
    nconv = CONV_DIM // cb
    outs = (q_ref, k_ref, v_ref)

    def conv_block(s, blk):
        rows = slice(s * ts, (s + 1) * ts)
        cols = slice(blk * LANES, (blk + 1) * LANES)
        ext = jnp.concatenate([carry[s, :, cols], pbuf[rows, cols]], axis=0)
        y = None
        for j in range(CONV_W):
            back = CONV_W - 1 - j
            tap = (pltpu.roll(ext, back, axis=0) if back else ext)[SUBLANES:] * cw_ref[j:j + 1, cols]
            y = tap if y is None else y + tap
        y = _silu(y)
        which, head = divmod(blk, A_HEADS)
        if which < 2:
            y = y * lax.rsqrt(jnp.sum(y * y, axis=-1, keepdims=True) + RMS_EPS)
            if which == 0:
                y = y * (A_DK ** -0.5)
        outs[which][rows, head * LANES:(head + 1) * LANES] = y.astype(outs[which].dtype)

    pc = CONV_PIECE
    npieces = CONV_DIM // pc

    def project(i):
        j, off = divmod(i * pc, cb)
        pbuf[:, i * pc:(i + 1) * pc] = jnp.dot(xb, w_refs[j][:, off:off + pc], preferred_element_type=F32)

    def gate_z(j):
        z_ref[:, j * cb:(j + 1) * cb] = jnp.dot(
            xb, w_refs[nconv + j][...], preferred_element_type=F32).astype(z_ref.dtype)

    def decay_beta():
        gates = jnp.dot(xb, wg_ref[...], preferred_element_type=F32)
        gt = gates.T[:2 * A_HEADS]
        rowi = lax.broadcasted_iota(jnp.int32, gt.shape, 0)
        sp_in = gt + dt_ref[...]
        softplus = jnp.maximum(sp_in, 0.0) + jnp.log1p(jnp.exp(-jnp.abs(sp_in)))
        gval = -jnp.exp(alog_ref[...]) * softplus
        bval = 1.0 / (1.0 + jnp.exp(-gt))
        res = jnp.where(rowi < A_HEADS, gval, bval)
        gb_ref[...] = jnp.concatenate([res, jnp.zeros((LANES - 2 * A_HEADS, tm), F32)], axis=0).T

    light = [functools.partial(gate_z, j) for j in range(A_VW // cb)] + [decay_beta]
    project(0)
    for i in range(npieces):
        if i + 1 < npieces:
            project(i + 1)
        if light and i % (npieces // 4) == 0:
            light.pop(0)()
        for s in range(ns):
            for blk in range(i * pc // LANES, (i + 1) * pc // LANES):
                conv_block(s, blk)
    assert not light

    for s in range(ns):
        last = pbuf[(s + 1) * ts - (CONV_W - 1):(s + 1) * ts, :]
        cout_ref[s] = last
        carry[s, tail:SUBLANES, :] = last


def _a_in(x2, w_in16, w_gate, conv_w, conv0, alog_col, dt_col, *, layer, batch, seq):
    if seq >= WIDE_TILE:
        ts, ns = WIDE_TILE, 1
    else:
        ts, ns = seq, min(batch, WIDE_TILE // seq)
    tm = ts * ns
    nl = seq // ts
    tokens = batch * seq
    row = lambda b, l: (b * nl + l, 0)
    const = lambda b, l: (0, 0)
    perseq = lambda b, l: (b, 0, 0)
    wide = jax.ShapeDtypeStruct((tokens, A_QK), BF16)
    nw = A_MAIN // COL_BLOCK
    return pl.pallas_call(
        functools.partial(_a_in_kernel, ts=ts, ns=ns),
        grid=(batch // ns, nl),
        in_specs=[pl.BlockSpec((tm, D_MODEL), row)] + _weight_specs(layer, D_MODEL, nw, 2) + [
            pl.BlockSpec((D_MODEL, LANES), const),
            pl.BlockSpec((CONV_W, CONV_DIM), const),
            pl.BlockSpec((ns, CONV_W - 1, CONV_DIM), perseq),
            pl.BlockSpec((2 * A_HEADS, 1), const),
            pl.BlockSpec((2 * A_HEADS, 1), const),
        ],
        out_specs=[
            pl.BlockSpec((tm, A_QK), row),
            pl.BlockSpec((tm, A_QK), row),
            pl.BlockSpec((tm, A_VW), row),
            pl.BlockSpec((tm, A_VW), row),
            pl.BlockSpec((tm, LANES), row),
            pl.BlockSpec((ns, CONV_W - 1, CONV_DIM), perseq),
        ],
        out_shape=[wide, wide, wide, wide,
                   jax.ShapeDtypeStruct((tokens, LANES), F32),
                   jax.ShapeDtypeStruct((batch, CONV_W - 1, CONV_DIM), F32)],
        scratch_shapes=[pltpu.VMEM((tm, CONV_DIM), F32), pltpu.VMEM((ns, SUBLANES, CONV_DIM), F32)],
        compiler_params=_params(("arbitrary", "arbitrary")),
        name="a_in",
    )(x2, *([w_in16] * nw), w_gate, conv_w, conv0, alog_col, dt_col)


def _delta_group(bi, q_ref, k_ref, v_ref, z_ref, gb_ref, nw, o_ref, s_ref, c):
    pk = LANES // c
    lc = int(math.log2(c))
    packs = [list(range(p * pk, (p + 1) * pk)) for p in range(A_HEADS // pk)]
    npk = range(len(packs))
    row = lax.broadcasted_iota(jnp.int32, (c, LANES), 0)
    lane = lax.broadcasted_iota(jnp.int32, (c, LANES), 1)
    colr = lane & (c - 1)
    member = lane >> lc
    eye = (row == colr).astype(F32)
    incl = row >= colr
    strict = row > colr
    diag8 = (row >> 3) == (colr >> 3)
    sq0 = lax.broadcasted_iota(jnp.int32, (LANES, LANES), 0)
    sq1 = lax.broadcasted_iota(jnp.int32, (LANES, LANES), 1)
    bd_mask = (sq0 >> lc) == (sq1 >> lc)
    kr0 = lax.broadcasted_iota(jnp.int32, (LANES, pk * LANES), 0)
    kr1 = lax.broadcasted_iota(jnp.int32, (LANES, pk * LANES), 1)
    k_mask = (kr0 >> lc) == (kr1 >> int(math.log2(LANES)))

    def bd(m):
        return jnp.where(bd_mask, jnp.concatenate([m] * pk, axis=0), 0.0)

    def by_member(vals):
        out = vals[0]
        for r in range(1, pk):
            out = jnp.where(member >= r, vals[r], out)
        return out

    def wide(col, hs):
        return jnp.concatenate([jnp.broadcast_to(col[h], (c, LANES)) for h in hs], axis=1)

    def hcol(r):
        return slice(r * LANES, (r + 1) * LANES)

    g = gb_ref[bi]
    gcum = g
    step = 1
    while step < c:
        gcum = gcum + jnp.where(row >= step, pltpu.roll(gcum, step, axis=0), 0.0)
        step *= 2
    gcum_t = jnp.concatenate([gcum] * pk, axis=0).T
    gc = [gcum[:, h:h + 1] for h in range(A_HEADS)]
    beta = [g[:, A_HEADS + h:A_HEADS + h + 1] for h in range(A_HEADS)]
    glast = [gcum[c - 1:c, h:h + 1] for h in range(A_HEADS)]
    eg = [jnp.exp(x) for x in gc]
    pcols = [slice(hs[0] * LANES, (hs[-1] + 1) * LANES) for hs in packs]
    kpf = [k_ref[bi, :, pc].astype(F32) for pc in pcols]
    qp16 = [q_ref[bi, :, pc] for pc in pcols]
    vpf = [v_ref[bi, :, pc].astype(F32) for pc in pcols]
    kbeta = [kpf[p] * wide(beta, hs) for p, hs in enumerate(packs)]
    decay = [jnp.exp(jnp.where(incl, by_member([jnp.broadcast_to(gc[h], (c, LANES)) for h in hs])
                               - by_member([gcum_t[h:h + 1, :] for h in hs]), -jnp.inf))
             for hs in packs]
    yield
    k_bd = [jnp.where(k_mask, jnp.concatenate([x] * pk, axis=0), 0.0).astype(BF16) for x in kpf]
    kq = [_mm_nt(jnp.concatenate([kbeta[p].astype(BF16), qp16[p]], axis=0), k_bd[p])
          for p in npk]
    yield
    a = [jnp.where(strict, kq[p][:c] * decay[p], 0.0) for p in npk]
    d = [jnp.where(diag8, x, 0.0) for x in a]
    heads = [(p, r, hs[r]) for p, hs in enumerate(packs) for r in range(pk)]
    s = [s_ref[bi, h] for _, _, h in heads]
    ks = [_mm(jnp.concatenate([kbeta[p][:, hcol(r)] * eg[h], qp16[p][:, hcol(r)].astype(F32) * eg[h]], axis=0), s[i])
          for i, (p, r, h) in enumerate(heads)]
    yield
    d2 = [_mm(x, bd(x)) for x in d]
    yield
    d4 = [_mm(x, bd(x)) for x in d2]
    p1 = [_mm(eye - x, bd(eye + y)) for x, y in zip(d, d2)]
    yield
    xs = [_mm(p, bd(eye + y)) for p, y in zip(p1, d4)]
    yield
    shift = 3
    while (1 << shift) < c:
        mask = ((row >> (shift + 1)) == (colr >> (shift + 1))) & ((row >> shift) > (colr >> shift))
        xc = [_mm(x, bd(jnp.where(mask, m, 0.0))) for x, m in zip(xs, a)]
        yield
        xs = [x - _mm(y, bd(x)) for x, y in zip(xs, xc)]
        yield
        shift += 1
    vn = []
    for p, hs in enumerate(packs):
        vb = vpf[p] * wide(beta, hs)
        stacked = jnp.concatenate([vb[:, hcol(r)] - ks[p * pk + r][:c] for r in range(pk)], axis=0)
        vn.append(_mm(bd(xs[p]), stacked))
    yield
    v_new = [vn[p][r * c:(r + 1) * c] for p, r, _ in heads]
    intra = [kq[p][c:] * decay[p] for p in npk]
    kd_t = [(kpf[p][:, hcol(r)] * jnp.exp(glast[h] - gc[h])).T for p, r, h in heads]
    ov = [_mm(jnp.concatenate([intra[p][:, r * c:(r + 1) * c], kd_t[i]], axis=0), v_new[i])
          for i, (p, r, h) in enumerate(heads)]
    yield
    for i, (p, r, h) in enumerate(heads):
        s_ref[bi, h] = s[i] * jnp.exp(glast[h]) + ov[i][c:]
        o = ks[i][c:] + ov[i][:c]
        zf = z_ref[bi, :, h * LANES:(h + 1) * LANES].astype(F32)
        gated = o * lax.rsqrt(jnp.mean(o * o, axis=-1, keepdims=True) + RMS_EPS) * nw * _silu(zf)
        o_ref[bi, :, h * LANES:(h + 1) * LANES] = gated.astype(o_ref.dtype)


def _run_interleaved(gens):
    live = list(gens)
    while live:
        for gen in list(live):
            try:
                next(gen)
            except StopIteration:
                live.remove(gen)


def _delta_kernel(q_ref, k_ref, v_ref, z_ref, gb_ref, s0_ref, nw_ref, o_ref, s_ref, *, c, nb):
    n = pl.program_id(1)

    @pl.when(n == 0)
    def _():
        s_ref[...] = s0_ref[...]

    nw = nw_ref[...]
    _run_interleaved([_delta_group(bi, q_ref, k_ref, v_ref, z_ref, gb_ref, nw, o_ref, s_ref, c) for bi in range(nb)])


def _delta(q, k, v, z, gb, s0_all, norm_w, *, layer, batch, seq):
    c = min(CHUNK, seq)
    n = seq // c
    nb = DELTA_BATCHES
    seq3 = lambda a: a.reshape(batch, seq, a.shape[-1])
    blk = lambda b, i: (b, i, 0)
    state = lambda b, i: (b, 0, 0, 0)
    o, s_new = pl.pallas_call(
        functools.partial(_delta_kernel, c=c, nb=nb),
        grid=(batch // nb, n),
        in_specs=[
            pl.BlockSpec((nb, c, A_QK), blk),
            pl.BlockSpec((nb, c, A_QK), blk),
            pl.BlockSpec((nb, c, A_VW), blk),
            pl.BlockSpec((nb, c, A_VW), blk),
            pl.BlockSpec((nb, c, LANES), blk),
            pl.BlockSpec((None, nb, A_HEADS, A_DK, A_DV), lambda b, i: (layer, b, 0, 0, 0)),
            pl.BlockSpec((1, LANES), lambda b, i: (0, 0)),
        ],
        out_specs=[
            pl.BlockSpec((nb, c, A_VW), blk),
            pl.BlockSpec((nb, A_HEADS, A_DK, A_DV), state),
        ],
        out_shape=[jax.ShapeDtypeStruct((batch, seq, A_VW), BF16),
                   jax.ShapeDtypeStruct((batch, A_HEADS, A_DK, A_DV), F32)],
        compiler_params=_params(("arbitrary", "arbitrary")),
        name="delta_rule",
    )(seq3(q), seq3(k), seq3(v), seq3(z), seq3(gb), s0_all, norm_w)
    return o.reshape(batch * seq, A_VW), s_new


def _out_ln_kernel(*refs, tm, sub):
    nw = D_MODEL // COL_BLOCK
    o_ref, x_ref, w_refs = refs[0], refs[1], refs[2:2 + nw]
    g_ref, b_ref, y_ref = refs[2 + nw:]
    for r0 in range(0, tm, sub):
        rows = slice(r0, r0 + sub)
        o = o_ref[rows, :]
        proj = jnp.concatenate([jnp.dot(o, w[...], preferred_element_type=F32) for w in w_refs], axis=1)
        r = DN_ALPHA * x_ref[rows, :] + proj
        mu = jnp.mean(r, axis=-1, keepdims=True)
        d = r - mu
        var = jnp.mean(d * d, axis=-1, keepdims=True)
        y_ref[rows, :] = d * lax.rsqrt(var + LN_EPS) * g_ref[...] + b_ref[...]


def _out_ln(o, x2, w_out16, ln_g, ln_b, *, layer):
    tokens = x2.shape[0]
    tm = min(OUT_TILE, tokens)
    sub = min(ROW_TILE, tm)
    row = lambda i: (i, 0)
    const = lambda i: (0, 0)
    nw = D_MODEL // COL_BLOCK
    return pl.pallas_call(
        functools.partial(_out_ln_kernel, tm=tm, sub=sub),
        grid=(tokens // tm,),
        in_specs=[pl.BlockSpec((tm, o.shape[1]), row), pl.BlockSpec((tm, D_MODEL), row)]
        + _weight_specs(layer, o.shape[1], nw, 1)
        + [pl.BlockSpec((1, D_MODEL), const), pl.BlockSpec((1, D_MODEL), const)],
        out_specs=pl.BlockSpec((tm, D_MODEL), row),
        out_shape=jax.ShapeDtypeStruct((tokens, D_MODEL), F32),
        compiler_params=_params(("arbitrary",)),
        name="out_ln",
    )(o, x2, *([w_out16] * nw), ln_g, ln_b)


def _rope_slab(x, cos_t, sin_up, sin_dn):
    half = ROPE_DIMS // 2
    return x * cos_t + pltpu.roll(x, LANES - half, axis=1) * sin_up + pltpu.roll(x, half, axis=1) * sin_dn


def _row_tiling(tokens, seq):
    tm = min(WIDE_TILE, tokens)
    if seq >= tm:
        per_seq = seq // tm
        return tm, 1, (lambda i: (i % per_seq, 0))
    return tm, tm // seq, (lambda i: (0, 0))


def _tile_tables(tables, reps):
    return tables if reps == 1 else tuple(jnp.tile(t, (reps, 1)) for t in tables)


def _kv_kernel(h_ref, w_ref, cos_ref, sup_ref, sdn_ref, k_ref, v_ref, klast_ref, vlast_ref, *, rows_per_seq, ns, keep):
    kv = jnp.dot(h_ref[...].astype(BF16), w_ref[...], preferred_element_type=F32)
    cos_t, sin_up, sin_dn = cos_ref[...], sup_ref[...], sdn_ref[...]
    k = jnp.concatenate([_rope_slab(kv[:, s * LANES:(s + 1) * LANES], cos_t, sin_up, sin_dn)
                         for s in range(B_KVW // LANES)], axis=1)
    v = kv[:, B_KVW:]
    k_ref[...] = k.astype(k_ref.dtype)
    v_ref[...] = v.astype(v_ref.dtype)
    for s in range(ns):
        end = (s + 1) * rows_per_seq
        klast_ref[s] = k[end - keep:end, :]
        vlast_ref[s] = v[end - keep:end, :]


def _shared_kv(h2, w_kv16, tables, *, batch, seq):
    tokens = batch * seq
    tm, ns, tab = _row_tiling(tokens, seq)
    keep = min(WINDOW, seq)
    rows_per_seq = tm // ns
    assert keep <= rows_per_seq
    per_seq = max(1, seq // tm)
    row = lambda i: (i, 0)
    lastb = lambda i: (i // per_seq, 0, 0)
    out = jax.ShapeDtypeStruct((tokens, B_KVW), BF16)
    last = jax.ShapeDtypeStruct((batch, keep, B_KVW), F32)
    return pl.pallas_call(
        functools.partial(_kv_kernel, rows_per_seq=rows_per_seq, ns=ns, keep=keep),
        grid=(tokens // tm,),
        in_specs=[
            pl.BlockSpec((tm, D_MODEL), row),
            pl.BlockSpec((D_MODEL, 2 * B_KVW), lambda i: (0, 0)),
            pl.BlockSpec((tm, LANES), tab),
            pl.BlockSpec((tm, LANES), tab),
            pl.BlockSpec((tm, LANES), tab),
        ],
        out_specs=[pl.BlockSpec((tm, B_KVW), row), pl.BlockSpec((tm, B_KVW), row),
                   pl.BlockSpec((ns, keep, B_KVW), lastb), pl.BlockSpec((ns, keep, B_KVW), lastb)],
        out_shape=[out, out, last, last],
        compiler_params=_params(("arbitrary",)),
        name="shared_kv",
    )(h2, w_kv16, *_tile_tables(tables, ns))


def _b_in_kernel(*refs, tm):
    nw = 2 * B_QW // COL_BLOCK
    x_ref, w_refs = refs[0], refs[1:1 + nw]
    cos_ref, sup_ref, sdn_ref, qx_ref, z_ref = refs[1 + nw:]
    xb = x_ref[...].astype(BF16)
    cos_t, sin_up, sin_dn = cos_ref[...], sup_ref[...], sdn_ref[...]
    lane_half = lax.broadcasted_iota(jnp.int32, (tm, LANES), 1) >> int(math.log2(B_HD))
    cb = COL_BLOCK
    nq = B_QW // cb
    for j in range(nq):
        proj = jnp.dot(xb, w_refs[j][...], preferred_element_type=F32)
        for sl in range(cb // LANES):
            slab = j * (cb // LANES) + sl
            rot = _rope_slab(proj[:, sl * LANES:(sl + 1) * LANES], cos_t, sin_up, sin_dn) * (B_HD ** -0.5 * LOG2E)
            for p in range(2):
                hq = 2 * slab + p
                x = jnp.where(lane_half == p, rot, 0.0)
                if p != (hq // B_GROUP) % 2:
                    x = pltpu.roll(x, B_HD, axis=1)
                qx_ref[:, hq * LANES:(hq + 1) * LANES] = x.astype(qx_ref.dtype)
    for j in range(nq):
        z_ref[:, j * cb:(j + 1) * cb] = jnp.dot(
            xb, w_refs[nq + j][...], preferred_element_type=F32).astype(z_ref.dtype)


def _b_in(x2, w_in16, tables, *, layer, batch, seq):
    tokens = batch * seq
    tm, ns, tab = _row_tiling(tokens, seq)
    row = lambda i: (i, 0)
    nw = 2 * B_QW // COL_BLOCK
    return pl.pallas_call(
        functools.partial(_b_in_kernel, tm=tm),
        grid=(tokens // tm,),
        in_specs=[pl.BlockSpec((tm, D_MODEL), row)] + _weight_specs(layer, D_MODEL, nw, 1) + [
            pl.BlockSpec((tm, LANES), tab),
            pl.BlockSpec((tm, LANES), tab),
            pl.BlockSpec((tm, LANES), tab),
        ],
        out_specs=[pl.BlockSpec((tm, B_QX), row), pl.BlockSpec((tm, B_QW), row)],
        out_shape=[jax.ShapeDtypeStruct((tokens, B_QX), BF16), jax.ShapeDtypeStruct((tokens, B_QW), BF16)],
        compiler_params=_params(("arbitrary",)),
        name="b_in",
    )(x2, *([w_in16] * nw), *_tile_tables(tables, ns))


def _attn_body(sink_ref, qx_ref, z_ref, k_ref, v_ref, o_ref, *, lq, lk, nc, masked):
    step = pl.program_id(1)
    lane_half = lax.broadcasted_iota(jnp.int32, (lq, LANES), 1) >> int(math.log2(B_HD))
    kv_slabs = range(B_KVW // LANES)
    kwin, vwin, valid = [], [], []
    for ci in range(nc):
        if masked is None:
            start, seq_i = 0, ci
            valid.append(None)
        else:
            seq_i = 0
            c = step * nc + ci
            first = jnp.maximum(c - WINDOW_CHUNKS, 0)
            start = pl.multiple_of(first * CHUNK, CHUNK)
            key_chunk = first + (lax.broadcasted_iota(jnp.int32, (lq, lk), 1) >> int(math.log2(CHUNK)))
            valid.append((key_chunk <= c) if masked else None)
        kwin.append([k_ref[seq_i, pl.ds(start, lk), s * LANES:(s + 1) * LANES] for s in kv_slabs])
        vwin.append([v_ref[seq_i, pl.ds(start, lk), s * LANES:(s + 1) * LANES] for s in kv_slabs])
    units = [(ci, j) for ci in range(nc) for j in range(B_KV_HEADS)]
    qstack = [jnp.concatenate([qx_ref[ci * lq:(ci + 1) * lq, (B_GROUP * j + g) * LANES:(B_GROUP * j + g + 1) * LANES]
                               for g in range(B_GROUP)], axis=0) for ci, j in units]
    scores = [_mm_nt(qstack[u], kwin[ci][j // 2]) for u, (ci, j) in enumerate(units)]
    heads = [(u, ci, j, g) for u, (ci, j) in enumerate(units) for g in range(B_GROUP)]
    sk = [sink_ref[B_GROUP * j + g] * LOG2E for _, _, j, g in heads]
    sc = [scores[u][g * lq:(g + 1) * lq] for u, _, _, g in heads]
    sc = [s if valid[ci] is None else jnp.where(valid[ci], s, -jnp.inf) for s, (_, ci, _, _) in zip(sc, heads)]
    mx = [jnp.maximum(jnp.max(s, axis=-1, keepdims=True), k) for s, k in zip(sc, sk)]
    pstack = [jnp.concatenate([jnp.exp2(sc[h] - mx[h]).astype(BF16) for h in range(u * B_GROUP, (u + 1) * B_GROUP)],
                              axis=0) for u in range(len(units))]
    ones = jnp.ones((lk, LANES), BF16)
    pv = [_mm(pstack[u], jnp.concatenate([vwin[ci][j // 2], ones], axis=1))
          for u, (ci, j) in enumerate(units)]
    out = []
    for h, (u, _, _, g) in enumerate(heads):
        part = pv[u][g * lq:(g + 1) * lq]
        out.append(part[:, :LANES] / (part[:, LANES:] + jnp.exp2(sk[h] - mx[h])))
    for ci in range(nc):
        rows = slice(ci * lq, (ci + 1) * lq)
        for slab in range(B_QW // LANES):
            pair, g = divmod(slab, B_GROUP)
            lo = out[(ci * B_KV_HEADS + 2 * pair) * B_GROUP + g]
            hi = out[(ci * B_KV_HEADS + 2 * pair + 1) * B_GROUP + g]
            both = jnp.where(lane_half == 0, lo, hi)
            zs = z_ref[rows, slab * LANES:(slab + 1) * LANES].astype(F32)
            o_ref[rows, slab * LANES:(slab + 1) * LANES] = (both * _silu(zs)).astype(o_ref.dtype)


def _attn_kernel(sink_ref, qx_ref, z_ref, k_ref, v_ref, o_ref, *, lq, lk, nc, banded):
    body = functools.partial(_attn_body, sink_ref, qx_ref, z_ref, k_ref, v_ref, o_ref, lq=lq, lk=lk, nc=nc)
    if not banded:
        body(masked=None)
        return
    step = pl.program_id(1)
    clamped_steps = -(-WINDOW_CHUNKS // nc)

    @pl.when(step < clamped_steps)
    def _():
        body(masked=True)

    @pl.when(step >= clamped_steps)
    def _():
        body(masked=False)


def _attention(qx, z, k3, v3, sinks, *, batch, seq, banded):
    lq = min(CHUNK, seq)
    ltot = k3.shape[1]
    if banded:
        nc, nseq = min(ATTN_CHUNKS, seq // lq), 1
        lk = (WINDOW_CHUNKS + 1) * CHUNK
    else:
        assert seq == lq
        nc = nseq = min(ATTN_CHUNKS, batch)
        lk = ltot
    nq = seq * nseq // (lq * nc)
    tokens = batch * seq
    row = lambda b, c: (b * nq + c, 0)
    whole = lambda b, c: (b, 0, 0)
    return pl.pallas_call(
        functools.partial(_attn_kernel, lq=lq, lk=lk, nc=nc, banded=banded),
        grid=(batch // nseq, nq),
        in_specs=[
            pl.BlockSpec(memory_space=pltpu.SMEM),
            pl.BlockSpec((nc * lq, B_QX), row),
            pl.BlockSpec((nc * lq, B_QW), row),
            pl.BlockSpec((nseq, ltot, B_KVW), whole),
            pl.BlockSpec((nseq, ltot, B_KVW), whole),
        ],
        out_specs=pl.BlockSpec((nc * lq, B_QW), row),
        out_shape=jax.ShapeDtypeStruct((tokens, B_QW), BF16),
        compiler_params=_params(("arbitrary", "arbitrary")),
        name="swa_attention",
    )(sinks, qx, z, k3, v3)


def _rope_tables(pos):
    half = ROPE_DIMS // 2
    inv = ROPE_THETA ** (-jnp.arange(half, dtype=F32) * 2.0 / ROPE_DIMS)
    ang = pos.astype(F32)[:, None] * inv[None, :]
    cos, sin = jnp.cos(ang), jnp.sin(ang)
    ones = jnp.ones((pos.shape[0], B_HD - ROPE_DIMS), F32)
    zeros_h = jnp.zeros((pos.shape[0], half), F32)
    zeros_r = jnp.zeros((pos.shape[0], B_HD - ROPE_DIMS), F32)
    cos_head = jnp.concatenate([cos, cos, ones], axis=1)
    up_head = jnp.concatenate([-sin, zeros_h, zeros_r], axis=1)
    dn_head = jnp.concatenate([zeros_h, sin, zeros_r], axis=1)
    rep = LANES // B_HD
    return tuple(jnp.tile(t, (1, rep)) for t in (cos_head, up_head, dn_head))


def _trunk(x, pos, conv_state, delta_state, past_k, past_v, wts):
    batch, seq, _ = x.shape
    tokens = batch * seq
    h = x.reshape(tokens, D_MODEL)
    tables = _rope_tables(pos)
    new_conv, new_delta = [], []
    for i in range(N_A_LAYERS):
        q, k, v, z, gb, cbuf = _a_in(h, wts["a_w_in"], wts["a_w_gate"][i], wts["a_conv_w"][i], conv_state[i],
                                     wts["a_log"][i], wts["a_dt"][i], layer=i, batch=batch, seq=seq)
        o, s_new = _delta(q, k, v, z, gb, delta_state, wts["a_norm_w"][i], layer=i, batch=batch, seq=seq)
        h = _out_ln(o, h, wts["a_w_out"], wts["a_ln_g"][i], wts["a_ln_b"][i], layer=i)
        new_conv.append(cbuf)
        new_delta.append(s_new)
    k2, v2, k_last, v_last = _shared_kv(h, wts["b_w_kv"], tables, batch=batch, seq=seq)
    k3 = k2.reshape(batch, seq, B_KVW)
    v3 = v2.reshape(batch, seq, B_KVW)
    cached = past_k is not None
    if cached:
        pk = past_k.reshape(batch, -1, B_KVW)
        pv = past_v.reshape(batch, -1, B_KVW)
        k3 = jnp.concatenate([pk.astype(BF16), k3], axis=1)
        v3 = jnp.concatenate([pv.astype(BF16), v3], axis=1)
        k_last = jnp.concatenate([pk, k_last], axis=1)[:, -WINDOW:]
        v_last = jnp.concatenate([pv, v_last], axis=1)[:, -WINDOW:]
    new_k = k_last.reshape(batch, WINDOW, B_KV_HEADS, B_HD)
    new_v = v_last.reshape(batch, WINDOW, B_KV_HEADS, B_HD)
    for j in range(N_B_LAYERS):
        qx, z = _b_in(h, wts["b_w_in"], tables, layer=j, batch=batch, seq=seq)
        o = _attention(qx, z, k3, v3, wts["b_sinks"][j], batch=batch, seq=seq, banded=not cached)
        h = _out_ln(o, h, wts["b_w_out"], wts["b_ln_g"][j], wts["b_ln_b"][j], layer=j)
    return h.reshape(batch, seq, D_MODEL), jnp.stack(new_conv), jnp.stack(new_delta), new_k, new_v


def _paired_heads(w, axis):
    order = [B_GROUP * (2 * pair + odd) + g
             for pair in range(B_KV_HEADS // 2) for g in range(B_GROUP) for odd in range(2)]
    shape = w.shape
    blocks = w.reshape(shape[:axis] + (B_Q_HEADS, B_HD) + shape[axis + 1:])
    return jnp.take(blocks, jnp.array(order, jnp.int32), axis=axis).reshape(shape)


def kernel(x_prompt, x_sample, state_delta, state_conv, cache_k, cache_v, a_w_in, a_conv_w, a_log, a_dt_bias,
           a_norm_w, a_w_out, a_ln_g, a_ln_b, b_w_kv, b_w_in, b_sinks, b_w_out, b_ln_g, b_ln_b):
    a_w_in16 = a_w_in.astype(BF16)
    zeros_h = jnp.zeros((N_A_LAYERS, A_HEADS), F32)
    wts = {
        "a_w_in": a_w_in16,
        "a_w_gate": jnp.pad(a_w_in16[:, :, A_MAIN:], ((0, 0), (0, 0), (0, LANES - 2 * A_HEADS))),
        "a_conv_w": a_conv_w,
        "a_log": jnp.concatenate([a_log.astype(F32), zeros_h], axis=1)[:, :, None],
        "a_dt": jnp.concatenate([a_dt_bias.astype(F32), zeros_h], axis=1)[:, :, None],
        "a_norm_w": a_norm_w.reshape(N_A_LAYERS, 1, A_DV),
        "a_w_out": a_w_out.astype(BF16),
        "a_ln_g": a_ln_g.reshape(N_A_LAYERS, 1, D_MODEL),
        "a_ln_b": a_ln_b.reshape(N_A_LAYERS, 1, D_MODEL),
        "b_w_kv": b_w_kv.astype(BF16),
        "b_w_in": jnp.concatenate([b_w_in[:, :, :B_QW], _paired_heads(b_w_in[:, :, B_QW:], axis=2)],
                                  axis=2).astype(BF16),
        "b_sinks": b_sinks,
        "b_w_out": _paired_heads(b_w_out, axis=1).astype(BF16),
        "b_ln_g": b_ln_g.reshape(N_B_LAYERS, 1, D_MODEL),
        "b_ln_b": b_ln_b.reshape(N_B_LAYERS, 1, D_MODEL),
    }
    bp, lp, _ = x_prompt.shape
    bs, ls, _ = x_sample.shape
    pos_prompt = jnp.arange(lp, dtype=jnp.int32)
    pos_sample = PAST_LEN + jnp.arange(ls, dtype=jnp.int32)
    zero_conv = jnp.zeros((N_A_LAYERS, bp, CONV_W - 1, CONV_DIM), F32)
    zero_delta = jnp.zeros((N_A_LAYERS, bp, A_HEADS, A_DK, A_DV), F32)
    y_p, p_conv, p_delta, p_k, p_v = _trunk(x_prompt, pos_prompt, zero_conv, zero_delta, None, None, wts)
    y_s, s_conv, s_delta, s_k, s_v = _trunk(x_sample, pos_sample, state_conv, state_delta, cache_k, cache_v, wts)
    return (y_p, y_s, p_delta, p_conv, p_k, p_v, s_delta, s_conv, s_k, s_v)
```

```python
import functools
import math

import jax
import jax.numpy as jnp
from jax import lax
from jax.experimental import pallas as pl
from jax.experimental.pallas import tpu as pltpu

D_MODEL = 1024
DEPTH = 4
PAST_LEN = 4096
CHUNK = 64
N_A_LAYERS = DEPTH // 2
N_B_LAYERS = DEPTH - N_A_LAYERS
A_HEADS = 8
A_DK = 128
A_DV = 128
A_QK = A_HEADS * A_DK
A_VW = A_HEADS * A_DV
CONV_W = 4
CONV_DIM = 2 * A_QK + A_VW
A_MAIN = CONV_DIM + A_VW
B_Q_HEADS = 16
B_KV_HEADS = 4
B_GROUP = B_Q_HEADS // B_KV_HEADS
B_HD = 64
B_QW = B_Q_HEADS * B_HD
B_KVW = B_KV_HEADS * B_HD
WINDOW = 128
WINDOW_CHUNKS = WINDOW // CHUNK
ROPE_DIMS = B_HD // 4
ROPE_THETA = 500000.0
DN_ALPHA = (2 * DEPTH) ** 0.25
LN_EPS = 1e-5
RMS_EPS = 1e-6

LANES = 128
SUBLANES = 8
VMEM_LIMIT = 48 * 1024 * 1024
ROW_TILE = 256
WIDE_TILE = 512
OUT_TILE = 1024
CONV_PIECE = 512
COL_BLOCK = 512
DELTA_BATCHES = 2
DELTA_CHUNKS = 8
ATTN_CHUNKS = 8
B_QX = B_Q_HEADS * LANES
LOG2E = math.log2(math.e)

F32 = jnp.float32
BF16 = jnp.bfloat16


def _mm(a, b):
    return jnp.dot(a.astype(BF16), b.astype(BF16), preferred_element_type=F32)


def _mm_nt(a, b):
    return lax.dot_general(a.astype(BF16), b.astype(BF16), (((1,), (1,)), ((), ())),
                           preferred_element_type=F32)


def _silu(x):
    h = 0.5 * x
    return h + h * jnp.tanh(h)


def _params(semantics):
    return pltpu.CompilerParams(dimension_semantics=semantics, vmem_limit_bytes=VMEM_LIMIT)


def _weight_specs(layer, kdim, nblocks, grid_rank):
    def spec(j):
        if grid_rank == 1:
            return pl.BlockSpec((None, kdim, COL_BLOCK), lambda i: (layer, 0, j))
        return pl.BlockSpec((None, kdim, COL_BLOCK), lambda b, l: (layer, 0, j))
    return [spec(j) for j in range(nblocks)]


def _a_in_kernel(*refs, ts, ns):
    nw = A_MAIN // COL_BLOCK
    x_ref, w_refs = refs[0], refs[1:1 + nw]
    (wg_ref, cw_ref, c0_ref, alog_ref, dt_ref,
     q_ref, k_ref, v_ref, z_ref, gb_ref, cout_ref, pbuf, carry) = refs[1 + nw:]
    tm = ts * ns
    l = pl.program_id(1)
    tail = SUBLANES - (CONV_W - 1)

    @pl.when(l == 0)
    def _():
        for s in range(ns):
            carry[s, tail:SUBLANES, :] = c0_ref[s]

    xb = x_ref[...].astype(BF16)
    cb = COL_BLOCK
    nconv = CONV_DIM // cb
    outs = (q_ref, k_ref, v_ref)

    def conv_block(s, blk):
        rows = slice(s * ts, (s + 1) * ts)
        cols = slice(blk * LANES, (blk + 1) * LANES)
        ext = jnp.concatenate([carry[s, :, cols], pbuf[rows, cols]], axis=0)
        y = None
        for j in range(CONV_W):
            back = CONV_W - 1 - j
            tap = (pltpu.roll(ext, back, axis=0) if back else ext)[SUBLANES:] * cw_ref[j:j + 1, cols]
            y = tap if y is None else y + tap
        y = _silu(y)
        which, head = divmod(blk, A_HEADS)
        if which < 2:
            y = y * lax.rsqrt(jnp.sum(y * y, axis=-1, keepdims=True) + RMS_EPS)
            if which == 0:
                y = y * (A_DK ** -0.5)
        outs[which][rows, head * LANES:(head + 1) * LANES] = y.astype(outs[which].dtype)

    pc = CONV_PIECE
    npieces = CONV_DIM // pc

    def project(i):
        j, off = divmod(i * pc, cb)
        pbuf[:, i * pc:(i + 1) * pc] = jnp.dot(xb, w_refs[j][:, off:off + pc], preferred_element_type=F32)

    def gate_z(j):
        z_ref[:, j * cb:(j + 1) * cb] = jnp.dot(
            xb, w_refs[nconv + j][...], preferred_element_type=F32).astype(z_ref.dtype)

    def decay_beta():
        gates = jnp.dot(xb, wg_ref[...], preferred_element_type=F32)
        gt = gates.T[:2 * A_HEADS]
        rowi = lax.broadcasted_iota(jnp.int32, gt.shape, 0)
        sp_in = gt + dt_ref[...]
        softplus = jnp.maximum(sp_in, 0.0) + jnp.log1p(jnp.exp(-jnp.abs(sp_in)))
        gval = -jnp.exp(alog_ref[...]) * softplus
        bval = 1.0 / (1.0 + jnp.exp(-gt))
        res = jnp.where(rowi < A_HEADS, gval, bval)
        gb_ref[...] = jnp.concatenate([res, jnp.zeros((LANES - 2 * A_HEADS, tm), F32)], axis=0).T

    light = [functools.partial(gate_z, j) for j in range(A_VW // cb)] + [decay_beta]
    project(0)
    for i in range(npieces):
        if i + 1 < npieces:
            project(i + 1)
        if light and i % (npieces // 4) == 0:
            light.pop(0)()
        for s in range(ns):
            for blk in range(i * pc // LANES, (i + 1) * pc // LANES):
                conv_block(s, blk)
    assert not light

    for s in range(ns):
        last = pbuf[(s + 1) * ts - (CONV_W - 1):(s + 1) * ts, :]
        cout_ref[s] = last
        carry[s, tail:SUBLANES, :] = last


def _a_in(x2, w_in16, w_gate, conv_w, conv0, alog_col, dt_col, *, layer, batch, seq):
    if seq >= WIDE_TILE:
        ts, ns = WIDE_TILE, 1
    else:
        ts, ns = seq, min(batch, WIDE_TILE // seq)
    tm = ts * ns
    nl = seq // ts
    tokens = batch * seq
    row = lambda b, l: (b * nl + l, 0)
    const = lambda b, l: (0, 0)
    perseq = lambda b, l: (b, 0, 0)
    wide = jax.ShapeDtypeStruct((tokens, A_QK), BF16)
    nw = A_MAIN // COL_BLOCK
    return pl.pallas_call(
        functools.partial(_a_in_kernel, ts=ts, ns=ns),
        grid=(batch // ns, nl),
        in_specs=[pl.BlockSpec((tm, D_MODEL), row)] + _weight_specs(layer, D_MODEL, nw, 2) + [
            pl.BlockSpec((D_MODEL, LANES), const),
            pl.BlockSpec((CONV_W, CONV_DIM), const),
            pl.BlockSpec((ns, CONV_W - 1, CONV_DIM), perseq),
            pl.BlockSpec((2 * A_HEADS, 1), const),
            pl.BlockSpec((2 * A_HEADS, 1), const),
        ],
        out_specs=[
            pl.BlockSpec((tm, A_QK), row),
            pl.BlockSpec((tm, A_QK), row),
            pl.BlockSpec((tm, A_VW), row),
            pl.BlockSpec((tm, A_VW), row),
            pl.BlockSpec((tm, LANES), row),
            pl.BlockSpec((ns, CONV_W - 1, CONV_DIM), perseq),
        ],
        out_shape=[wide, wide, wide, wide,
                   jax.ShapeDtypeStruct((tokens, LANES), F32),
                   jax.ShapeDtypeStruct((batch, CONV_W - 1, CONV_DIM), F32)],
        scratch_shapes=[pltpu.VMEM((tm, CONV_DIM), F32), pltpu.VMEM((ns, SUBLANES, CONV_DIM), F32)],
        compiler_params=_params(("arbitrary", "arbitrary")),
        name="a_in",
    )(x2, *([w_in16] * nw), w_gate, conv_w, conv0, alog_col, dt_col)


def _delta_group(bi, r0, q_ref, k_ref, v_ref, z_ref, gb_ref, nw, o_ref, s_ref, c):
    pk = LANES // c
    lc = int(math.log2(c))
    packs = [list(range(p * pk, (p + 1) * pk)) for p in range(A_HEADS // pk)]
    npk = range(len(packs))
    row = lax.broadcasted_iota(jnp.int32, (c, LANES), 0)
    lane = lax.broadcasted_iota(jnp.int32, (c, LANES), 1)
    colr = lane & (c - 1)
    member = lane >> lc
    eye = (row == colr).astype(F32)
    incl = row >= colr
    strict = row > colr
    diag8 = (row >> 3) == (colr >> 3)
    sq0 = lax.broadcasted_iota(jnp.int32, (LANES, LANES), 0)
    sq1 = lax.broadcasted_iota(jnp.int32, (LANES, LANES), 1)
    bd_mask = (sq0 >> lc) == (sq1 >> lc)
    kr0 = lax.broadcasted_iota(jnp.int32, (LANES, pk * LANES), 0)
    kr1 = lax.broadcasted_iota(jnp.int32, (LANES, pk * LANES), 1)
    k_mask = (kr0 >> lc) == (kr1 >> int(math.log2(LANES)))

    def bd(m):
        return jnp.where(bd_mask, jnp.concatenate([m] * pk, axis=0), 0.0)

    def by_member(vals):
        out = vals[0]
        for r in range(1, pk):
            out = jnp.where(member >= r, vals[r], out)
        return out

    def wide(col, hs):
        return jnp.concatenate([jnp.broadcast_to(col[h], (c, LANES)) for h in hs], axis=1)

    def hcol(r):
        return slice(r * LANES, (r + 1) * LANES)

    rows = slice(r0, r0 + c)
    g = gb_ref[bi, rows]
    gcum = g
    step = 1
    while step < c:
        gcum = gcum + jnp.where(row >= step, pltpu.roll(gcum, step, axis=0), 0.0)
        step *= 2
    gcum_t = jnp.concatenate([gcum] * pk, axis=0).T
    gc = [gcum[:, h:h + 1] for h in range(A_HEADS)]
    beta = [g[:, A_HEADS + h:A_HEADS + h + 1] for h in range(A_HEADS)]
    glast = [gcum[c - 1:c, h:h + 1] for h in range(A_HEADS)]
    eg = [jnp.exp(x) for x in gc]
    pcols = [slice(hs[0] * LANES, (hs[-1] + 1) * LANES) for hs in packs]
    kpf = [k_ref[bi, rows, pc].astype(F32) for pc in pcols]
    qp16 = [q_ref[bi, rows, pc] for pc in pcols]
    vpf = [v_ref[bi, rows, pc].astype(F32) for pc in pcols]
    kbeta = [kpf[p] * wide(beta, hs) for p, hs in enumerate(packs)]
    decay = [jnp.exp(jnp.where(incl, by_member([jnp.broadcast_to(gc[h], (c, LANES)) for h in hs])
                               - by_member([gcum_t[h:h + 1, :] for h in hs]), -jnp.inf))
             for hs in packs]
    yield
    k_bd =[jnp.where(k_mask, jnp.concatenate([x] * pk, axis=0), 0.0).astype(BF16) for x in kpf]
    kq = [_mm_nt(jnp.concatenate([kbeta[p].astype(BF16), qp16[p]], axis=0), k_bd[p])
          for p in npk]
    yield
    a = [jnp.where(strict, kq[p][:c] * decay[p], 0.0) for p in npk]
    d = [jnp.where(diag8, x, 0.0) for x in a]
    heads = [(p, r, hs[r]) for p, hs in enumerate(packs) for r in range(pk)]
    s = [s_ref[bi, h] for _, _, h in heads]
    ks = [_mm(jnp.concatenate([kbeta[p][:, hcol(r)] * eg[h], qp16[p][:, hcol(r)].astype(F32) * eg[h]], axis=0), s[i])
          for i, (p, r, h) in enumerate(heads)]
    yield
    d2 = [_mm(x, bd(x)) for x in d]
    yield
    d4 = [_mm(x, bd(x)) for x in d2]
    p1 = [_mm(eye - x, bd(eye + y)) for x, y in zip(d, d2)]
    yield
    xs = [_mm(p, bd(eye + y)) for p, y in zip(p1, d4)]
    yield
    shift = 3
    while (1 << shift) < c:
        mask = ((row >> (shift + 1)) == (colr >> (shift + 1))) & ((row >> shift) > (colr >> shift))
        xc = [_mm(x, bd(jnp.where(mask, m, 0.0))) for x, m in zip(xs, a)]
        yield
        xs = [x - _mm(y, bd(x)) for x, y in zip(xs, xc)]
        yield
        shift += 1
    vn = []
    for p, hs in enumerate(packs):
        vb = vpf[p] * wide(beta, hs)
        stacked = jnp.concatenate([vb[:, hcol(r)] - ks[p * pk + r][:c] for r in range(pk)], axis=0)
        vn.append(_mm(bd(xs[p]), stacked))
    yield
    v_new = [vn[p][r * c:(r + 1) * c] for p, r, _ in heads]
    intra = [kq[p][c:] * decay[p] for p in npk]
    kd_t = [(kpf[p][:, hcol(r)] * jnp.exp(glast[h] - gc[h])).T for p, r, h in heads]
    ov = [_mm(jnp.concatenate([intra[p][:, r * c:(r + 1) * c], kd_t[i]], axis=0), v_new[i])
          for i, (p, r, h) in enumerate(heads)]
    yield
    for i, (p, r, h) in enumerate(heads):
        s_ref[bi, h] = s[i] * jnp.exp(glast[h]) + ov[i][c:]
        o = ks[i][c:] + ov[i][:c]
        zf = z_ref[bi, rows, h * LANES:(h + 1) * LANES].astype(F32)
        gated = o * lax.rsqrt(jnp.mean(o * o, axis=-1, keepdims=True) + RMS_EPS) * nw * _silu(zf)
        o_ref[bi, rows, h * LANES:(h + 1) * LANES] = gated.astype(o_ref.dtype)


def _run_interleaved(gens):
    live = list(gens)
    while live:
        for gen in list(live):
            try:
                next(gen)
            except StopIteration:
                live.remove(gen)


def _delta_kernel(q_ref, k_ref, v_ref, z_ref, gb_ref, s0_ref, nw_ref, o_ref, s_ref, *, c, nb, nchunks):
    n = pl.program_id(1)

    @pl.when(n == 0)
    def _():
        s_ref[...] = s0_ref[...]

    nw = nw_ref[...]
    for ci in range(nchunks):
        _run_interleaved([_delta_group(bi, ci * c, q_ref, k_ref, v_ref, z_ref, gb_ref, nw, o_ref, s_ref, c)
                          for bi in range(nb)])


def _delta(q, k, v, z, gb, s0_all, norm_w, *, layer, batch, seq):
    c = min(CHUNK, seq)
    nchunks = min(DELTA_CHUNKS, seq // c)
    n = seq // (c * nchunks)
    nb = DELTA_BATCHES
    seq3 = lambda a: a.reshape(batch, seq, a.shape[-1])
    blk = lambda b, i: (b, i, 0)
    state = lambda b, i: (b, 0, 0, 0)
    o, s_new = pl.pallas_call(
        functools.partial(_delta_kernel, c=c, nb=nb, nchunks=nchunks),
        grid=(batch // nb, n),
        in_specs=[
            pl.BlockSpec((nb, nchunks * c, A_QK), blk),
            pl.BlockSpec((nb, nchunks * c, A_QK), blk),
            pl.BlockSpec((nb, nchunks * c, A_VW), blk),
            pl.BlockSpec((nb, nchunks * c, A_VW), blk),
            pl.BlockSpec((nb, nchunks * c, LANES), blk),
            pl.BlockSpec((None, nb, A_HEADS, A_DK, A_DV), lambda b, i: (layer, b, 0, 0, 0)),
            pl.BlockSpec((1, LANES), lambda b, i: (0, 0)),
        ],
        out_specs=[
            pl.BlockSpec((nb, nchunks * c, A_VW), blk),
            pl.BlockSpec((nb, A_HEADS, A_DK, A_DV), state),
        ],
        out_shape=[jax.ShapeDtypeStruct((batch, seq, A_VW), BF16),
                   jax.ShapeDtypeStruct((batch, A_HEADS, A_DK, A_DV), F32)],
        compiler_params=_params(("arbitrary", "arbitrary")),
        name="delta_rule",
    )(seq3(q), seq3(k), seq3(v), seq3(z), seq3(gb), s0_all, norm_w)
    return o.reshape(batch * seq, A_VW), s_new


def _out_ln_kernel(*refs, tm, sub):
    nw = D_MODEL // COL_BLOCK
    o_ref, x_ref, w_refs = refs[0], refs[1], refs[2:2 + nw]
    g_ref, b_ref, y_ref = refs[2 + nw:]
    for r0 in range(0, tm, sub):
        rows = slice(r0, r0 + sub)
        o = o_ref[rows, :]
        proj = jnp.concatenate([jnp.dot(o, w[...], preferred_element_type=F32) for w in w_refs], axis=1)
        r = DN_ALPHA * x_ref[rows, :] + proj
        mu = jnp.mean(r, axis=-1, keepdims=True)
        d = r - mu
        var = jnp.mean(d * d, axis=-1, keepdims=True)
        y_ref[rows, :] = d * lax.rsqrt(var + LN_EPS) * g_ref[...] + b_ref[...]


def _out_ln(o, x2, w_out16, ln_g, ln_b, *, layer):
    tokens = x2.shape[0]
    tm = min(OUT_TILE, tokens)
    sub = min(ROW_TILE, tm)
    row = lambda i: (i, 0)
    const = lambda i: (0, 0)
    nw = D_MODEL // COL_BLOCK
    return pl.pallas_call(
        functools.partial(_out_ln_kernel, tm=tm, sub=sub),
        grid=(tokens // tm,),
        in_specs=[pl.BlockSpec((tm, o.shape[1]), row), pl.BlockSpec((tm, D_MODEL), row)]
        + _weight_specs(layer, o.shape[1], nw, 1)
        + [pl.BlockSpec((1, D_MODEL), const), pl.BlockSpec((1, D_MODEL), const)],
        out_specs=pl.BlockSpec((tm, D_MODEL), row),
        out_shape=jax.ShapeDtypeStruct((tokens, D_MODEL), F32),
        compiler_params=_params(("arbitrary",)),
        name="out_ln",
    )(o, x2, *([w_out16] * nw), ln_g, ln_b)


def _rope_slab(x, cos_t, sin_up, sin_dn):
    half = ROPE_DIMS // 2
    return x * cos_t + pltpu.roll(x, LANES - half, axis=1) * sin_up + pltpu.roll(x, half, axis=1) * sin_dn


def _row_tiling(tokens, seq):
    tm = min(WIDE_TILE, tokens)
    if seq >= tm:
        per_seq = seq // tm
        return tm, 1, (lambda i: (i % per_seq, 0))
    return tm, tm // seq, (lambda i: (0, 0))


def _tile_tables(tables, reps):
    return tables if reps == 1 else tuple(jnp.tile(t, (reps, 1)) for t in tables)


def _kv_kernel(h_ref, w_ref, cos_ref, sup_ref, sdn_ref, k_ref, v_ref, klast_ref, vlast_ref, *, rows_per_seq, ns, keep):
    kv = jnp.dot(h_ref[...].astype(BF16), w_ref[...], preferred_element_type=F32)
    cos_t, sin_up, sin_dn = cos_ref[...], sup_ref[...], sdn_ref[...]
    k = jnp.concatenate([_rope_slab(kv[:, s * LANES:(s + 1) * LANES], cos_t, sin_up, sin_dn)
                         for s in range(B_KVW // LANES)], axis=1)
    v = kv[:, B_KVW:]
    k_ref[...] = k.astype(k_ref.dtype)
    v_ref[...] = v.astype(v_ref.dtype)
    for s in range(ns):
        end = (s + 1) * rows_per_seq
        klast_ref[s] = k[end - keep:end, :]
        vlast_ref[s] = v[end - keep:end, :]


def _shared_kv(h2, w_kv16, tables, *, batch, seq):
    tokens = batch * seq
    tm, ns, tab = _row_tiling(tokens, seq)
    keep = min(WINDOW, seq)
    rows_per_seq = tm // ns
    assert keep <= rows_per_seq
    per_seq = max(1, seq // tm)
    row = lambda i: (i, 0)
    lastb = lambda i: (i // per_seq, 0, 0)
    out = jax.ShapeDtypeStruct((tokens, B_KVW), BF16)
    last = jax.ShapeDtypeStruct((batch, keep, B_KVW), F32)
    return pl.pallas_call(
        functools.partial(_kv_kernel, rows_per_seq=rows_per_seq, ns=ns, keep=keep),
        grid=(tokens // tm,),
        in_specs=[
            pl.BlockSpec((tm, D_MODEL), row),
            pl.BlockSpec((D_MODEL, 2 * B_KVW), lambda i: (0, 0)),
            pl.BlockSpec((tm, LANES), tab),
            pl.BlockSpec((tm, LANES), tab),
            pl.BlockSpec((tm, LANES), tab),
        ],
        out_specs=[pl.BlockSpec((tm, B_KVW), row), pl.BlockSpec((tm, B_KVW), row),
                   pl.BlockSpec((ns, keep, B_KVW), lastb), pl.BlockSpec((ns, keep, B_KVW), lastb)],
        out_shape=[out, out, last, last],
        compiler_params=_params(("arbitrary",)),
        name="shared_kv",
    )(h2, w_kv16, *_tile_tables(tables, ns))


def _b_in_kernel(*refs, tm):
    nw = 2 * B_QW // COL_BLOCK
    x_ref, w_refs = refs[0], refs[1:1 + nw]
    cos_ref, sup_ref, sdn_ref, qx_ref, z_ref = refs[1 + nw:]
    xb = x_ref[...].astype(BF16)
    cos_t, sin_up, sin_dn = cos_ref[...], sup_ref[...], sdn_ref[...]
    lane_half = lax.broadcasted_iota(jnp.int32, (tm, LANES), 1) >> int(math.log2(B_HD))
    cb = COL_BLOCK
    nq = B_QW // cb
    for j in range(nq):
        proj = jnp.dot(xb, w_refs[j][...], preferred_element_type=F32)
        for sl in range(cb // LANES):
            slab = j * (cb // LANES) + sl
            rot = _rope_slab(proj[:, sl * LANES:(sl + 1) * LANES], cos_t, sin_up, sin_dn) * (B_HD ** -0.5 * LOG2E)
            for p in range(2):
                hq = 2 * slab + p
                x = jnp.where(lane_half == p, rot, 0.0)
                if p != (hq // B_GROUP) % 2:
                    x = pltpu.roll(x, B_HD, axis=1)
                qx_ref[:, hq * LANES:(hq + 1) * LANES] = x.astype(qx_ref.dtype)
    for j in range(nq):
        z_ref[:, j * cb:(j + 1) * cb] = jnp.dot(
            xb, w_refs[nq + j][...], preferred_element_type=F32).astype(z_ref.dtype)


def _b_in(x2, w_in16, tables, *, layer, batch, seq):
    tokens = batch * seq
    tm, ns, tab = _row_tiling(tokens, seq)
    row = lambda i: (i, 0)
    nw = 2 * B_QW // COL_BLOCK
    return pl.pallas_call(
        functools.partial(_b_in_kernel, tm=tm),
        grid=(tokens // tm,),
        in_specs=[pl.BlockSpec((tm, D_MODEL), row)] + _weight_specs(layer, D_MODEL, nw, 1) + [
            pl.BlockSpec((tm, LANES), tab),
            pl.BlockSpec((tm, LANES), tab),
            pl.BlockSpec((tm, LANES), tab),
        ],
        out_specs=[pl.BlockSpec((tm, B_QX), row), pl.BlockSpec((tm, B_QW), row)],
        out_shape=[jax.ShapeDtypeStruct((tokens, B_QX), BF16), jax.ShapeDtypeStruct((tokens, B_QW), BF16)],
        compiler_params=_params(("arbitrary",)),
        name="b_in",
    )(x2, *([w_in16] * nw), *_tile_tables(tables, ns))


def _attn_body(sink_ref, qx_ref, z_ref, k_ref, v_ref, o_ref, *, lq, lk, nc, masked):
    step = pl.program_id(1)
    lane_half = lax.broadcasted_iota(jnp.int32, (lq, LANES), 1) >> int(math.log2(B_HD))
    kv_slabs = range(B_KVW // LANES)
    kwin, vwin, valid = [], [], []
    for ci in range(nc):
        if masked is None:
            start, seq_i = 0, ci
            valid.append(None)
        else:
            seq_i = 0
            c = step * nc + ci
            first = jnp.maximum(c - WINDOW_CHUNKS, 0)
            start = pl.multiple_of(first * CHUNK, CHUNK)
            key_chunk = first + (lax.broadcasted_iota(jnp.int32, (lq, lk), 1) >> int(math.log2(CHUNK)))
            valid.append((key_chunk <= c) if masked else None)
        kwin.append([k_ref[seq_i, pl.ds(start, lk), s * LANES:(s + 1) * LANES] for s in kv_slabs])
        vwin.append([v_ref[seq_i, pl.ds(start, lk), s * LANES:(s + 1) * LANES] for s in kv_slabs])
    units = [(ci, j) for ci in range(nc) for j in range(B_KV_HEADS)]
    qstack = [jnp.concatenate([qx_ref[ci * lq:(ci + 1) * lq, (B_GROUP * j + g) * LANES:(B_GROUP * j + g + 1) * LANES]
                               for g in range(B_GROUP)], axis=0) for ci, j in units]
    scores = [_mm_nt(qstack[u], kwin[ci][j // 2]) for u, (ci, j) in enumerate(units)]
    heads = [(u, ci, j, g) for u, (ci, j) in enumerate(units) for g in range(B_GROUP)]
    sk = [sink_ref[B_GROUP * j + g] * LOG2E for _, _, j, g in heads]
    sc = [scores[u][g * lq:(g + 1) * lq] for u, _, _, g in heads]
    sc = [s if valid[ci] is None else jnp.where(valid[ci], s, -jnp.inf) for s, (_, ci, _, _) in zip(sc, heads)]
    mx = [jnp.maximum(jnp.max(s, axis=-1, keepdims=True), k) for s, k in zip(sc, sk)]
    pstack = [jnp.concatenate([jnp.exp2(sc[h] - mx[h]).astype(BF16) for h in range(u * B_GROUP, (u + 1) * B_GROUP)],
                              axis=0) for u in range(len(units))]
    ones = jnp.ones((lk, LANES), BF16)
    pv = [_mm(pstack[u], jnp.concatenate([vwin[ci][j // 2], ones], axis=1))
          for u, (ci, j) in enumerate(units)]
    out = []
    for h, (u, _, _, g) in enumerate(heads):
        part = pv[u][g * lq:(g + 1) * lq]
        out.append(part[:, :LANES] / (part[:, LANES:] + jnp.exp2(sk[h] - mx[h])))
    for ci in range(nc):
        rows = slice(ci * lq, (ci + 1) * lq)
        for slab in range(B_QW // LANES):
            pair, g = divmod(slab, B_GROUP)
            lo = out[(ci * B_KV_HEADS + 2 * pair) * B_GROUP + g]
            hi = out[(ci * B_KV_HEADS + 2 * pair + 1) * B_GROUP + g]
            both = jnp.where(lane_half == 0, lo, hi)
            zs = z_ref[rows, slab * LANES:(slab + 1) * LANES].astype(F32)
            o_ref[rows, slab * LANES:(slab + 1) * LANES] = (both * _silu(zs)).astype(o_ref.dtype)


def _attn_kernel(sink_ref, qx_ref, z_ref, k_ref, v_ref, o_ref, *, lq, lk, nc, banded):
    body = functools.partial(_attn_body, sink_ref, qx_ref, z_ref, k_ref, v_ref, o_ref, lq=lq, lk=lk, nc=nc)
    if not banded:
        body(masked=None)
        return
    step = pl.program_id(1)
    clamped_steps = -(-WINDOW_CHUNKS // nc)

    @pl.when(step < clamped_steps)
    def _():
        body(masked=True)

    @pl.when(step >= clamped_steps)
    def _():
        body(masked=False)


def _attention(qx, z, k3, v3, sinks, *, batch, seq, banded):
    lq = min(CHUNK, seq)
    ltot = k3.shape[1]
    if banded:
        nc, nseq = min(ATTN_CHUNKS, seq // lq), 1
        lk = (WINDOW_CHUNKS + 1) * CHUNK
    else:
        assert seq == lq
        nc = nseq = min(ATTN_CHUNKS, batch)
        lk = ltot
    nq = seq * nseq // (lq * nc)
    tokens = batch * seq
    row = lambda b, c: (b * nq + c, 0)
    whole = lambda b, c: (b, 0, 0)
    return pl.pallas_call(
        functools.partial(_attn_kernel, lq=lq, lk=lk, nc=nc, banded=banded),
        grid=(batch // nseq, nq),
        in_specs=[
            pl.BlockSpec(memory_space=pltpu.SMEM),
            pl.BlockSpec((nc * lq, B_QX), row),
            pl.BlockSpec((nc * lq, B_QW), row),
            pl.BlockSpec((nseq, ltot, B_KVW), whole),
            pl.BlockSpec((nseq, ltot, B_KVW), whole),
        ],
        out_specs=pl.BlockSpec((nc * lq, B_QW), row),
        out_shape=jax.ShapeDtypeStruct((tokens, B_QW), BF16),
        compiler_params=_params(("arbitrary", "arbitrary")),
        name="swa_attention",
    )(sinks, qx, z, k3, v3)


def _rope_tables(pos):
    half = ROPE_DIMS // 2
    inv = ROPE_THETA ** (-jnp.arange(half, dtype=F32) * 2.0 / ROPE_DIMS)
    ang = pos.astype(F32)[:, None] * inv[None, :]
    cos, sin = jnp.cos(ang), jnp.sin(ang)
    ones = jnp.ones((pos.shape[0], B_HD - ROPE_DIMS), F32)
    zeros_h = jnp.zeros((pos.shape[0], half), F32)
    zeros_r = jnp.zeros((pos.shape[0], B_HD - ROPE_DIMS), F32)
    cos_head = jnp.concatenate([cos, cos, ones], axis=1)
    up_head = jnp.concatenate([-sin, zeros_h, zeros_r], axis=1)
    dn_head = jnp.concatenate([zeros_h, sin, zeros_r], axis=1)
    rep = LANES // B_HD
    return tuple(jnp.tile(t, (1, rep)) for t in (cos_head, up_head, dn_head))


def _trunk(x, pos, conv_state, delta_state, past_k, past_v, wts):
    batch, seq, _ = x.shape
    tokens = batch * seq
    h = x.reshape(tokens, D_MODEL)
    tables = _rope_tables(pos)
    new_conv, new_delta = [], []
    for i in range(N_A_LAYERS):
        q, k, v, z, gb, cbuf = _a_in(h, wts["a_w_in"], wts["a_w_gate"][i], wts["a_conv_w"][i], conv_state[i],
                                     wts["a_log"][i], wts["a_dt"][i], layer=i, batch=batch, seq=seq)
        o, s_new = _delta(q, k, v, z, gb, delta_state, wts["a_norm_w"][i], layer=i, batch=batch, seq=seq)
        h = _out_ln(o, h, wts["a_w_out"], wts["a_ln_g"][i], wts["a_ln_b"][i], layer=i)
        new_conv.append(cbuf)
        new_delta.append(s_new)
    k2, v2, k_last, v_last = _shared_kv(h, wts["b_w_kv"], tables, batch=batch, seq=seq)
    k3 = k2.reshape(batch, seq, B_KVW)
    v3 = v2.reshape(batch, seq, B_KVW)
    cached = past_k is not None
    if cached:
        pk = past_k.reshape(batch, -1, B_KVW)
        pv = past_v.reshape(batch, -1, B_KVW)
        k3 = jnp.concatenate([pk.astype(BF16), k3], axis=1)
        v3 = jnp.concatenate([pv.astype(BF16), v3], axis=1)
        k_last = jnp.concatenate([pk, k_last], axis=1)[:, -WINDOW:]
        v_last = jnp.concatenate([pv, v_last], axis=1)[:, -WINDOW:]
    new_k = k_last.reshape(batch, WINDOW, B_KV_HEADS, B_HD)
    new_v = v_last.reshape(batch, WINDOW, B_KV_HEADS, B_HD)
    for j in range(N_B_LAYERS):
        qx, z = _b_in(h, wts["b_w_in"], tables, layer=j, batch=batch, seq=seq)
        o = _attention(qx, z, k3, v3, wts["b_sinks"][j], batch=batch, seq=seq, banded=not cached)
        h = _out_ln(o, h, wts["b_w_out"], wts["b_ln_g"][j], wts["b_ln_b"][j], layer=j)
    return h.reshape(batch, seq, D_MODEL), jnp.stack(new_conv), jnp.stack(new_delta), new_k, new_v


def _paired_heads(w, axis):
    order = [B_GROUP * (2 * pair + odd) + g
             for pair in range(B_KV_HEADS // 2) for g in range(B_GROUP) for odd in range(2)]
    shape = w.shape
    blocks = w.reshape(shape[:axis] + (B_Q_HEADS, B_HD) + shape[axis + 1:])
    return jnp.take(blocks, jnp.array(order, jnp.int32), axis=axis).reshape(shape)


def kernel(x_prompt, x_sample, state_delta, state_conv, cache_k, cache_v, a_w_in, a_conv_w, a_log, a_dt_bias,
           a_norm_w, a_w_out, a_ln_g, a_ln_b, b_w_kv, b_w_in, b_sinks, b_w_out, b_ln_g, b_ln_b):
    a_w_in16 = a_w_in.astype(BF16)
    zeros_h = jnp.zeros((N_A_LAYERS, A_HEADS), F32)
    wts = {
        "a_w_in": a_w_in16,
        "a_w_gate": jnp.pad(a_w_in16[:, :, A_MAIN:], ((0, 0), (0, 0), (0, LANES - 2 * A_HEADS))),
        "a_conv_w": a_conv_w,
        "a_log": jnp.concatenate([a_log.astype(F32), zeros_h], axis=1)[:, :, None],
        "a_dt": jnp.concatenate([a_dt_bias.astype(F32), zeros_h], axis=1)[:, :, None],
        "a_norm_w": a_norm_w.reshape(N_A_LAYERS, 1, A_DV),
        "a_w_out": a_w_out.astype(BF16),
        "a_ln_g": a_ln_g.reshape(N_A_LAYERS, 1, D_MODEL),
        "a_ln_b": a_ln_b.reshape(N_A_LAYERS, 1, D_MODEL),
        "b_w_kv": b_w_kv.astype(BF16),
        "b_w_in": jnp.concatenate([b_w_in[:, :, :B_QW], _paired_heads(b_w_in[:, :, B_QW:], axis=2)],
                                  axis=2).astype(BF16),
        "b_sinks": b_sinks,
        "b_w_out": _paired_heads(b_w_out, axis=1).astype(BF16),
        "b_ln_g": b_ln_g.reshape(N_B_LAYERS, 1, D_MODEL),
        "b_ln_b": b_ln_b.reshape(N_B_LAYERS, 1, D_MODEL),
    }
    bp, lp, _ = x_prompt.shape
    bs, ls, _ = x_sample.shape
    pos_prompt = jnp.arange(lp, dtype=jnp.int32)
    pos_sample = PAST_LEN + jnp.arange(ls, dtype=jnp.int32)
    zero_conv = jnp.zeros((N_A_LAYERS, bp, CONV_W - 1, CONV_DIM), F32)
    zero_delta = jnp.zeros((N_A_LAYERS, bp, A_HEADS, A_DK, A_DV), F32)
    y_p, p_conv, p_delta, p_k, p_v = _trunk(x_prompt, pos_prompt, zero_conv, zero_delta, None, None, wts)
    y_s, s_conv, s_delta, s_k, s_v = _trunk(x_sample, pos_sample, state_conv, state_delta, cache_k, cache_v, wts)
    return (y_p, y_s, p_delta, p_conv, p_k, p_v, s_delta, s_conv, s_k, s_v)
```

```python
import functools
import math

import jax
import jax.numpy as jnp
from jax import lax
from jax.experimental import pallas as pl
from jax.experimental.pallas import tpu as pltpu

D_MODEL = 1024
DEPTH = 4
PAST_LEN = 4096
CHUNK = 64
N_A_LAYERS = DEPTH // 2
N_B_LAYERS = DEPTH - N_A_LAYERS
A_HEADS = 8
A_DK = 128
A_DV = 128
A_QK = A_HEADS * A_DK
A_VW = A_HEADS * A_DV
CONV_W = 4
CONV_DIM = 2 * A_QK + A_VW
A_MAIN = CONV_DIM + A_VW
B_Q_HEADS = 16
B_KV_HEADS = 4
B_GROUP = B_Q_HEADS // B_KV_HEADS
B_HD = 64
B_QW = B_Q_HEADS * B_HD
B_KVW = B_KV_HEADS * B_HD
WINDOW = 128
WINDOW_CHUNKS = WINDOW // CHUNK
ROPE_DIMS = B_HD // 4
ROPE_THETA = 500000.0
DN_ALPHA = (2 * DEPTH) ** 0.25
LN_EPS = 1e-5
RMS_EPS = 1e-6

LANES = 128
SUBLANES = 8
VMEM_LIMIT = 48 * 1024 * 1024
ROW_TILE = 256
WIDE_TILE = 512
OUT_TILE = 1024
CONV_PIECE = 512
COL_BLOCK = 512
DELTA_BATCHES = 2
DELTA_CHUNKS = 8
ATTN_CHUNKS = 8
B_QX = B_Q_HEADS * LANES
LOG2E = math.log2(math.e)

F32 = jnp.float32
BF16 = jnp.bfloat16


def _mm(a, b):
    return jnp.dot(a.astype(BF16), b.astype(BF16), preferred_element_type=F32)


def _mm_nt(a, b):
    return lax.dot_general(a.astype(BF16), b.astype(BF16), (((1,), (1,)), ((), ())),
                           preferred_element_type=F32)


def _silu(x):
    h = 0.5 * x
    return h + h * jnp.tanh(h)


def _params(semantics):
    return pltpu.CompilerParams(dimension_semantics=semantics, vmem_limit_bytes=VMEM_LIMIT)


def _weight_specs(layer, kdim, nblocks, grid_rank):
    def spec(j):
        if grid_rank == 1:
            return pl.BlockSpec((None, kdim, COL_BLOCK), lambda i: (layer, 0, j))
        return pl.BlockSpec((None, kdim, COL_BLOCK), lambda b, l: (layer, 0, j))
    return [spec(j) for j in range(nblocks)]


def _a_in_kernel(*refs, ts, ns):
    nw = A_MAIN // COL_BLOCK
    x_ref, w_refs = refs[0], refs[1:1 + nw]
    (wg_ref, cw_ref, c0_ref, alog_ref, dt_ref,
     q_ref, k_ref, v_ref, z_ref, gb_ref, cout_ref, pbuf, carry) = refs[1 + nw:]
    tm = ts * ns
    l = pl.program_id(1)
    tail = SUBLANES - (CONV_W - 1)

    @pl.when(l == 0)
    def _():
        for s in range(ns):
            carry[s, tail:SUBLANES, :] = c0_ref[s]

    xb = x_ref[...].astype(BF16)
    cb = COL_BLOCK
    nconv = CONV_DIM // cb
    outs = (q_ref, k_ref, v_ref)

    def conv_block(s, blk):
        rows = slice(s * ts, (s + 1) * ts)
        cols = slice(blk * LANES, (blk + 1) * LANES)
        ext = jnp.concatenate([carry[s, :, cols], pbuf[rows, cols]], axis=0)
        y = None
        for j in range(CONV_W):
            back = CONV_W - 1 - j
            tap = (pltpu.roll(ext, back, axis=0) if back else ext)[SUBLANES:] * cw_ref[j:j + 1, cols]
            y = tap if y is None else y + tap
        y = _silu(y)
        which, head = divmod(blk, A_HEADS)
        if which < 2:
            y = y * lax.rsqrt(jnp.sum(y * y, axis=-1, keepdims=True) + RMS_EPS)
            if which == 0:
                y = y * (A_DK ** -0.5)
        outs[which][rows, head * LANES:(head + 1) * LANES] = y.astype(outs[which].dtype)

    pc = CONV_PIECE
    npieces = CONV_DIM // pc

    def project(i):
        j, off = divmod(i * pc, cb)
        pbuf[:, i * pc:(i + 1) * pc] = jnp.dot(xb, w_refs[j][:, off:off + pc], preferred_element_type=F32)

    def gate_z(j):
        z_ref[:, j * cb:(j + 1) * cb] = jnp.dot(
            xb, w_refs[nconv + j][...], preferred_element_type=F32).astype(z_ref.dtype)

    def decay_beta():
        gates = jnp.dot(xb, wg_ref[...], preferred_element_type=F32)
        gt = gates.T[:2 * A_HEADS]
        rowi = lax.broadcasted_iota(jnp.int32, gt.shape, 0)
        sp_in = gt + dt_ref[...]
        softplus = jnp.maximum(sp_in, 0.0) + jnp.log1p(jnp.exp(-jnp.abs(sp_in)))
        gval = -jnp.exp(alog_ref[...]) * softplus
        bval = 1.0 / (1.0 + jnp.exp(-gt))
        res = jnp.where(rowi < A_HEADS, gval, bval)
        gb_ref[...] = jnp.concatenate([res, jnp.zeros((LANES - 2 * A_HEADS, tm), F32)], axis=0).T

    light = [functools.partial(gate_z, j) for j in range(A_VW // cb)] + [decay_beta]
    project(0)
    for i in range(npieces):
        if i + 1 < npieces:
            project(i + 1)
        if light and i % (npieces // 4) == 0:
            light.pop(0)()
        for s in range(ns):
            for blk in range(i * pc // LANES, (i + 1) * pc // LANES):
                conv_block(s, blk)
    assert not light

    for s in range(ns):
        last = pbuf[(s + 1) * ts - (CONV_W - 1):(s + 1) * ts, :]
        cout_ref[s] = last
        carry[s, tail:SUBLANES, :] = last


def _a_in(x2, w_in16, w_gate, conv_w, conv0, alog_col, dt_col, *, layer, batch, seq):
    if seq >= WIDE_TILE:
        ts, ns = WIDE_TILE, 1
    else:
        ts, ns = seq, min(batch, WIDE_TILE // seq)
    tm = ts * ns
    nl = seq // ts
    tokens = batch * seq
    row = lambda b, l: (b * nl + l, 0)
    const = lambda b, l: (0, 0)
    perseq = lambda b, l: (b, 0, 0)
    wide = jax.ShapeDtypeStruct((tokens, A_QK), BF16)
    nw = A_MAIN // COL_BLOCK
    return pl.pallas_call(
        functools.partial(_a_in_kernel, ts=ts, ns=ns),
        grid=(batch // ns, nl),
        in_specs=[pl.BlockSpec((tm, D_MODEL), row)] + _weight_specs(layer, D_MODEL, nw, 2) + [
            pl.BlockSpec((D_MODEL, LANES), const),
            pl.BlockSpec((CONV_W, CONV_DIM), const),
            pl.BlockSpec((ns, CONV_W - 1, CONV_DIM), perseq),
            pl.BlockSpec((2 * A_HEADS, 1), const),
            pl.BlockSpec((2 * A_HEADS, 1), const),
        ],
        out_specs=[
            pl.BlockSpec((tm, A_QK), row),
            pl.BlockSpec((tm, A_QK), row),
            pl.BlockSpec((tm, A_VW), row),
            pl.BlockSpec((tm, A_VW), row),
            pl.BlockSpec((tm, LANES), row),
            pl.BlockSpec((ns, CONV_W - 1, CONV_DIM), perseq),
        ],
        out_shape=[wide, wide, wide, wide,
                   jax.ShapeDtypeStruct((tokens, LANES), F32),
                   jax.ShapeDtypeStruct((batch, CONV_W - 1, CONV_DIM), F32)],
        scratch_shapes=[pltpu.VMEM((tm, CONV_DIM), F32), pltpu.VMEM((ns, SUBLANES, CONV_DIM), F32)],
        compiler_params=_params(("arbitrary", "arbitrary")),
        name="a_in",
    )(x2, *([w_in16] * nw), w_gate, conv_w, conv0, alog_col, dt_col)


def _delta_group(bi, r0, q_ref, k_ref, v_ref, z_ref, gb_ref, nw, o_ref, s_ref, c):
    pk = LANES // c
    lc = int(math.log2(c))
    packs = [list(range(p * pk, (p + 1) * pk)) for p in range(A_HEADS // pk)]
    npk = range(len(packs))
    row = lax.broadcasted_iota(jnp.int32, (c, LANES), 0)
    lane = lax.broadcasted_iota(jnp.int32, (c, LANES), 1)
    colr = lane & (c - 1)
    member = lane >> lc
    eye = (row == colr).astype(F32)
    incl = row >= colr
    strict = row > colr
    diag8 = (row >> 3) == (colr >> 3)
    sq0 = lax.broadcasted_iota(jnp.int32, (LANES, LANES), 0)
    sq1 = lax.broadcasted_iota(jnp.int32, (LANES, LANES), 1)
    bd_mask = (sq0 >> lc) == (sq1 >> lc)
    kr0 = lax.broadcasted_iota(jnp.int32, (LANES, pk * LANES), 0)
    kr1 = lax.broadcasted_iota(jnp.int32, (LANES, pk * LANES), 1)
    k_mask = (kr0 >> lc) == (kr1 >> int(math.log2(LANES)))

    def bd(m):
        return jnp.where(bd_mask, jnp.concatenate([m] * pk, axis=0), 0.0)

    def by_member(vals):
        out = vals[0]
        for r in range(1, pk):
            out = jnp.where(member >= r, vals[r], out)
        return out

    def wide(col, hs):
        return jnp.concatenate([jnp.broadcast_to(col[h], (c, LANES)) for h in hs], axis=1)

    def hcol(r):
        return slice(r * LANES, (r + 1) * LANES)

    rows = slice(r0, r0 + c)
    g = gb_ref[bi, rows]
    gcum = g
    step = 1
    while step < c:
        gcum = gcum + jnp.where(row >= step, pltpu.roll(gcum, step, axis=0), 0.0)
        step *= 2
    gcum_t = jnp.concatenate([gcum] * pk, axis=0).T
    gc = [gcum[:, h:h + 1] for h in range(A_HEADS)]
    beta = [g[:, A_HEADS + h:A_HEADS + h + 1] for h in range(A_HEADS)]
    glast = [gcum[c - 1:c, h:h + 1] for h in range(A_HEADS)]
    eg = [jnp.exp(x) for x in gc]
    pcols = [slice(hs[0] * LANES, (hs[-1] + 1) * LANES) for hs in packs]
    kpf = [k_ref[bi, rows, pc].astype(F32) for pc in pcols]
    qp16 = [q_ref[bi, rows, pc] for pc in pcols]
    vpf = [v_ref[bi, rows, pc].astype(F32) for pc in pcols]
    kbeta = [kpf[p] * wide(beta, hs) for p, hs in enumerate(packs)]
    decay = [jnp.exp(jnp.where(incl, by_member([jnp.broadcast_to(gc[h], (c, LANES)) for h in hs])
                               - by_member([gcum_t[h:h + 1, :] for h in hs]), -jnp.inf))
             for hs in packs]
    yield
    k_bd =[jnp.where(k_mask, jnp.concatenate([x] * pk, axis=0), 0.0).astype(BF16) for x in kpf]
    kq = [_mm_nt(jnp.concatenate([kbeta[p].astype(BF16), qp16[p]], axis=0), k_bd[p])
          for p in npk]
    yield
    a = [jnp.where(strict, kq[p][:c] * decay[p], 0.0) for p in npk]
    d = [jnp.where(diag8, x, 0.0) for x in a]
    heads = [(p, r, hs[r]) for p, hs in enumerate(packs) for r in range(pk)]
    s = [s_ref[bi, h] for _, _, h in heads]
    ks = [_mm(jnp.concatenate([kbeta[p][:, hcol(r)] * eg[h], qp16[p][:, hcol(r)].astype(F32) * eg[h]], axis=0), s[i])
          for i, (p, r, h) in enumerate(heads)]
    yield
    d2 = [_mm(x, bd(x)) for x in d]
    yield
    d4 = [_mm(x, bd(x)) for x in d2]
    p1 = [_mm(eye - x, bd(eye + y)) for x, y in zip(d, d2)]
    yield
    xs = [_mm(p, bd(eye + y)) for p, y in zip(p1, d4)]
    yield
    shift = 3
    while (1 << shift) < c:
        mask = ((row >> (shift + 1)) == (colr >> (shift + 1))) & ((row >> shift) > (colr >> shift))
        xc = [_mm(x, bd(jnp.where(mask, m, 0.0))) for x, m in zip(xs, a)]
        yield
        xs = [x - _mm(y, bd(x)) for x, y in zip(xs, xc)]
        yield
        shift += 1
    vn = []
    for p, hs in enumerate(packs):
        vb = vpf[p] * wide(beta, hs)
        stacked = jnp.concatenate([vb[:, hcol(r)] - ks[p * pk + r][:c] for r in range(pk)], axis=0)
        vn.append(_mm(bd(xs[p]), stacked))
    yield
    v_new = [vn[p][r * c:(r + 1) * c] for p, r, _ in heads]
    intra = [kq[p][c:] * decay[p] for p in npk]
    kd_t = [(kpf[p][:, hcol(r)] * jnp.exp(glast[h] - gc[h])).T for p, r, h in heads]
    ov = [_mm(jnp.concatenate([intra[p][:, r * c:(r + 1) * c], kd_t[i]], axis=0), v_new[i])
          for i, (p, r, h) in enumerate(heads)]
    yield
    for i, (p, r, h) in enumerate(heads):
        s_ref[bi, h] = s[i] * jnp.exp(glast[h]) + ov[i][c:]
        o = ks[i][c:] + ov[i][:c]
        zf = z_ref[bi, rows, h * LANES:(h + 1) * LANES].astype(F32)
        gated = o * lax.rsqrt(jnp.mean(o * o, axis=-1, keepdims=True) + RMS_EPS) * nw * _silu(zf)
        o_ref[bi, rows, h * LANES:(h + 1) * LANES] = gated.astype(o_ref.dtype)


def _run_interleaved(gens):
    live = list(gens)
    while live:
        for gen in list(live):
            try:
                next(gen)
            except StopIteration:
                live.remove(gen)


def _delta_kernel(q_ref, k_ref, v_ref, z_ref, gb_ref, s0_ref, nw_ref, o_ref, s_ref, *, c, nb, nchunks):
    n = pl.program_id(1)

    @pl.when(n == 0)
    def _():
        s_ref[...] = s0_ref[...]

    nw = nw_ref[...]
    for ci in range(nchunks):
        _run_interleaved([_delta_group(bi, ci * c, q_ref, k_ref, v_ref, z_ref, gb_ref, nw, o_ref, s_ref, c)
                          for bi in range(nb)])


def _delta(q, k, v, z, gb, s0_all, norm_w, *, layer, batch, seq):
    c = min(CHUNK, seq)
    nchunks = min(DELTA_CHUNKS, seq // c)
    n = seq // (c * nchunks)
    nb = DELTA_BATCHES
    seq3 = lambda a: a.reshape(batch, seq, a.shape[-1])
    blk = lambda b, i: (b, i, 0)
    state = lambda b, i: (b, 0, 0, 0)
    o, s_new = pl.pallas_call(
        functools.partial(_delta_kernel, c=c, nb=nb, nchunks=nchunks),
        grid=(batch // nb, n),
        in_specs=[
            pl.BlockSpec((nb, nchunks * c, A_QK), blk),
            pl.BlockSpec((nb, nchunks * c, A_QK), blk),
            pl.BlockSpec((nb, nchunks * c, A_VW), blk),
            pl.BlockSpec((nb, nchunks * c, A_VW), blk),
            pl.BlockSpec((nb, nchunks * c, LANES), blk),
            pl.BlockSpec((None, nb, A_HEADS, A_DK, A_DV), lambda b, i: (layer, b, 0, 0, 0)),
            pl.BlockSpec((1, LANES), lambda b, i: (0, 0)),
        ],
        out_specs=[
            pl.BlockSpec((nb, nchunks * c, A_VW), blk),
            pl.BlockSpec((nb, A_HEADS, A_DK, A_DV), state),
        ],
        out_shape=[jax.ShapeDtypeStruct((batch, seq, A_VW), BF16),
                   jax.ShapeDtypeStruct((batch, A_HEADS, A_DK, A_DV), F32)],
        compiler_params=_params(("arbitrary", "arbitrary")),
        name="delta_rule",
    )(seq3(q), seq3(k), seq3(v), seq3(z), seq3(gb), s0_all, norm_w)
    return o.reshape(batch * seq, A_VW), s_new


def _out_ln_kernel(*refs, tm, sub):
    nw = D_MODEL // COL_BLOCK
    o_ref, x_ref, w_refs = refs[0], refs[1], refs[2:2 + nw]
    g_ref, b_ref, y_ref = refs[2 + nw:]
    for r0 in range(0, tm, sub):
        rows = slice(r0, r0 + sub)
        o = o_ref[rows, :]
        proj = jnp.concatenate([jnp.dot(o, w[...], preferred_element_type=F32) for w in w_refs], axis=1)
        r = DN_ALPHA * x_ref[rows, :] + proj
        mu = jnp.mean(r, axis=-1, keepdims=True)
        d = r - mu
        var = jnp.mean(d * d, axis=-1, keepdims=True)
        y_ref[rows, :] = d * lax.rsqrt(var + LN_EPS) * g_ref[...] + b_ref[...]


def _out_ln(o, x2, w_out16, ln_g, ln_b, *, layer):
    tokens = x2.shape[0]
    tm = min(OUT_TILE, tokens)
    sub = min(ROW_TILE, tm)
    row = lambda i: (i, 0)
    const = lambda i: (0, 0)
    nw = D_MODEL // COL_BLOCK
    return pl.pallas_call(
        functools.partial(_out_ln_kernel, tm=tm, sub=sub),
        grid=(tokens // tm,),
        in_specs=[pl.BlockSpec((tm, o.shape[1]), row), pl.BlockSpec((tm, D_MODEL), row)]
        + _weight_specs(layer, o.shape[1], nw, 1)
        + [pl.BlockSpec((1, D_MODEL), const), pl.BlockSpec((1, D_MODEL), const)],
        out_specs=pl.BlockSpec((tm, D_MODEL), row),
        out_shape=jax.ShapeDtypeStruct((tokens, D_MODEL), F32),
        compiler_params=_params(("arbitrary",)),
        name="out_ln",
    )(o, x2, *([w_out16] * nw), ln_g, ln_b)


def _rope_slab(x, cos_t, sin_up, sin_dn):
    half = ROPE_DIMS // 2
    return x * cos_t + pltpu.roll(x, LANES - half, axis=1) * sin_up + pltpu.roll(x, half, axis=1) * sin_dn


def _row_tiling(tokens, seq):
    tm = min(WIDE_TILE, tokens)
    if seq >= tm:
        per_seq = seq // tm
        return tm, 1, (lambda i: (i % per_seq, 0))
    return tm, tm // seq, (lambda i: (0, 0))


def _tile_tables(tables, reps):
    return tables if reps == 1 else tuple(jnp.tile(t, (reps, 1)) for t in tables)


def _b_in_kernel(*refs, tm, kv):
    nw = 2 * B_QW // COL_BLOCK
    x_ref, w_refs = refs[0], refs[1:1 + nw]
    if kv is None:
        cos_ref, sup_ref, sdn_ref, qx_ref, z_ref = refs[1 + nw:]
    else:
        wkv_ref, cos_ref, sup_ref, sdn_ref, qx_ref, z_ref, k_ref, v_ref, klast_ref, vlast_ref = refs[1 + nw:]
    xb = x_ref[...].astype(BF16)
    cos_t, sin_up, sin_dn = cos_ref[...], sup_ref[...], sdn_ref[...]
    lane_half = lax.broadcasted_iota(jnp.int32, (tm, LANES), 1) >> int(math.log2(B_HD))
    cb = COL_BLOCK
    nq = B_QW // cb
    for j in range(nq):
        proj = jnp.dot(xb, w_refs[j][...], preferred_element_type=F32)
        for sl in range(cb // LANES):
            slab = j * (cb // LANES) + sl
            rot = _rope_slab(proj[:, sl * LANES:(sl + 1) * LANES], cos_t, sin_up, sin_dn) * (B_HD ** -0.5 * LOG2E)
            for p in range(2):
                hq = 2 * slab + p
                x = jnp.where(lane_half == p, rot, 0.0)
                if p != (hq // B_GROUP) % 2:
                    x = pltpu.roll(x, B_HD, axis=1)
                qx_ref[:, hq * LANES:(hq + 1) * LANES] = x.astype(qx_ref.dtype)
    for j in range(nq):
        z_ref[:, j * cb:(j + 1) * cb] = jnp.dot(
            xb, w_refs[nq + j][...], preferred_element_type=F32).astype(z_ref.dtype)
    if kv is not None:
        rows_per_seq, ns, keep = kv
        kvp = jnp.dot(xb, wkv_ref[...], preferred_element_type=F32)
        k = jnp.concatenate([_rope_slab(kvp[:, s * LANES:(s + 1) * LANES], cos_t, sin_up, sin_dn)
                             for s in range(B_KVW // LANES)], axis=1)
        v = kvp[:, B_KVW:]
        k_ref[...] = k.astype(k_ref.dtype)
        v_ref[...] = v.astype(v_ref.dtype)
        for s in range(ns):
            end = (s + 1) * rows_per_seq
            klast_ref[s] = k[end - keep:end, :]
            vlast_ref[s] = v[end - keep:end, :]


def _b_in(x2, w_in16, tables, *, layer, batch, seq, w_kv16=None):
    tokens = batch * seq
    tm, ns, tab = _row_tiling(tokens, seq)
    row = lambda i: (i, 0)
    nw = 2 * B_QW // COL_BLOCK
    in_specs = [pl.BlockSpec((tm, D_MODEL), row)] + _weight_specs(layer, D_MODEL, nw, 1)
    out_specs = [pl.BlockSpec((tm, B_QX), row), pl.BlockSpec((tm, B_QW), row)]
    out_shape = [jax.ShapeDtypeStruct((tokens, B_QX), BF16), jax.ShapeDtypeStruct((tokens, B_QW), BF16)]
    operands = [x2] + [w_in16] * nw
    kv = None
    if w_kv16 is not None:
        keep = min(WINDOW, seq)
        rows_per_seq = tm // ns
        assert keep <= rows_per_seq
        per_seq = max(1, seq // tm)
        lastb = lambda i: (i // per_seq, 0, 0)
        kv = (rows_per_seq, ns, keep)
        in_specs.append(pl.BlockSpec((D_MODEL, 2 * B_KVW), lambda i: (0, 0)))
        operands.append(w_kv16)
        out_specs += [pl.BlockSpec((tm, B_KVW), row), pl.BlockSpec((tm, B_KVW), row),
                      pl.BlockSpec((ns, keep, B_KVW), lastb), pl.BlockSpec((ns, keep, B_KVW), lastb)]
        out_shape += [jax.ShapeDtypeStruct((tokens, B_KVW), BF16)] * 2
        out_shape += [jax.ShapeDtypeStruct((batch, keep, B_KVW), F32)] * 2
    in_specs += [pl.BlockSpec((tm, LANES), tab)] * 3
    return pl.pallas_call(
        functools.partial(_b_in_kernel, tm=tm, kv=kv),
        grid=(tokens // tm,),
        in_specs=in_specs,
        out_specs=out_specs,
        out_shape=out_shape,
        compiler_params=_params(("arbitrary",)),
        name="b_in",
    )(*operands, *_tile_tables(tables, ns))


def _attn_body(sink_ref, qx_ref, z_ref, k_ref, v_ref, o_ref, *, lq, lk, nc, masked):
    step = pl.program_id(1)
    lane_half = lax.broadcasted_iota(jnp.int32, (lq, LANES), 1) >> int(math.log2(B_HD))
    kv_slabs = range(B_KVW // LANES)
    kwin, vwin, valid = [], [], []
    for ci in range(nc):
        if masked is None:
            start, seq_i = 0, ci
            valid.append(None)
        else:
            seq_i = 0
            c = step * nc + ci
            first = jnp.maximum(c - WINDOW_CHUNKS, 0)
            start = pl.multiple_of(first * CHUNK, CHUNK)
            key_chunk = first + (lax.broadcasted_iota(jnp.int32, (lq, lk), 1) >> int(math.log2(CHUNK)))
            valid.append((key_chunk <= c) if masked else None)
        kwin.append([k_ref[seq_i, pl.ds(start, lk), s * LANES:(s + 1) * LANES] for s in kv_slabs])
        vwin.append([v_ref[seq_i, pl.ds(start, lk), s * LANES:(s + 1) * LANES] for s in kv_slabs])
    units = [(ci, j) for ci in range(nc) for j in range(B_KV_HEADS)]
    qstack = [jnp.concatenate([qx_ref[ci * lq:(ci + 1) * lq, (B_GROUP * j + g) * LANES:(B_GROUP * j + g + 1) * LANES]
                               for g in range(B_GROUP)], axis=0) for ci, j in units]
    scores = [_mm_nt(qstack[u], kwin[ci][j // 2]) for u, (ci, j) in enumerate(units)]
    heads = [(u, ci, j, g) for u, (ci, j) in enumerate(units) for g in range(B_GROUP)]
    sk = [sink_ref[B_GROUP * j + g] * LOG2E for _, _, j, g in heads]
    sc = [scores[u][g * lq:(g + 1) * lq] for u, _, _, g in heads]
    sc = [s if valid[ci] is None else jnp.where(valid[ci], s, -jnp.inf) for s, (_, ci, _, _) in zip(sc, heads)]
    mx = [jnp.maximum(jnp.max(s, axis=-1, keepdims=True), k) for s, k in zip(sc, sk)]
    pstack = [jnp.concatenate([jnp.exp2(sc[h] - mx[h]).astype(BF16) for h in range(u * B_GROUP, (u + 1) * B_GROUP)],
                              axis=0) for u in range(len(units))]
    ones = jnp.ones((lk, LANES), BF16)
    pv = [_mm(pstack[u], jnp.concatenate([vwin[ci][j // 2], ones], axis=1))
          for u, (ci, j) in enumerate(units)]
    out = []
    for h, (u, _, _, g) in enumerate(heads):
        part = pv[u][g * lq:(g + 1) * lq]
        out.append(part[:, :LANES] / (part[:, LANES:] + jnp.exp2(sk[h] - mx[h])))
    for ci in range(nc):
        rows = slice(ci * lq, (ci + 1) * lq)
        for slab in range(B_QW // LANES):
            pair, g = divmod(slab, B_GROUP)
            lo = out[(ci * B_KV_HEADS + 2 * pair) * B_GROUP + g]
            hi = out[(ci * B_KV_HEADS + 2 * pair + 1) * B_GROUP + g]
            both = jnp.where(lane_half == 0, lo, hi)
            zs = z_ref[rows, slab * LANES:(slab + 1) * LANES].astype(F32)
            o_ref[rows, slab * LANES:(slab + 1) * LANES] = (both * _silu(zs)).astype(o_ref.dtype)


def _attn_kernel(sink_ref, qx_ref, z_ref, k_ref, v_ref, o_ref, *, lq, lk, nc, banded):
    body = functools.partial(_attn_body, sink_ref, qx_ref, z_ref, k_ref, v_ref, o_ref, lq=lq, lk=lk, nc=nc)
    if not banded:
        body(masked=None)
        return
    step = pl.program_id(1)
    clamped_steps = -(-WINDOW_CHUNKS // nc)

    @pl.when(step < clamped_steps)
    def _():
        body(masked=True)

    @pl.when(step >= clamped_steps)
    def _():
        body(masked=False)


def _attention(qx, z, k3, v3, sinks, *, batch, seq, banded):
    lq = min(CHUNK, seq)
    ltot = k3.shape[1]
    if banded:
        nc, nseq = min(ATTN_CHUNKS, seq // lq), 1
        lk = (WINDOW_CHUNKS + 1) * CHUNK
    else:
        assert seq == lq
        nc = nseq = min(ATTN_CHUNKS, batch)
        lk = ltot
    nq = seq * nseq // (lq * nc)
    tokens = batch * seq
    row = lambda b, c: (b * nq + c, 0)
    whole = lambda b, c: (b, 0, 0)
    return pl.pallas_call(
        functools.partial(_attn_kernel, lq=lq, lk=lk, nc=nc, banded=banded),
        grid=(batch // nseq, nq),
        in_specs=[
            pl.BlockSpec(memory_space=pltpu.SMEM),
            pl.BlockSpec((nc * lq, B_QX), row),
            pl.BlockSpec((nc * lq, B_QW), row),
            pl.BlockSpec((nseq, ltot, B_KVW), whole),
            pl.BlockSpec((nseq, ltot, B_KVW), whole),
        ],
        out_specs=pl.BlockSpec((nc * lq, B_QW), row),
        out_shape=jax.ShapeDtypeStruct((tokens, B_QW), BF16),
        compiler_params=_params(("arbitrary", "arbitrary")),
        name="swa_attention",
    )(sinks, qx, z, k3, v3)


def _rope_tables(pos):
    half = ROPE_DIMS // 2
    inv = ROPE_THETA ** (-jnp.arange(half, dtype=F32) * 2.0 / ROPE_DIMS)
    ang = pos.astype(F32)[:, None] * inv[None, :]
    cos, sin = jnp.cos(ang), jnp.sin(ang)
    ones = jnp.ones((pos.shape[0], B_HD - ROPE_DIMS), F32)
    zeros_h = jnp.zeros((pos.shape[0], half), F32)
    zeros_r = jnp.zeros((pos.shape[0], B_HD - ROPE_DIMS), F32)
    cos_head = jnp.concatenate([cos, cos, ones], axis=1)
    up_head = jnp.concatenate([-sin, zeros_h, zeros_r], axis=1)
    dn_head = jnp.concatenate([zeros_h, sin, zeros_r], axis=1)
    rep = LANES // B_HD
    return tuple(jnp.tile(t, (1, rep)) for t in (cos_head, up_head, dn_head))


def _trunk(x, pos, conv_state, delta_state, past_k, past_v, wts):
    batch, seq, _ = x.shape
    tokens = batch * seq
    h = x.reshape(tokens, D_MODEL)
    tables = _rope_tables(pos)
    new_conv, new_delta = [], []
    for i in range(N_A_LAYERS):
        q, k, v, z, gb, cbuf = _a_in(h, wts["a_w_in"], wts["a_w_gate"][i], wts["a_conv_w"][i], conv_state[i],
                                     wts["a_log"][i], wts["a_dt"][i], layer=i, batch=batch, seq=seq)
        o, s_new = _delta(q, k, v, z, gb, delta_state, wts["a_norm_w"][i], layer=i, batch=batch, seq=seq)
        h = _out_ln(o, h, wts["a_w_out"], wts["a_ln_g"][i], wts["a_ln_b"][i], layer=i)
        new_conv.append(cbuf)
        new_delta.append(s_new)
    qx, z, k2, v2, k_last, v_last = _b_in(h, wts["b_w_in"], tables, layer=0, batch=batch, seq=seq,
                                          w_kv16=wts["b_w_kv"])
    k3 = k2.reshape(batch, seq, B_KVW)
    v3 = v2.reshape(batch, seq, B_KVW)
    cached = past_k is not None
    if cached:
        pk = past_k.reshape(batch, -1, B_KVW)
        pv = past_v.reshape(batch, -1, B_KVW)
        k3 = jnp.concatenate([pk.astype(BF16), k3], axis=1)
        v3 = jnp.concatenate([pv.astype(BF16), v3], axis=1)
        k_last = jnp.concatenate([pk, k_last], axis=1)[:, -WINDOW:]
        v_last = jnp.concatenate([pv, v_last], axis=1)[:, -WINDOW:]
    new_k = k_last.reshape(batch, WINDOW, B_KV_HEADS, B_HD)
    new_v = v_last.reshape(batch, WINDOW, B_KV_HEADS, B_HD)
    for j in range(N_B_LAYERS):
        if j > 0:
            qx, z = _b_in(h, wts["b_w_in"], tables, layer=j, batch=batch, seq=seq)
        o = _attention(qx, z, k3, v3, wts["b_sinks"][j], batch=batch, seq=seq, banded=not cached)
        h = _out_ln(o, h, wts["b_w_out"], wts["b_ln_g"][j], wts["b_ln_b"][j], layer=j)
    return h.reshape(batch, seq, D_MODEL), jnp.stack(new_conv), jnp.stack(new_delta), new_k, new_v


def _paired_heads(w, axis):
    order = [B_GROUP * (2 * pair + odd) + g
             for pair in range(B_KV_HEADS // 2) for g in range(B_GROUP) for odd in range(2)]
    shape = w.shape
    blocks = w.reshape(shape[:axis] + (B_Q_HEADS, B_HD) + shape[axis + 1:])
    return jnp.take(blocks, jnp.array(order, jnp.int32), axis=axis).reshape(shape)


def kernel(x_prompt, x_sample, state_delta, state_conv, cache_k, cache_v, a_w_in, a_conv_w, a_log, a_dt_bias,
           a_norm_w, a_w_out, a_ln_g, a_ln_b, b_w_kv, b_w_in, b_sinks, b_w_out, b_ln_g, b_ln_b):
    a_w_in16 = a_w_in.astype(BF16)
    zeros_h = jnp.zeros((N_A_LAYERS, A_HEADS), F32)
    wts = {
        "a_w_in": a_w_in16,
        "a_w_gate": jnp.pad(a_w_in16[:, :, A_MAIN:], ((0, 0), (0, 0), (0, LANES - 2 * A_HEADS))),
        "a_conv_w": a_conv_w,
        "a_log": jnp.concatenate([a_log.astype(F32), zeros_h], axis=1)[:, :, None],
        "a_dt": jnp.concatenate([a_dt_bias.astype(F32), zeros_h], axis=1)[:, :, None],
        "a_norm_w": a_norm_w.reshape(N_A_LAYERS, 1, A_DV),
        "a_w_out": a_w_out.astype(BF16),
        "a_ln_g": a_ln_g.reshape(N_A_LAYERS, 1, D_MODEL),
        "a_ln_b": a_ln_b.reshape(N_A_LAYERS, 1, D_MODEL),
        "b_w_kv": b_w_kv.astype(BF16),
        "b_w_in": jnp.concatenate([b_w_in[:, :, :B_QW], _paired_heads(b_w_in[:, :, B_QW:], axis=2)],
                                  axis=2).astype(BF16),
        "b_sinks": b_sinks,
        "b_w_out": _paired_heads(b_w_out, axis=1).astype(BF16),
        "b_ln_g": b_ln_g.reshape(N_B_LAYERS, 1, D_MODEL),
        "b_ln_b": b_ln_b.reshape(N_B_LAYERS, 1, D_MODEL),
    }
    bp, lp, _ = x_prompt.shape
    bs, ls, _ = x_sample.shape
    pos_prompt = jnp.arange(lp, dtype=jnp.int32)
    pos_sample = PAST_LEN + jnp.arange(ls, dtype=jnp.int32)
    zero_conv = jnp.zeros((N_A_LAYERS, bp, CONV_W - 1, CONV_DIM), F32)
    zero_delta = jnp.zeros((N_A_LAYERS, bp, A_HEADS, A_DK, A_DV), F32)
    y_p, p_conv, p_delta, p_k, p_v = _trunk(x_prompt, pos_prompt, zero_conv, zero_delta, None, None, wts)
    y_s, s_conv, s_delta, s_k, s_v = _trunk(x_sample, pos_sample, state_conv, state_delta, cache_k, cache_v, wts)
    return (y_p, y_s, p_delta, p_conv, p_k, p_v, s_delta, s_conv, s_k, s_v)
```

```python
import functools
import math

import jax
import jax.numpy as jnp
from jax import lax
from jax.experimental import pallas as pl
from jax.experimental.pallas import tpu as pltpu

D_MODEL = 1024
DEPTH = 4
PAST_LEN = 4096
CHUNK = 64
N_A_LAYERS = DEPTH // 2
N_B_LAYERS = DEPTH - N_A_LAYERS
A_HEADS = 8
A_DK = 128
A_DV = 128
A_QK = A_HEADS * A_DK
A_VW = A_HEADS * A_DV
CONV_W = 4
CONV_DIM = 2 * A_QK + A_VW
A_MAIN = CONV_DIM + A_VW
B_Q_HEADS = 16
B_KV_HEADS = 4
B_GROUP = B_Q_HEADS // B_KV_HEADS
B_HD = 64
B_QW = B_Q_HEADS * B_HD
B_KVW = B_KV_HEADS * B_HD
WINDOW = 128
WINDOW_CHUNKS = WINDOW // CHUNK
ROPE_DIMS = B_HD // 4
ROPE_THETA = 500000.0
DN_ALPHA = (2 * DEPTH) ** 0.25
LN_EPS = 1e-5
RMS_EPS = 1e-6

LANES = 128
SUBLANES = 8
VMEM_LIMIT = 48 * 1024 * 1024
ROW_TILE = 256
WIDE_TILE = 512
OUT_TILE = 1024
CONV_PIECE = 512
COL_BLOCK = 512
DELTA_BATCHES = 2
DELTA_CHUNKS = 8
ATTN_CHUNKS = 8
B_QX = B_Q_HEADS * LANES
LOG2E = math.log2(math.e)

F32 = jnp.float32
BF16 = jnp.bfloat16


def _mm(a, b):
    return jnp.dot(a.astype(BF16), b.astype(BF16), preferred_element_type=F32)


def _mm_nt(a, b):
    return lax.dot_general(a.astype(BF16), b.astype(BF16), (((1,), (1,)), ((), ())),
                           preferred_element_type=F32)


def _silu(x):
    h = 0.5 * x
    return h + h * jnp.tanh(h)


def _params(semantics):
    return pltpu.CompilerParams(dimension_semantics=semantics, vmem_limit_bytes=VMEM_LIMIT)


def _weight_specs(layer, kdim, nblocks, grid_rank):
    def spec(j):
        if grid_rank == 1:
            return pl.BlockSpec((None, kdim, COL_BLOCK), lambda i: (layer, 0, j))
        return pl.BlockSpec((None, kdim, COL_BLOCK), lambda b, l: (layer, 0, j))
    return [spec(j) for j in range(nblocks)]


def _a_in_kernel(*refs, ts, ns):
    nw = A_MAIN // COL_BLOCK
    x_ref, w_refs = refs[0], refs[1:1 + nw]
    (wg_ref, cw_ref, c0_ref, alog_ref, dt_ref,
     q_ref, k_ref, v_ref, z_ref, gb_ref, cout_ref, pbuf, carry) = refs[1 + nw:]
    tm = ts * ns
    l = pl.program_id(1)
    tail = SUBLANES - (CONV_W - 1)

    @pl.when(l == 0)
    def _():
        for s in range(ns):
            carry[s, tail:SUBLANES, :] = c0_ref[s]

    xb = x_ref[...].astype(BF16)
    cb = COL_BLOCK
    nconv = CONV_DIM // cb
    outs = (q_ref, k_ref, v_ref)

    def conv_block(s, blk):
        rows = slice(s * ts, (s + 1) * ts)
        cols = slice(blk * LANES, (blk + 1) * LANES)
        ext = jnp.concatenate([carry[s, :, cols], pbuf[rows, cols]], axis=0)
        y = None
        for j in range(CONV_W):
            back = CONV_W - 1 - j
            tap = (pltpu.roll(ext, back, axis=0) if back else ext)[SUBLANES:] * cw_ref[j:j + 1, cols]
            y = tap if y is None else y + tap
        y = _silu(y)
        which, head = divmod(blk, A_HEADS)
        if which < 2:
            y = y * lax.rsqrt(jnp.sum(y * y, axis=-1, keepdims=True) + RMS_EPS)
            if which == 0:
                y = y * (A_DK ** -0.5)
        outs[which][rows, head * LANES:(head + 1) * LANES] = y.astype(outs[which].dtype)

    pc = CONV_PIECE
    npieces = CONV_DIM // pc

    def project(i):
        j, off = divmod(i * pc, cb)
        pbuf[:, i * pc:(i + 1) * pc] = jnp.dot(xb, w_refs[j][:, off:off + pc], preferred_element_type=F32)

    def gate_z(j):
        z_ref[:, j * cb:(j + 1) * cb] = jnp.dot(
            xb, w_refs[nconv + j][...], preferred_element_type=F32).astype(z_ref.dtype)

    def decay_beta():
        gates = jnp.dot(xb, wg_ref[...], preferred_element_type=F32)
        gt = gates.T[:2 * A_HEADS]
        rowi = lax.broadcasted_iota(jnp.int32, gt.shape, 0)
        sp_in = gt + dt_ref[...]
        softplus = jnp.maximum(sp_in, 0.0) + jnp.log1p(jnp.exp(-jnp.abs(sp_in)))
        gval = -jnp.exp(alog_ref[...]) * softplus
        bval = 1.0 / (1.0 + jnp.exp(-gt))
        res = jnp.where(rowi < A_HEADS, gval, bval)
        gb_ref[...] = jnp.concatenate([res, jnp.zeros((LANES - 2 * A_HEADS, tm), F32)], axis=0).T

    light = [functools.partial(gate_z, j) for j in range(A_VW // cb)] + [decay_beta]
    project(0)
    for i in range(npieces):
        if i + 1 < npieces:
            project(i + 1)
        if light and i % (npieces // 4) == 0:
            light.pop(0)()
        for s in range(ns):
            for blk in range(i * pc // LANES, (i + 1) * pc // LANES):
                conv_block(s, blk)
    assert not light

    for s in range(ns):
        last = pbuf[(s + 1) * ts - (CONV_W - 1):(s + 1) * ts, :]
        cout_ref[s] = last
        carry[s, tail:SUBLANES, :] = last


def _a_in(x2, w_in16, w_gate, conv_w, conv0, alog_col, dt_col, *, layer, batch, seq):
    if seq >= WIDE_TILE:
        ts, ns = WIDE_TILE, 1
    else:
        ts, ns = seq, min(batch, WIDE_TILE // seq)
    tm = ts * ns
    nl = seq // ts
    tokens = batch * seq
    row = lambda b, l: (b * nl + l, 0)
    const = lambda b, l: (0, 0)
    perseq = lambda b, l: (b, 0, 0)
    wide = jax.ShapeDtypeStruct((tokens, A_QK), BF16)
    nw = A_MAIN // COL_BLOCK
    return pl.pallas_call(
        functools.partial(_a_in_kernel, ts=ts, ns=ns),
        grid=(batch // ns, nl),
        in_specs=[pl.BlockSpec((tm, D_MODEL), row)] + _weight_specs(layer, D_MODEL, nw, 2) + [
            pl.BlockSpec((D_MODEL, LANES), const),
            pl.BlockSpec((CONV_W, CONV_DIM), const),
            pl.BlockSpec((ns, CONV_W - 1, CONV_DIM), perseq),
            pl.BlockSpec((2 * A_HEADS, 1), const),
            pl.BlockSpec((2 * A_HEADS, 1), const),
        ],
        out_specs=[
            pl.BlockSpec((tm, A_QK), row),
            pl.BlockSpec((tm, A_QK), row),
            pl.BlockSpec((tm, A_VW), row),
            pl.BlockSpec((tm, A_VW), row),
            pl.BlockSpec((tm, LANES), row),
            pl.BlockSpec((ns, CONV_W - 1, CONV_DIM), perseq),
        ],
        out_shape=[wide, wide, wide, wide,
                   jax.ShapeDtypeStruct((tokens, LANES), F32),
                   jax.ShapeDtypeStruct((batch, CONV_W - 1, CONV_DIM), F32)],
        scratch_shapes=[pltpu.VMEM((tm, CONV_DIM), F32), pltpu.VMEM((ns, SUBLANES, CONV_DIM), F32)],
        compiler_params=_params(("arbitrary", "arbitrary")),
        name="a_in",
    )(x2, *([w_in16] * nw), w_gate, conv_w, conv0, alog_col, dt_col)


def _delta_group(bi, r0, q_ref, k_ref, v_ref, z_ref, gb_ref, nw, o_ref, s_ref, c):
    pk = LANES // c
    lc = int(math.log2(c))
    packs = [list(range(p * pk, (p + 1) * pk)) for p in range(A_HEADS // pk)]
    npk = range(len(packs))
    row = lax.broadcasted_iota(jnp.int32, (c, LANES), 0)
    lane = lax.broadcasted_iota(jnp.int32, (c, LANES), 1)
    colr = lane & (c - 1)
    member = lane >> lc
    eye = (row == colr).astype(F32)
    incl = row >= colr
    strict = row > colr
    diag8 = (row >> 3) == (colr >> 3)
    sq0 = lax.broadcasted_iota(jnp.int32, (LANES, LANES), 0)
    sq1 = lax.broadcasted_iota(jnp.int32, (LANES, LANES), 1)
    bd_mask = (sq0 >> lc) == (sq1 >> lc)
    kr0 = lax.broadcasted_iota(jnp.int32, (LANES, pk * LANES), 0)
    kr1 = lax.broadcasted_iota(jnp.int32, (LANES, pk * LANES), 1)
    k_mask = (kr0 >> lc) == (kr1 >> int(math.log2(LANES)))

    def bd(m):
        return jnp.where(bd_mask, jnp.concatenate([m] * pk, axis=0), 0.0)

    def by_member(vals):
        out = vals[0]
        for r in range(1, pk):
            out = jnp.where(member >= r, vals[r], out)
        return out

    def wide(col, hs):
        return jnp.concatenate([jnp.broadcast_to(col[h], (c, LANES)) for h in hs], axis=1)

    def hcol(r):
        return slice(r * LANES, (r + 1) * LANES)

    rows = slice(r0, r0 + c)
    g = gb_ref[bi, rows]
    gcum = g
    step = 1
    while step < c:
        gcum = gcum + jnp.where(row >= step, pltpu.roll(gcum, step, axis=0), 0.0)
        step *= 2
    gcum_t = jnp.concatenate([gcum] * pk, axis=0).T
    gc = [gcum[:, h:h + 1] for h in range(A_HEADS)]
    beta = [g[:, A_HEADS + h:A_HEADS + h + 1] for h in range(A_HEADS)]
    glast = [gcum[c - 1:c, h:h + 1] for h in range(A_HEADS)]
    eg = [jnp.exp(x) for x in gc]
    pcols = [slice(hs[0] * LANES, (hs[-1] + 1) * LANES) for hs in packs]
    kpf = [k_ref[bi, rows, pc].astype(F32) for pc in pcols]
    qp16 = [q_ref[bi, rows, pc] for pc in pcols]
    vpf = [v_ref[bi, rows, pc].astype(F32) for pc in pcols]
    kbeta = [kpf[p] * wide(beta, hs) for p, hs in enumerate(packs)]
    decay = [jnp.exp(jnp.where(incl, by_member([jnp.broadcast_to(gc[h], (c, LANES)) for h in hs])
                               - by_member([gcum_t[h:h + 1, :] for h in hs]), -jnp.inf))
             for hs in packs]
    yield
    k_bd =[jnp.where(k_mask, jnp.concatenate([x] * pk, axis=0), 0.0).astype(BF16) for x in kpf]
    kq = [_mm_nt(jnp.concatenate([kbeta[p].astype(BF16), qp16[p]], axis=0), k_bd[p])
          for p in npk]
    yield
    a = [jnp.where(strict, kq[p][:c] * decay[p], 0.0) for p in npk]
    d = [jnp.where(diag8, x, 0.0) for x in a]
    heads = [(p, r, hs[r]) for p, hs in enumerate(packs) for r in range(pk)]
    s = [s_ref[bi, h] for _, _, h in heads]
    ks = [_mm(jnp.concatenate([kbeta[p][:, hcol(r)] * eg[h], qp16[p][:, hcol(r)].astype(F32) * eg[h]], axis=0), s[i])
          for i, (p, r, h) in enumerate(heads)]
    yield
    d2 = [_mm(x, bd(x)) for x in d]
    yield
    d4 = [_mm(x, bd(x)) for x in d2]
    p1 = [_mm(eye - x, bd(eye + y)) for x, y in zip(d, d2)]
    yield
    xs = [_mm(p, bd(eye + y)) for p, y in zip(p1, d4)]
    yield
    shift = 3
    while (1 << shift) < c:
        mask = ((row >> (shift + 1)) == (colr >> (shift + 1))) & ((row >> shift) > (colr >> shift))
        xc = [_mm(x, bd(jnp.where(mask, m, 0.0))) for x, m in zip(xs, a)]
        yield
        xs = [x - _mm(y, bd(x)) for x, y in zip(xs, xc)]
        yield
        shift += 1
    vn = []
    for p, hs in enumerate(packs):
        vb = vpf[p] * wide(beta, hs)
        stacked = jnp.concatenate([vb[:, hcol(r)] - ks[p * pk + r][:c] for r in range(pk)], axis=0)
        vn.append(_mm(bd(xs[p]), stacked))
    yield
    v_new = [vn[p][r * c:(r + 1) * c] for p, r, _ in heads]
    intra = [kq[p][c:] * decay[p] for p in npk]
    kd_t = [(kpf[p][:, hcol(r)] * jnp.exp(glast[h] - gc[h])).T for p, r, h in heads]
    ov = [_mm(jnp.concatenate([intra[p][:, r * c:(r + 1) * c], kd_t[i]], axis=0), v_new[i])
          for i, (p, r, h) in enumerate(heads)]
    yield
    for i, (p, r, h) in enumerate(heads):
        s_ref[bi, h] = s[i] * jnp.exp(glast[h]) + ov[i][c:]
        o = ks[i][c:] + ov[i][:c]
        zf = z_ref[bi, rows, h * LANES:(h + 1) * LANES].astype(F32)
        gated = o * lax.rsqrt(jnp.mean(o * o, axis=-1, keepdims=True) + RMS_EPS) * nw * _silu(zf)
        o_ref[bi, rows, h * LANES:(h + 1) * LANES] = gated.astype(o_ref.dtype)


def _run_interleaved(gens):
    live = list(gens)
    while live:
        for gen in list(live):
            try:
                next(gen)
            except StopIteration:
                live.remove(gen)


def _delta_kernel(q_ref, k_ref, v_ref, z_ref, gb_ref, s0_ref, nw_ref, o_ref, s_ref, *, c, nb, nchunks):
    n = pl.program_id(1)

    @pl.when(n == 0)
    def _():
        s_ref[...] = s0_ref[...]

    nw = nw_ref[...]
    for ci in range(nchunks):
        _run_interleaved([_delta_group(bi, ci * c, q_ref, k_ref, v_ref, z_ref, gb_ref, nw, o_ref, s_ref, c)
                          for bi in range(nb)])


def _delta(q, k, v, z, gb, s0_all, norm_w, *, layer, batch, seq):
    c = min(CHUNK, seq)
    nchunks = min(DELTA_CHUNKS, seq // c)
    n = seq // (c * nchunks)
    nb = DELTA_BATCHES
    seq3 = lambda a: a.reshape(batch, seq, a.shape[-1])
    blk = lambda b, i: (b, i, 0)
    state = lambda b, i: (b, 0, 0, 0)
    o, s_new = pl.pallas_call(
        functools.partial(_delta_kernel, c=c, nb=nb, nchunks=nchunks),
        grid=(batch // nb, n),
        in_specs=[
            pl.BlockSpec((nb, nchunks * c, A_QK), blk),
            pl.BlockSpec((nb, nchunks * c, A_QK), blk),
            pl.BlockSpec((nb, nchunks * c, A_VW), blk),
            pl.BlockSpec((nb, nchunks * c, A_VW), blk),
            pl.BlockSpec((nb, nchunks * c, LANES), blk),
            pl.BlockSpec((None, nb, A_HEADS, A_DK, A_DV), lambda b, i: (layer, b, 0, 0, 0)),
            pl.BlockSpec((1, LANES), lambda b, i: (0, 0)),
        ],
        out_specs=[
            pl.BlockSpec((nb, nchunks * c, A_VW), blk),
            pl.BlockSpec((nb, A_HEADS, A_DK, A_DV), state),
        ],
        out_shape=[jax.ShapeDtypeStruct((batch, seq, A_VW), BF16),
                   jax.ShapeDtypeStruct((batch, A_HEADS, A_DK, A_DV), F32)],
        compiler_params=_params(("arbitrary", "arbitrary")),
        name="delta_rule",
    )(seq3(q), seq3(k), seq3(v), seq3(z), seq3(gb), s0_all, norm_w)
    return o.reshape(batch * seq, A_VW), s_new


def _out_ln_kernel(*refs, tm, sub):
    nw = D_MODEL // COL_BLOCK
    o_ref, x_ref, w_refs = refs[0], refs[1], refs[2:2 + nw]
    g_ref, b_ref, y_ref = refs[2 + nw:]
    for r0 in range(0, tm, sub):
        rows = slice(r0, r0 + sub)
        o = o_ref[rows, :]
        proj = jnp.concatenate([jnp.dot(o, w[...], preferred_element_type=F32) for w in w_refs], axis=1)
        r = DN_ALPHA * x_ref[rows, :] + proj
        mu = jnp.mean(r, axis=-1, keepdims=True)
        d = r - mu
        var = jnp.mean(d * d, axis=-1, keepdims=True)
        y_ref[rows, :] = d * lax.rsqrt(var + LN_EPS) * g_ref[...] + b_ref[...]


def _out_ln(o, x2, w_out16, ln_g, ln_b, *, layer):
    tokens = x2.shape[0]
    tm = min(OUT_TILE, tokens)
    sub = min(ROW_TILE, tm)
    row = lambda i: (i, 0)
    const = lambda i: (0, 0)
    nw = D_MODEL // COL_BLOCK
    return pl.pallas_call(
        functools.partial(_out_ln_kernel, tm=tm, sub=sub),
        grid=(tokens // tm,),
        in_specs=[pl.BlockSpec((tm, o.shape[1]), row), pl.BlockSpec((tm, D_MODEL), row)]
        + _weight_specs(layer, o.shape[1], nw, 1)
        + [pl.BlockSpec((1, D_MODEL), const), pl.BlockSpec((1, D_MODEL), const)],
        out_specs=pl.BlockSpec((tm, D_MODEL), row),
        out_shape=jax.ShapeDtypeStruct((tokens, D_MODEL), F32),
        compiler_params=_params(("arbitrary",)),
        name="out_ln",
    )(o, x2, *([w_out16] * nw), ln_g, ln_b)


def _rope_slab(x, cos_t, sin_up, sin_dn):
    half = ROPE_DIMS // 2
    return x * cos_t + pltpu.roll(x, LANES - half, axis=1) * sin_up + pltpu.roll(x, half, axis=1) * sin_dn


def _row_tiling(tokens, seq):
    tm = min(WIDE_TILE, tokens)
    if seq >= tm:
        per_seq = seq // tm
        return tm, 1, (lambda i: (i % per_seq, 0))
    return tm, tm // seq, (lambda i: (0, 0))


def _tile_tables(tables, reps):
    return tables if reps == 1 else tuple(jnp.tile(t, (reps, 1)) for t in tables)


def _b_in_kernel(*refs, tm, kv):
    nw = 2 * B_QW // COL_BLOCK
    x_ref, w_refs = refs[0], refs[1:1 + nw]
    if kv is None:
        cos_ref, sup_ref, sdn_ref, qx_ref, z_ref = refs[1 + nw:]
    else:
        wkv_ref, cos_ref, sup_ref, sdn_ref, qx_ref, z_ref, k_ref, v_ref, klast_ref, vlast_ref = refs[1 + nw:]
    xb = x_ref[...].astype(BF16)
    cos_t, sin_up, sin_dn = cos_ref[...], sup_ref[...], sdn_ref[...]
    lane_half = lax.broadcasted_iota(jnp.int32, (tm, LANES), 1) >> int(math.log2(B_HD))
    cb = COL_BLOCK
    nq = B_QW // cb
    for j in range(nq):
        proj = jnp.dot(xb, w_refs[j][...], preferred_element_type=F32)
        for sl in range(cb // LANES):
            slab = j * (cb // LANES) + sl
            rot = _rope_slab(proj[:, sl * LANES:(sl + 1) * LANES], cos_t, sin_up, sin_dn) * (B_HD ** -0.5 * LOG2E)
            for p in range(2):
                hq = 2 * slab + p
                x = jnp.where(lane_half == p, rot, 0.0)
                if p != (hq // B_GROUP) % 2:
                    x = pltpu.roll(x, B_HD, axis=1)
                qx_ref[:, hq * LANES:(hq + 1) * LANES] = x.astype(qx_ref.dtype)
    for j in range(nq):
        z_ref[:, j * cb:(j + 1) * cb] = jnp.dot(
            xb, w_refs[nq + j][...], preferred_element_type=F32).astype(z_ref.dtype)
    if kv is not None:
        rows_per_seq, ns, keep = kv
        kvp = jnp.dot(xb, wkv_ref[...], preferred_element_type=F32)
        k = jnp.concatenate([_rope_slab(kvp[:, s * LANES:(s + 1) * LANES], cos_t, sin_up, sin_dn)
                             for s in range(B_KVW // LANES)], axis=1)
        v = kvp[:, B_KVW:]
        k_ref[...] = k.astype(k_ref.dtype)
        v_ref[...] = v.astype(v_ref.dtype)
        for s in range(ns):
            end = (s + 1) * rows_per_seq
            klast_ref[s] = k[end - keep:end, :]
            vlast_ref[s] = v[end - keep:end, :]


def _b_in(x2, w_in16, tables, *, layer, batch, seq, w_kv16=None):
    tokens = batch * seq
    tm, ns, tab = _row_tiling(tokens, seq)
    row = lambda i: (i, 0)
    nw = 2 * B_QW // COL_BLOCK
    in_specs = [pl.BlockSpec((tm, D_MODEL), row)] + _weight_specs(layer, D_MODEL, nw, 1)
    out_specs = [pl.BlockSpec((tm, B_QX), row), pl.BlockSpec((tm, B_QW), row)]
    out_shape = [jax.ShapeDtypeStruct((tokens, B_QX), BF16), jax.ShapeDtypeStruct((tokens, B_QW), BF16)]
    operands = [x2] + [w_in16] * nw
    kv = None
    if w_kv16 is not None:
        keep = min(WINDOW, seq)
        rows_per_seq = tm // ns
        assert keep <= rows_per_seq
        per_seq = max(1, seq // tm)
        lastb = lambda i: (i // per_seq, 0, 0)
        kv = (rows_per_seq, ns, keep)
        in_specs.append(pl.BlockSpec((D_MODEL, 2 * B_KVW), lambda i: (0, 0)))
        operands.append(w_kv16)
        out_specs += [pl.BlockSpec((tm, B_KVW), row), pl.BlockSpec((tm, B_KVW), row),
                      pl.BlockSpec((ns, keep, B_KVW), lastb), pl.BlockSpec((ns, keep, B_KVW), lastb)]
        out_shape += [jax.ShapeDtypeStruct((tokens, B_KVW), BF16)] * 2
        out_shape += [jax.ShapeDtypeStruct((batch, keep, B_KVW), F32)] * 2
    in_specs += [pl.BlockSpec((tm, LANES), tab)] * 3
    return pl.pallas_call(
        functools.partial(_b_in_kernel, tm=tm, kv=kv),
        grid=(tokens // tm,),
        in_specs=in_specs,
        out_specs=out_specs,
        out_shape=out_shape,
        compiler_params=_params(("arbitrary",)),
        name="b_in",
    )(*operands, *_tile_tables(tables, ns))


def _attn_body(sink_ref, qx_ref, z_ref, k_ref, v_ref, o_ref, *, lq, lk, nc, masked):
    step = pl.program_id(1)
    lane_half = lax.broadcasted_iota(jnp.int32, (lq, LANES), 1) >> int(math.log2(B_HD))
    kv_slabs = range(B_KVW // LANES)
    kwin, vwin, valid = [], [], []
    for ci in range(nc):
        if masked is None:
            start, seq_i = 0, ci
            valid.append(None)
        else:
            seq_i = 0
            c = step * nc + ci
            first = jnp.maximum(c - WINDOW_CHUNKS, 0)
            start = pl.multiple_of(first * CHUNK, CHUNK)
            key_chunk = first + (lax.broadcasted_iota(jnp.int32, (lq, lk), 1) >> int(math.log2(CHUNK)))
            valid.append((key_chunk <= c) if masked else None)
        kwin.append([k_ref[seq_i, pl.ds(start, lk), s * LANES:(s + 1) * LANES] for s in kv_slabs])
        vwin.append([v_ref[seq_i, pl.ds(start, lk), s * LANES:(s + 1) * LANES] for s in kv_slabs])
    units = [(ci, j) for ci in range(nc) for j in range(B_KV_HEADS)]
    qstack = [jnp.concatenate([qx_ref[ci * lq:(ci + 1) * lq, (B_GROUP * j + g) * LANES:(B_GROUP * j + g + 1) * LANES]
                               for g in range(B_GROUP)], axis=0) for ci, j in units]
    scores = [_mm_nt(qstack[u], kwin[ci][j // 2]) for u, (ci, j) in enumerate(units)]
    heads = [(u, ci, j, g) for u, (ci, j) in enumerate(units) for g in range(B_GROUP)]
    sk = [sink_ref[B_GROUP * j + g] * LOG2E for _, _, j, g in heads]
    sc = [scores[u][g * lq:(g + 1) * lq] for u, _, _, g in heads]
    sc = [s if valid[ci] is None else jnp.where(valid[ci], s, -jnp.inf) for s, (_, ci, _, _) in zip(sc, heads)]
    mx = [jnp.maximum(jnp.max(s, axis=-1, keepdims=True), k) for s, k in zip(sc, sk)]
    pstack = [jnp.concatenate([jnp.exp2(sc[h] - mx[h]).astype(BF16) for h in range(u * B_GROUP, (u + 1) * B_GROUP)],
                              axis=0) for u in range(len(units))]
    ones = jnp.ones((lk, LANES), BF16)
    pv = [_mm(pstack[u], jnp.concatenate([vwin[ci][j // 2], ones], axis=1))
          for u, (ci, j) in enumerate(units)]
    out = []
    for h, (u, _, _, g) in enumerate(heads):
        part = pv[u][g * lq:(g + 1) * lq]
        out.append(part[:, :LANES] / (part[:, LANES:] + jnp.exp2(sk[h] - mx[h])))
    for ci in range(nc):
        rows = slice(ci * lq, (ci + 1) * lq)
        for slab in range(B_QW // LANES):
            pair, g = divmod(slab, B_GROUP)
            lo = out[(ci * B_KV_HEADS + 2 * pair) * B_GROUP + g]
            hi = out[(ci * B_KV_HEADS + 2 * pair + 1) * B_GROUP + g]
            both = jnp.where(lane_half == 0, lo, hi)
            zs = z_ref[rows, slab * LANES:(slab + 1) * LANES].astype(F32)
            o_ref[rows, slab * LANES:(slab + 1) * LANES] = (both * _silu(zs)).astype(o_ref.dtype)


def _attn_kernel(sink_ref, qx_ref, z_ref, k_ref, v_ref, o_ref, *, lq, lk, nc, banded):
    body = functools.partial(_attn_body, sink_ref, qx_ref, z_ref, k_ref, v_ref, o_ref, lq=lq, lk=lk, nc=nc)
    if not banded:
        body(masked=None)
        return
    step = pl.program_id(1)
    clamped_steps = -(-WINDOW_CHUNKS // nc)

    @pl.when(step < clamped_steps)
    def _():
        body(masked=True)

    @pl.when(step >= clamped_steps)
    def _():
        body(masked=False)


def _attention(qx, z, k3, v3, sinks, *, batch, seq, banded):
    lq = min(CHUNK, seq)
    ltot = k3.shape[1]
    if banded:
        nc, nseq = min(ATTN_CHUNKS, seq // lq), 1
        lk = (WINDOW_CHUNKS + 1) * CHUNK
    else:
        assert seq == lq
        nc = nseq = min(ATTN_CHUNKS, batch)
        lk = ltot
    nq = seq * nseq // (lq * nc)
    tokens = batch * seq
    row = lambda b, c: (b * nq + c, 0)
    whole = lambda b, c: (b, 0, 0)
    return pl.pallas_call(
        functools.partial(_attn_kernel, lq=lq, lk=lk, nc=nc, banded=banded),
        grid=(batch // nseq, nq),
        in_specs=[
            pl.BlockSpec(memory_space=pltpu.SMEM),
            pl.BlockSpec((nc * lq, B_QX), row),
            pl.BlockSpec((nc * lq, B_QW), row),
            pl.BlockSpec((nseq, ltot, B_KVW), whole),
            pl.BlockSpec((nseq, ltot, B_KVW), whole),
        ],
        out_specs=pl.BlockSpec((nc * lq, B_QW), row),
        out_shape=jax.ShapeDtypeStruct((tokens, B_QW), BF16),
        compiler_params=_params(("arbitrary", "arbitrary")),
        name="swa_attention",
    )(sinks, qx, z, k3, v3)


def _rope_tables(pos):
    half = ROPE_DIMS // 2
    inv = ROPE_THETA ** (-jnp.arange(half, dtype=F32) * 2.0 / ROPE_DIMS)
    ang = pos.astype(F32)[:, None] * inv[None, :]
    cos, sin = jnp.cos(ang), jnp.sin(ang)
    ones = jnp.ones((pos.shape[0], B_HD - ROPE_DIMS), F32)
    zeros_h = jnp.zeros((pos.shape[0], half), F32)
    zeros_r = jnp.zeros((pos.shape[0], B_HD - ROPE_DIMS), F32)
    cos_head = jnp.concatenate([cos, cos, ones], axis=1)
    up_head = jnp.concatenate([-sin, zeros_h, zeros_r], axis=1)
    dn_head = jnp.concatenate([zeros_h, sin, zeros_r], axis=1)
    rep = LANES // B_HD
    return tuple(jnp.tile(t, (1, rep)) for t in (cos_head, up_head, dn_head))


def _trunk(x, pos, conv_state, delta_state, past_k, past_v, wts):
    batch, seq, _ = x.shape
    tokens = batch * seq
    h = x.reshape(tokens, D_MODEL)
    tables = _rope_tables(pos)
    new_conv, new_delta = [], []
    for i in range(N_A_LAYERS):
        q, k, v, z, gb, cbuf = _a_in(h, wts["a_w_in"], wts["a_w_gate"][i], wts["a_conv_w"][i], conv_state[i],
                                     wts["a_log"][i], wts["a_dt"][i], layer=i, batch=batch, seq=seq)
        o, s_new = _delta(q, k, v, z, gb, delta_state, wts["a_norm_w"][i], layer=i, batch=batch, seq=seq)
        h = _out_ln(o, h, wts["a_w_out"], wts["a_ln_g"][i], wts["a_ln_b"][i], layer=i)
        new_conv.append(cbuf)
        new_delta.append(s_new)
    qx, z, k2, v2, k_last, v_last = _b_in(h, wts["b_w_in"], tables, layer=0, batch=batch, seq=seq,
                                          w_kv16=wts["b_w_kv"])
    k3 = k2.reshape(batch, seq, B_KVW)
    v3 = v2.reshape(batch, seq, B_KVW)
    cached = past_k is not None
    if cached:
        pk = past_k.reshape(batch, -1, B_KVW)
        pv = past_v.reshape(batch, -1, B_KVW)
        k3 = jnp.concatenate([pk.astype(BF16), k3], axis=1)
        v3 = jnp.concatenate([pv.astype(BF16), v3], axis=1)
        k_last = jnp.concatenate([pk, k_last], axis=1)[:, -WINDOW:]
        v_last = jnp.concatenate([pv, v_last], axis=1)[:, -WINDOW:]
    new_k = k_last.reshape(batch, WINDOW, B_KV_HEADS, B_HD)
    new_v = v_last.reshape(batch, WINDOW, B_KV_HEADS, B_HD)
    for j in range(N_B_LAYERS):
        if j > 0:
            qx, z = _b_in(h, wts["b_w_in"], tables, layer=j, batch=batch, seq=seq)
        o = _attention(qx, z, k3, v3, wts["b_sinks"][j], batch=batch, seq=seq, banded=not cached)
        h = _out_ln(o, h, wts["b_w_out"], wts["b_ln_g"][j], wts["b_ln_b"][j], layer=j)
    return h.reshape(batch, seq, D_MODEL), jnp.stack(new_conv), jnp.stack(new_delta), new_k, new_v


def _paired_heads(w, axis):
    order = [B_GROUP * (2 * pair + odd) + g
             for pair in range(B_KV_HEADS // 2) for g in range(B_GROUP) for odd in range(2)]
    shape = w.shape
    blocks = w.reshape(shape[:axis] + (B_Q_HEADS, B_HD) + shape[axis + 1:])
    return jnp.take(blocks, jnp.array(order, jnp.int32), axis=axis).reshape(shape)


def kernel(x_prompt, x_sample, state_delta, state_conv, cache_k, cache_v, a_w_in, a_conv_w, a_log, a_dt_bias,
           a_norm_w, a_w_out, a_ln_g, a_ln_b, b_w_kv, b_w_in, b_sinks, b_w_out, b_ln_g, b_ln_b):
    zeros_h = jnp.zeros((N_A_LAYERS, A_HEADS), F32)
    wts = {
        "a_w_in": a_w_in[:, :, :A_MAIN].astype(BF16),
        "a_w_gate": jnp.pad(a_w_in[:, :, A_MAIN:].astype(BF16), ((0, 0), (0, 0), (0, LANES - 2 * A_HEADS))),
        "a_conv_w": a_conv_w,
        "a_log": jnp.concatenate([a_log.astype(F32), zeros_h], axis=1)[:, :, None],
        "a_dt": jnp.concatenate([a_dt_bias.astype(F32), zeros_h], axis=1)[:, :, None],
        "a_norm_w": a_norm_w.reshape(N_A_LAYERS, 1, A_DV),
        "a_w_out": a_w_out.astype(BF16),
        "a_ln_g": a_ln_g.reshape(N_A_LAYERS, 1, D_MODEL),
        "a_ln_b": a_ln_b.reshape(N_A_LAYERS, 1, D_MODEL),
        "b_w_kv": b_w_kv.astype(BF16),
        "b_w_in": jnp.concatenate([b_w_in[:, :, :B_QW], _paired_heads(b_w_in[:, :, B_QW:], axis=2)],
                                  axis=2).astype(BF16),
        "b_sinks": b_sinks,
        "b_w_out": _paired_heads(b_w_out, axis=1).astype(BF16),
        "b_ln_g": b_ln_g.reshape(N_B_LAYERS, 1, D_MODEL),
        "b_ln_b": b_ln_b.reshape(N_B_LAYERS, 1, D_MODEL),
    }
    bp, lp, _ = x_prompt.shape
    bs, ls, _ = x_sample.shape
    pos_prompt = jnp.arange(lp, dtype=jnp.int32)
    pos_sample = PAST_LEN + jnp.arange(ls, dtype=jnp.int32)
    zero_conv = jnp.zeros((N_A_LAYERS, bp, CONV_W - 1, CONV_DIM), F32)
    zero_delta = jnp.zeros((N_A_LAYERS, bp, A_HEADS, A_DK, A_DV), F32)
    y_p, p_conv, p_delta, p_k, p_v = _trunk(x_prompt, pos_prompt, zero_conv, zero_delta, None, None, wts)
    y_s, s_conv, s_delta, s_k, s_v = _trunk(x_sample, pos_sample, state_conv, state_delta, cache_k, cache_v, wts)
    return (y_p, y_s, p_delta, p_conv, p_k, p_v, s_delta, s_conv, s_k, s_v)
```

```python
import functools
import math

import jax
import jax.numpy as jnp
from jax import lax
from jax.experimental import pallas as pl
from jax.experimental.pallas import tpu as pltpu

D_MODEL = 1024
DEPTH = 4
PAST_LEN = 4096
CHUNK = 64
N_A_LAYERS = DEPTH // 2
N_B_LAYERS = DEPTH - N_A_LAYERS
A_HEADS = 8
A_DK = 128
A_DV = 128
A_QK = A_HEADS * A_DK
A_VW = A_HEADS * A_DV
CONV_W = 4
CONV_DIM = 2 * A_QK + A_VW
A_MAIN = CONV_DIM + A_VW
B_Q_HEADS = 16
B_KV_HEADS = 4
B_GROUP = B_Q_HEADS // B_KV_HEADS
B_HD = 64
B_QW = B_Q_HEADS * B_HD
B_KVW = B_KV_HEADS * B_HD
WINDOW = 128
WINDOW_CHUNKS = WINDOW // CHUNK
ROPE_DIMS = B_HD // 4
ROPE_THETA = 500000.0
DN_ALPHA = (2 * DEPTH) ** 0.25
LN_EPS = 1e-5
RMS_EPS = 1e-6

LANES = 128
SUBLANES = 8
VMEM_LIMIT = 48 * 1024 * 1024
ROW_TILE = 256
WIDE_TILE = 512
OUT_TILE = 1024
CONV_PIECE = 512
COL_BLOCK = 512
DELTA_BATCHES = 2
DELTA_CHUNKS = 8
ATTN_CHUNKS = 8
B_QX = B_Q_HEADS * LANES
LOG2E = math.log2(math.e)

F32 = jnp.float32
BF16 = jnp.bfloat16


def _mm(a, b):
    return jnp.dot(a.astype(BF16), b.astype(BF16), preferred_element_type=F32)


def _mm_nt(a, b):
    return lax.dot_general(a.astype(BF16), b.astype(BF16), (((1,), (1,)), ((), ())),
                           preferred_element_type=F32)


def _silu(x):
    h = 0.5 * x
    return h + h * jnp.tanh(h)


def _params(semantics):
    return pltpu.CompilerParams(dimension_semantics=semantics, vmem_limit_bytes=VMEM_LIMIT)


def _weight_specs(layer, kdim, nblocks, grid_rank):
    def spec(j):
        if grid_rank == 1:
            return pl.BlockSpec((None, kdim, COL_BLOCK), lambda i: (layer, 0, j))
        return pl.BlockSpec((None, kdim, COL_BLOCK), lambda b, l: (layer, 0, j))
    return [spec(j) for j in range(nblocks)]


def _a_in_kernel(*refs, ts, ns):
    nw = A_MAIN // COL_BLOCK
    x_ref, w_refs = refs[0], refs[1:1 + nw]
    (wg_ref, cw_ref, c0_ref, alog_ref, dt_ref,
     q_ref, k_ref, v_ref, z_ref, gb_ref, cout_ref, pbuf, carry) = refs[1 + nw:]
    tm = ts * ns
    l = pl.program_id(1)
    tail = SUBLANES - (CONV_W - 1)

    @pl.when(l == 0)
    def _():
        for s in range(ns):
            carry[s, tail:SUBLANES, :] = c0_ref[s]

    xb = x_ref[...].astype(BF16)
    cb = COL_BLOCK
    nconv = CONV_DIM // cb
    outs = (q_ref, k_ref, v_ref)

    def conv_block(s, blk):
        rows = slice(s * ts, (s + 1) * ts)
        cols = slice(blk * LANES, (blk + 1) * LANES)
        ext = jnp.concatenate([carry[s, :, cols], pbuf[rows, cols]], axis=0)
        y = None
        for j in range(CONV_W):
            back = CONV_W - 1 - j
            tap = (pltpu.roll(ext, back, axis=0) if back else ext)[SUBLANES:] * cw_ref[j:j + 1, cols]
            y = tap if y is None else y + tap
        y = _silu(y)
        which, head = divmod(blk, A_HEADS)
        if which < 2:
            y = y * lax.rsqrt(jnp.sum(y * y, axis=-1, keepdims=True) + RMS_EPS)
            if which == 0:
                y = y * (A_DK ** -0.5)
        outs[which][rows, head * LANES:(head + 1) * LANES] = y.astype(outs[which].dtype)

    pc = CONV_PIECE
    npieces = CONV_DIM // pc

    def project(i):
        j, off = divmod(i * pc, cb)
        pbuf[:, i * pc:(i + 1) * pc] = jnp.dot(xb, w_refs[j][:, off:off + pc], preferred_element_type=F32)

    def gate_z(j):
        z_ref[:, j * cb:(j + 1) * cb] = jnp.dot(
            xb, w_refs[nconv + j][...], preferred_element_type=F32).astype(z_ref.dtype)

    def decay_beta():
        gates = jnp.dot(xb, wg_ref[...], preferred_element_type=F32)
        gt = gates.T[:2 * A_HEADS]
        rowi = lax.broadcasted_iota(jnp.int32, gt.shape, 0)
        sp_in = gt + dt_ref[...]
        softplus = jnp.maximum(sp_in, 0.0) + jnp.log1p(jnp.exp(-jnp.abs(sp_in)))
        gval = -jnp.exp(alog_ref[...]) * softplus
        bval = 1.0 / (1.0 + jnp.exp(-gt))
        res = jnp.where(rowi < A_HEADS, gval, bval)
        gb_ref[...] = jnp.concatenate([res, jnp.zeros((LANES - 2 * A_HEADS, tm), F32)], axis=0).T

    light = [functools.partial(gate_z, j) for j in range(A_VW // cb)] + [decay_beta]
    project(0)
    for i in range(npieces):
        if i + 1 < npieces:
            project(i + 1)
        if light and i % (npieces // 4) == 0:
            light.pop(0)()
        for s in range(ns):
            for blk in range(i * pc // LANES, (i + 1) * pc // LANES):
                conv_block(s, blk)
    assert not light

    for s in range(ns):
        last = pbuf[(s + 1) * ts - (CONV_W - 1):(s + 1) * ts, :]
        cout_ref[s] = last
        carry[s, tail:SUBLANES, :] = last


def _a_in(x2, w_in16, w_gate, conv_w, conv0, alog_col, dt_col, *, layer, batch, seq):
    if seq >= WIDE_TILE:
        ts, ns = WIDE_TILE, 1
    else:
        ts, ns = seq, min(batch, WIDE_TILE // seq)
    tm = ts * ns
    nl = seq // ts
    tokens = batch * seq
    row = lambda b, l: (b * nl + l, 0)
    const = lambda b, l: (0, 0)
    perseq = lambda b, l: (b, 0, 0)
    wide = jax.ShapeDtypeStruct((tokens, A_QK), BF16)
    nw = A_MAIN // COL_BLOCK
    return pl.pallas_call(
        functools.partial(_a_in_kernel, ts=ts, ns=ns),
        grid=(batch // ns, nl),
        in_specs=[pl.BlockSpec((tm, D_MODEL), row)] + _weight_specs(layer, D_MODEL, nw, 2) + [
            pl.BlockSpec((D_MODEL, LANES), const),
            pl.BlockSpec((CONV_W, CONV_DIM), const),
            pl.BlockSpec((ns, CONV_W - 1, CONV_DIM), perseq),
            pl.BlockSpec((2 * A_HEADS, 1), const),
            pl.BlockSpec((2 * A_HEADS, 1), const),
        ],
        out_specs=[
            pl.BlockSpec((tm, A_QK), row),
            pl.BlockSpec((tm, A_QK), row),
            pl.BlockSpec((tm, A_VW), row),
            pl.BlockSpec((tm, A_VW), row),
            pl.BlockSpec((tm, LANES), row),
            pl.BlockSpec((ns, CONV_W - 1, CONV_DIM), perseq),
        ],
        out_shape=[wide, wide, wide, wide,
                   jax.ShapeDtypeStruct((tokens, LANES), F32),
                   jax.ShapeDtypeStruct((batch, CONV_W - 1, CONV_DIM), F32)],
        scratch_shapes=[pltpu.VMEM((tm, CONV_DIM), F32), pltpu.VMEM((ns, SUBLANES, CONV_DIM), F32)],
        compiler_params=_params(("arbitrary", "arbitrary")),
        name="a_in",
    )(x2, *([w_in16] * nw), w_gate, conv_w, conv0, alog_col, dt_col)


def _delta_group(bi, r0, q_ref, k_ref, v_ref, z_ref, gb_ref, nw, o_ref, s_ref, c):
    pk = LANES // c
    lc = int(math.log2(c))
    packs = [list(range(p * pk, (p + 1) * pk)) for p in range(A_HEADS // pk)]
    npk = range(len(packs))
    row = lax.broadcasted_iota(jnp.int32, (c, LANES), 0)
    lane = lax.broadcasted_iota(jnp.int32, (c, LANES), 1)
    colr = lane & (c - 1)
    member = lane >> lc
    eye = (row == colr).astype(F32)
    incl = row >= colr
    strict = row > colr
    diag8 = (row >> 3) == (colr >> 3)
    sq0 = lax.broadcasted_iota(jnp.int32, (LANES, LANES), 0)
    sq1 = lax.broadcasted_iota(jnp.int32, (LANES, LANES), 1)
    bd_mask = (sq0 >> lc) == (sq1 >> lc)
    kr0 = lax.broadcasted_iota(jnp.int32, (LANES, pk * LANES), 0)
    kr1 = lax.broadcasted_iota(jnp.int32, (LANES, pk * LANES), 1)
    k_mask = (kr0 >> lc) == (kr1 >> int(math.log2(LANES)))

    def bd(m):
        return jnp.where(bd_mask, jnp.concatenate([m] * pk, axis=0), 0.0)

    def by_member(vals):
        out = vals[0]
        for r in range(1, pk):
            out = jnp.where(member >= r, vals[r], out)
        return out

    def wide(col, hs):
        return jnp.concatenate([jnp.broadcast_to(col[h], (c, LANES)) for h in hs], axis=1)

    def hcol(r):
        return slice(r * LANES, (r + 1) * LANES)

    rows = slice(r0, r0 + c)
    g = gb_ref[bi, rows]
    gcum = g
    step = 1
    while step < c:
        gcum = gcum + jnp.where(row >= step, pltpu.roll(gcum, step, axis=0), 0.0)
        step *= 2
    gcum_t = jnp.concatenate([gcum] * pk, axis=0).T
    gc = [gcum[:, h:h + 1] for h in range(A_HEADS)]
    beta = [g[:, A_HEADS + h:A_HEADS + h + 1] for h in range(A_HEADS)]
    glast = [gcum[c - 1:c, h:h + 1] for h in range(A_HEADS)]
    eg = [jnp.exp(x) for x in gc]
    pcols = [slice(hs[0] * LANES, (hs[-1] + 1) * LANES) for hs in packs]
    kpf = [k_ref[bi, rows, pc].astype(F32) for pc in pcols]
    qp16 = [q_ref[bi, rows, pc] for pc in pcols]
    vpf = [v_ref[bi, rows, pc].astype(F32) for pc in pcols]
    kbeta = [kpf[p] * wide(beta, hs) for p, hs in enumerate(packs)]
    decay = [jnp.exp(jnp.where(incl, by_member([jnp.broadcast_to(gc[h], (c, LANES)) for h in hs])
                               - by_member([gcum_t[h:h + 1, :] for h in hs]), -jnp.inf))
             for hs in packs]
    yield
    k_bd =[jnp.where(k_mask, jnp.concatenate([x] * pk, axis=0), 0.0).astype(BF16) for x in kpf]
    kq = [_mm_nt(jnp.concatenate([kbeta[p].astype(BF16), qp16[p]], axis=0), k_bd[p])
          for p in npk]
    yield
    a = [jnp.where(strict, kq[p][:c] * decay[p], 0.0) for p in npk]
    d = [jnp.where(diag8, x, 0.0) for x in a]
    heads = [(p, r, hs[r]) for p, hs in enumerate(packs) for r in range(pk)]
    s = [s_ref[bi, h] for _, _, h in heads]
    ks = [_mm(jnp.concatenate([kbeta[p][:, hcol(r)] * eg[h], qp16[p][:, hcol(r)].astype(F32) * eg[h]], axis=0), s[i])
          for i, (p, r, h) in enumerate(heads)]
    yield
    d2 = [_mm(x, bd(x)) for x in d]
    yield
    d4 = [_mm(x, bd(x)) for x in d2]
    p1 = [_mm(eye - x, bd(eye + y)) for x, y in zip(d, d2)]
    yield
    xs = [_mm(p, bd(eye + y)) for p, y in zip(p1, d4)]
    yield
    shift = 3
    while (1 << shift) < c:
        mask = ((row >> (shift + 1)) == (colr >> (shift + 1))) & ((row >> shift) > (colr >> shift))
        xc = [_mm(x, bd(jnp.where(mask, m, 0.0))) for x, m in zip(xs, a)]
        yield
        xs = [x - _mm(y, bd(x)) for x, y in zip(xs, xc)]
        yield
        shift += 1
    vn = []
    for p, hs in enumerate(packs):
        vb = vpf[p] * wide(beta, hs)
        stacked = jnp.concatenate([vb[:, hcol(r)] - ks[p * pk + r][:c] for r in range(pk)], axis=0)
        vn.append(_mm(bd(xs[p]), stacked))
    yield
    v_new = [vn[p][r * c:(r + 1) * c] for p, r, _ in heads]
    intra = [kq[p][c:] * decay[p] for p in npk]
    kd_t = [(kpf[p][:, hcol(r)] * jnp.exp(glast[h] - gc[h])).T for p, r, h in heads]
    ov = [_mm(jnp.concatenate([intra[p][:, r * c:(r + 1) * c], kd_t[i]], axis=0), v_new[i])
          for i, (p, r, h) in enumerate(heads)]
    yield
    for i, (p, r, h) in enumerate(heads):
        s_ref[bi, h] = s[i] * jnp.exp(glast[h]) + ov[i][c:]
        o = ks[i][c:] + ov[i][:c]
        zf = z_ref[bi, rows, h * LANES:(h + 1) * LANES].astype(F32)
        gated = o * lax.rsqrt(jnp.mean(o * o, axis=-1, keepdims=True) + RMS_EPS) * nw * _silu(zf)
        o_ref[bi, rows, h * LANES:(h + 1) * LANES] = gated.astype(o_ref.dtype)


def _run_interleaved(gens):
    live = list(gens)
    while live:
        for gen in list(live):
            try:
                next(gen)
            except StopIteration:
                live.remove(gen)


def _delta_kernel(q_ref, k_ref, v_ref, z_ref, gb_ref, s0_ref, nw_ref, o_ref, s_ref, *, c, nb, nchunks):
    n = pl.program_id(1)

    @pl.when(n == 0)
    def _():
        s_ref[...] = s0_ref[...]

    nw = nw_ref[...]
    for ci in range(nchunks):
        _run_interleaved([_delta_group(bi, ci * c, q_ref, k_ref, v_ref, z_ref, gb_ref, nw, o_ref, s_ref, c)
                          for bi in range(nb)])


def _delta(q, k, v, z, gb, s0_all, norm_w, *, layer, batch, seq):
    c = min(CHUNK, seq)
    nchunks = min(DELTA_CHUNKS, seq // c)
    n = seq // (c * nchunks)
    nb = DELTA_BATCHES
    seq3 = lambda a: a.reshape(batch, seq, a.shape[-1])
    blk = lambda b, i: (b, i, 0)
    state = lambda b, i: (b, 0, 0, 0)
    o, s_new = pl.pallas_call(
        functools.partial(_delta_kernel, c=c, nb=nb, nchunks=nchunks),
        grid=(batch // nb, n),
        in_specs=[
            pl.BlockSpec((nb, nchunks * c, A_QK), blk),
            pl.BlockSpec((nb, nchunks * c, A_QK), blk),
            pl.BlockSpec((nb, nchunks * c, A_VW), blk),
            pl.BlockSpec((nb, nchunks * c, A_VW), blk),
            pl.BlockSpec((nb, nchunks * c, LANES), blk),
            pl.BlockSpec((None, nb, A_HEADS, A_DK, A_DV), lambda b, i: (layer, b, 0, 0, 0)),
            pl.BlockSpec((1, LANES), lambda b, i: (0, 0)),
        ],
        out_specs=[
            pl.BlockSpec((nb, nchunks * c, A_VW), blk),
            pl.BlockSpec((nb, A_HEADS, A_DK, A_DV), state),
        ],
        out_shape=[jax.ShapeDtypeStruct((batch, seq, A_VW), BF16),
                   jax.ShapeDtypeStruct((batch, A_HEADS, A_DK, A_DV), F32)],
        compiler_params=_params(("arbitrary", "arbitrary")),
        name="delta_rule",
    )(seq3(q), seq3(k), seq3(v), seq3(z), seq3(gb), s0_all, norm_w)
    return o.reshape(batch * seq, A_VW), s_new


def _out_ln_kernel(*refs, tm, sub):
    nw = D_MODEL // COL_BLOCK
    o_ref, x_ref, w_refs = refs[0], refs[1], refs[2:2 + nw]
    g_ref, b_ref, y_ref = refs[2 + nw:]
    for r0 in range(0, tm, sub):
        rows = slice(r0, r0 + sub)
        o = o_ref[rows, :]
        proj = jnp.concatenate([jnp.dot(o, w[...], preferred_element_type=F32) for w in w_refs], axis=1)
        r = DN_ALPHA * x_ref[rows, :] + proj
        mu = jnp.mean(r, axis=-1, keepdims=True)
        d = r - mu
        var = jnp.mean(d * d, axis=-1, keepdims=True)
        y_ref[rows, :] = d * lax.rsqrt(var + LN_EPS) * g_ref[...] + b_ref[...]


def _out_ln(o, x2, w_out16, ln_g, ln_b, *, layer):
    tokens = x2.shape[0]
    tm = min(OUT_TILE, tokens)
    sub = min(ROW_TILE, tm)
    row = lambda i: (i, 0)
    const = lambda i: (0, 0)
    nw = D_MODEL // COL_BLOCK
    return pl.pallas_call(
        functools.partial(_out_ln_kernel, tm=tm, sub=sub),
        grid=(tokens // tm,),
        in_specs=[pl.BlockSpec((tm, o.shape[1]), row), pl.BlockSpec((tm, D_MODEL), row)]
        + _weight_specs(layer, o.shape[1], nw, 1)
        + [pl.BlockSpec((1, D_MODEL), const), pl.BlockSpec((1, D_MODEL), const)],
        out_specs=pl.BlockSpec((tm, D_MODEL), row),
        out_shape=jax.ShapeDtypeStruct((tokens, D_MODEL), F32),
        compiler_params=_params(("arbitrary",)),
        name="out_ln",
    )(o, x2, *([w_out16] * nw), ln_g, ln_b)


def _rope_slab(x, cos_t, sin_up, sin_dn):
    half = ROPE_DIMS // 2
    return x * cos_t + pltpu.roll(x, LANES - half, axis=1) * sin_up + pltpu.roll(x, half, axis=1) * sin_dn


def _row_tiling(tokens, seq):
    tm = min(WIDE_TILE, tokens)
    if seq >= tm:
        per_seq = seq // tm
        return tm, 1, (lambda i: (i % per_seq, 0))
    return tm, tm // seq, (lambda i: (0, 0))


def _tile_tables(tables, reps):
    return tables if reps == 1 else tuple(jnp.tile(t, (reps, 1)) for t in tables)


def _b_in_kernel(*refs, tm, kv):
    nw = 2 * B_QW // COL_BLOCK
    x_ref, w_refs = refs[0], refs[1:1 + nw]
    if kv is None:
        cos_ref, sup_ref, sdn_ref, qx_ref, z_ref = refs[1 + nw:]
    else:
        wkv_ref, cos_ref, sup_ref, sdn_ref, qx_ref, z_ref, k_ref, v_ref, klast_ref, vlast_ref = refs[1 + nw:]
    xb = x_ref[...].astype(BF16)
    cos_t, sin_up, sin_dn = cos_ref[...], sup_ref[...], sdn_ref[...]
    lane_half = lax.broadcasted_iota(jnp.int32, (tm, LANES), 1) >> int(math.log2(B_HD))
    cb = COL_BLOCK
    nq = B_QW // cb
    for j in range(nq):
        proj = jnp.dot(xb, w_refs[j][...], preferred_element_type=F32)
        for sl in range(cb // LANES):
            slab = j * (cb // LANES) + sl
            rot = _rope_slab(proj[:, sl * LANES:(sl + 1) * LANES], cos_t, sin_up, sin_dn) * (B_HD ** -0.5 * LOG2E)
            for p in range(2):
                hq = 2 * slab + p
                x = jnp.where(lane_half == p, rot, 0.0)
                if p != (hq // B_GROUP) % 2:
                    x = pltpu.roll(x, B_HD, axis=1)
                qx_ref[:, hq * LANES:(hq + 1) * LANES] = x.astype(qx_ref.dtype)
    for j in range(nq):
        z_ref[:, j * cb:(j + 1) * cb] = jnp.dot(
            xb, w_refs[nq + j][...], preferred_element_type=F32).astype(z_ref.dtype)
    if kv is not None:
        rows_per_seq, ns, keep = kv
        kvp = jnp.dot(xb, wkv_ref[...], preferred_element_type=F32)
        k = jnp.concatenate([_rope_slab(kvp[:, s * LANES:(s + 1) * LANES], cos_t, sin_up, sin_dn)
                             for s in range(B_KVW // LANES)], axis=1)
        v = kvp[:, B_KVW:]
        k_ref[...] = k.astype(k_ref.dtype)
        v_ref[...] = v.astype(v_ref.dtype)
        for s in range(ns):
            end = (s + 1) * rows_per_seq
            klast_ref[s] = k[end - keep:end, :]
            vlast_ref[s] = v[end - keep:end, :]


def _b_in(x2, w_in16, tables, *, layer, batch, seq, w_kv16=None):
    tokens = batch * seq
    tm, ns, tab = _row_tiling(tokens, seq)
    row = lambda i: (i, 0)
    nw = 2 * B_QW // COL_BLOCK
    in_specs = [pl.BlockSpec((tm, D_MODEL), row)] + _weight_specs(layer, D_MODEL, nw, 1)
    out_specs = [pl.BlockSpec((tm, B_QX), row), pl.BlockSpec((tm, B_QW), row)]
    out_shape = [jax.ShapeDtypeStruct((tokens, B_QX), BF16), jax.ShapeDtypeStruct((tokens, B_QW), BF16)]
    operands = [x2] + [w_in16] * nw
    kv = None
    if w_kv16 is not None:
        keep = min(WINDOW, seq)
        rows_per_seq = tm // ns
        assert keep <= rows_per_seq
        per_seq = max(1, seq // tm)
        lastb = lambda i: (i // per_seq, 0, 0)
        kv = (rows_per_seq, ns, keep)
        in_specs.append(pl.BlockSpec((D_MODEL, 2 * B_KVW), lambda i: (0, 0)))
        operands.append(w_kv16)
        out_specs += [pl.BlockSpec((tm, B_KVW), row), pl.BlockSpec((tm, B_KVW), row),
                      pl.BlockSpec((ns, keep, B_KVW), lastb), pl.BlockSpec((ns, keep, B_KVW), lastb)]
        out_shape += [jax.ShapeDtypeStruct((tokens, B_KVW), BF16)] * 2
        out_shape += [jax.ShapeDtypeStruct((batch, keep, B_KVW), F32)] * 2
    in_specs += [pl.BlockSpec((tm, LANES), tab)] * 3
    return pl.pallas_call(
        functools.partial(_b_in_kernel, tm=tm, kv=kv),
        grid=(tokens // tm,),
        in_specs=in_specs,
        out_specs=out_specs,
        out_shape=out_shape,
        compiler_params=_params(("arbitrary",)),
        name="b_in",
    )(*operands, *_tile_tables(tables, ns))


def _attn_body(sink_ref, qx_ref, z_ref, k_ref, v_ref, o_ref, *, lq, lk, nc, masked):
    step = pl.program_id(1)
    lane_half = lax.broadcasted_iota(jnp.int32, (lq, LANES), 1) >> int(math.log2(B_HD))
    kv_slabs = range(B_KVW // LANES)
    kwin, vwin, valid = [], [], []
    for ci in range(nc):
        if masked is None:
            start, seq_i = 0, ci
            valid.append(None)
        else:
            seq_i = 0
            c = step * nc + ci
            first = jnp.maximum(c - WINDOW_CHUNKS, 0)
            start = pl.multiple_of(first * CHUNK, CHUNK)
            key_chunk = first + (lax.broadcasted_iota(jnp.int32, (lq, lk), 1) >> int(math.log2(CHUNK)))
            valid.append((key_chunk <= c) if masked else None)
        kwin.append([k_ref[seq_i, pl.ds(start, lk), s * LANES:(s + 1) * LANES] for s in kv_slabs])
        vwin.append([v_ref[seq_i, pl.ds(start, lk), s * LANES:(s + 1) * LANES] for s in kv_slabs])
    units = [(ci, j) for ci in range(nc) for j in range(B_KV_HEADS)]
    qstack = [jnp.concatenate([qx_ref[ci * lq:(ci + 1) * lq, (B_GROUP * j + g) * LANES:(B_GROUP * j + g + 1) * LANES]
                               for g in range(B_GROUP)], axis=0) for ci, j in units]
    scores = [_mm_nt(qstack[u], kwin[ci][j // 2]) for u, (ci, j) in enumerate(units)]
    heads = [(u, ci, j, g) for u, (ci, j) in enumerate(units) for g in range(B_GROUP)]
    sk = [sink_ref[B_GROUP * j + g] * LOG2E for _, _, j, g in heads]
    sc = [scores[u][g * lq:(g + 1) * lq] for u, _, _, g in heads]
    sc = [s if valid[ci] is None else jnp.where(valid[ci], s, -jnp.inf) for s, (_, ci, _, _) in zip(sc, heads)]
    mx = [jnp.maximum(jnp.max(s, axis=-1, keepdims=True), k) for s, k in zip(sc, sk)]
    pstack = [jnp.concatenate([jnp.exp2(sc[h] - mx[h]).astype(BF16) for h in range(u * B_GROUP, (u + 1) * B_GROUP)],
                              axis=0) for u in range(len(units))]
    ones = jnp.ones((lk, LANES), BF16)
    pv = [_mm(pstack[u], jnp.concatenate([vwin[ci][j // 2], ones], axis=1))
          for u, (ci, j) in enumerate(units)]
    out = []
    for h, (u, _, _, g) in enumerate(heads):
        part = pv[u][g * lq:(g + 1) * lq]
        out.append(part[:, :LANES] / (part[:, LANES:] + jnp.exp2(sk[h] - mx[h])))
    for ci in range(nc):
        rows = slice(ci * lq, (ci + 1) * lq)
        for slab in range(B_QW // LANES):
            pair, g = divmod(slab, B_GROUP)
            lo = out[(ci * B_KV_HEADS + 2 * pair) * B_GROUP + g]
            hi = out[(ci * B_KV_HEADS + 2 * pair + 1) * B_GROUP + g]
            both = jnp.where(lane_half == 0, lo, hi)
            zs = z_ref[rows, slab * LANES:(slab + 1) * LANES].astype(F32)
            o_ref[rows, slab * LANES:(slab + 1) * LANES] = (both * _silu(zs)).astype(o_ref.dtype)


def _attn_kernel(sink_ref, qx_ref, z_ref, k_ref, v_ref, o_ref, *, lq, lk, nc, banded):
    body = functools.partial(_attn_body, sink_ref, qx_ref, z_ref, k_ref, v_ref, o_ref, lq=lq, lk=lk, nc=nc)
    if not banded:
        body(masked=None)
        return
    step = pl.program_id(1)
    clamped_steps = -(-WINDOW_CHUNKS // nc)

    @pl.when(step < clamped_steps)
    def _():
        body(masked=True)

    @pl.when(step >= clamped_steps)
    def _():
        body(masked=False)


def _attention(qx, z, k3, v3, sinks, *, batch, seq, banded):
    lq = min(CHUNK, seq)
    ltot = k3.shape[1]
    if banded:
        nc, nseq = min(ATTN_CHUNKS, seq // lq), 1
        lk = (WINDOW_CHUNKS + 1) * CHUNK
    else:
        assert seq == lq
        nc = nseq = min(ATTN_CHUNKS, batch)
        lk = ltot
    nq = seq * nseq // (lq * nc)
    tokens = batch * seq
    row = lambda b, c: (b * nq + c, 0)
    whole = lambda b, c: (b, 0, 0)
    return pl.pallas_call(
        functools.partial(_attn_kernel, lq=lq, lk=lk, nc=nc, banded=banded),
        grid=(batch // nseq, nq),
        in_specs=[
            pl.BlockSpec(memory_space=pltpu.SMEM),
            pl.BlockSpec((nc * lq, B_QX), row),
            pl.BlockSpec((nc * lq, B_QW), row),
            pl.BlockSpec((nseq, ltot, B_KVW), whole),
            pl.BlockSpec((nseq, ltot, B_KVW), whole),
        ],
        out_specs=pl.BlockSpec((nc * lq, B_QW), row),
        out_shape=jax.ShapeDtypeStruct((tokens, B_QW), BF16),
        compiler_params=_params(("arbitrary", "arbitrary")),
        name="swa_attention",
    )(sinks, qx, z, k3, v3)


def _rope_tables(pos):
    half = ROPE_DIMS // 2
    inv = ROPE_THETA ** (-jnp.arange(half, dtype=F32) * 2.0 / ROPE_DIMS)
    ang = pos.astype(F32)[:, None] * inv[None, :]
    cos, sin = jnp.cos(ang), jnp.sin(ang)
    ones = jnp.ones((pos.shape[0], B_HD - ROPE_DIMS), F32)
    zeros_h = jnp.zeros((pos.shape[0], half), F32)
    zeros_r = jnp.zeros((pos.shape[0], B_HD - ROPE_DIMS), F32)
    cos_head = jnp.concatenate([cos, cos, ones], axis=1)
    up_head = jnp.concatenate([-sin, zeros_h, zeros_r], axis=1)
    dn_head = jnp.concatenate([zeros_h, sin, zeros_r], axis=1)
    rep = LANES // B_HD
    return tuple(jnp.tile(t, (1, rep)) for t in (cos_head, up_head, dn_head))


def _trunk(x, pos, conv_state, delta_state, past_k, past_v, wts):
    batch, seq, _ = x.shape
    tokens = batch * seq
    h = x.reshape(tokens, D_MODEL)
    tables = _rope_tables(pos)
    new_conv, new_delta = [], []
    for i in range(N_A_LAYERS):
        q, k, v, z, gb, cbuf = _a_in(h, wts["a_w_in"], wts["a_w_gate"][i], wts["a_conv_w"][i], conv_state[i],
                                     wts["a_log"][i], wts["a_dt"][i], layer=i, batch=batch, seq=seq)
        o, s_new = _delta(q, k, v, z, gb, delta_state, wts["a_norm_w"][i], layer=i, batch=batch, seq=seq)
        h = _out_ln(o, h, wts["a_w_out"], wts["a_ln_g"][i], wts["a_ln_b"][i], layer=i)
        new_conv.append(cbuf)
        new_delta.append(s_new)
    qx, z, k2, v2, k_last, v_last = _b_in(h, wts["b_w_in"], tables, layer=0, batch=batch, seq=seq,
                                          w_kv16=wts["b_w_kv"])
    k3 = k2.reshape(batch, seq, B_KVW)
    v3 = v2.reshape(batch, seq, B_KVW)
    cached = past_k is not None
    if cached:
        pk = past_k.reshape(batch, -1, B_KVW)
        pv = past_v.reshape(batch, -1, B_KVW)
        k3 = jnp.concatenate([pk.astype(BF16), k3], axis=1)
        v3 = jnp.concatenate([pv.astype(BF16), v3], axis=1)
        k_last = jnp.concatenate([pk, k_last], axis=1)[:, -WINDOW:]
        v_last = jnp.concatenate([pv, v_last], axis=1)[:, -WINDOW:]
    new_k = k_last.reshape(batch, WINDOW, B_KV_HEADS, B_HD)
    new_v = v_last.reshape(batch, WINDOW, B_KV_HEADS, B_HD)
    for j in range(N_B_LAYERS):
        if j > 0:
            qx, z = _b_in(h, wts["b_w_in"], tables, layer=j, batch=batch, seq=seq)
        o = _attention(qx, z, k3, v3, wts["b_sinks"][j], batch=batch, seq=seq, banded=not cached)
        h = _out_ln(o, h, wts["b_w_out"], wts["b_ln_g"][j], wts["b_ln_b"][j], layer=j)
    return h.reshape(batch, seq, D_MODEL), jnp.stack(new_conv), jnp.stack(new_delta), new_k, new_v


def _paired_heads(w, axis):
    order = [B_GROUP * (2 * pair + odd) + g
             for pair in range(B_KV_HEADS // 2) for g in range(B_GROUP) for odd in range(2)]
    shape = w.shape
    blocks = w.reshape(shape[:axis] + (B_Q_HEADS, B_HD) + shape[axis + 1:])
    return jnp.take(blocks, jnp.array(order, jnp.int32), axis=axis).reshape(shape)


def kernel(x_prompt, x_sample, state_delta, state_conv, cache_k, cache_v, a_w_in, a_conv_w, a_log, a_dt_bias,
           a_norm_w, a_w_out, a_ln_g, a_ln_b, b_w_kv, b_w_in, b_sinks, b_w_out, b_ln_g, b_ln_b):
    a_w_in16 = a_w_in.astype(BF16)
    zeros_h = jnp.zeros((N_A_LAYERS, A_HEADS), F32)
    wts = {
        "a_w_in": a_w_in16,
        "a_w_gate": jnp.pad(a_w_in16[:, :, A_MAIN:], ((0, 0), (0, 0), (0, LANES - 2 * A_HEADS))),
        "a_conv_w": a_conv_w,
        "a_log": jnp.concatenate([a_log.astype(F32), zeros_h], axis=1)[:, :, None],
        "a_dt": jnp.concatenate([a_dt_bias.astype(F32), zeros_h], axis=1)[:, :, None],
        "a_norm_w": a_norm_w.reshape(N_A_LAYERS, 1, A_DV),
        "a_w_out": a_w_out.astype(BF16),
        "a_ln_g": a_ln_g.reshape(N_A_LAYERS, 1, D_MODEL),
        "a_ln_b": a_ln_b.reshape(N_A_LAYERS, 1, D_MODEL),
        "b_w_kv": b_w_kv.astype(BF16),
        "b_w_in": jnp.concatenate([b_w_in[:, :, :B_QW], _paired_heads(b_w_in[:, :, B_QW:], axis=2)],
                                  axis=2).astype(BF16),
        "b_sinks": b_sinks,
        "b_w_out": _paired_heads(b_w_out, axis=1).astype(BF16),
        "b_ln_g": b_ln_g.reshape(N_B_LAYERS, 1, D_MODEL),
        "b_ln_b": b_ln_b.reshape(N_B_LAYERS, 1, D_MODEL),
    }
    bp, lp, _ = x_prompt.shape
    bs, ls, _ = x_sample.shape
    pos_prompt = jnp.arange(lp, dtype=jnp.int32)
    pos_sample = PAST_LEN + jnp.arange(ls, dtype=jnp.int32)
    zero_conv = jnp.zeros((N_A_LAYERS, bp, CONV_W - 1, CONV_DIM), F32)
    zero_delta = jnp.zeros((N_A_LAYERS, bp, A_HEADS, A_DK, A_DV), F32)
    y_p, p_conv, p_delta, p_k, p_v = _trunk(x_prompt, pos_prompt, zero_conv, zero_delta, None, None, wts)
    y_s, s_conv, s_delta, s_k, s_v = _trunk(x_sample, pos_sample, state_conv, state_delta, cache_k, cache_v, wts)
    return (y_p, y_s, p_delta, p_conv, p_k, p_v, s_delta, s_conv, s_k, s_v)
```

```python
import functools
import math

import jax
import jax.numpy as jnp
from jax import lax
from jax.experimental import pallas as pl
from jax.experimental.pallas import tpu as pltpu

D_MODEL = 1024
DEPTH = 4
PAST_LEN = 4096
CHUNK = 64
N_A_LAYERS = DEPTH // 2
N_B_LAYERS = DEPTH - N_A_LAYERS
A_HEADS = 8
A_DK = 128
A_DV = 128
A_QK = A_HEADS * A_DK
A_VW = A_HEADS * A_DV
CONV_W = 4
CONV_DIM = 2 * A_QK + A_VW
A_MAIN = CONV_DIM + A_VW
B_Q_HEADS = 16
B_KV_HEADS = 4
B_GROUP = B_Q_HEADS // B_KV_HEADS
B_HD = 64
B_QW = B_Q_HEADS * B_HD
B_KVW = B_KV_HEADS * B_HD
WINDOW = 128
WINDOW_CHUNKS = WINDOW // CHUNK
ROPE_DIMS = B_HD // 4
ROPE_THETA = 500000.0
DN_ALPHA = (2 * DEPTH) ** 0.25
LN_EPS = 1e-5
RMS_EPS = 1e-6

LANES = 128
SUBLANES = 8
VMEM_LIMIT = 48 * 1024 * 1024
ROW_TILE = 256
WIDE_TILE = 512
OUT_TILE = 1024
CONV_PIECE = 512
COL_BLOCK = 512
DELTA_BATCHES = 2
DELTA_CHUNKS = 8
ATTN_CHUNKS = 16
B_QX = B_Q_HEADS * LANES
LOG2E = math.log2(math.e)

F32 = jnp.float32
BF16 = jnp.bfloat16


def _mm(a, b):
    return jnp.dot(a.astype(BF16), b.astype(BF16), preferred_element_type=F32)


def _mm_nt(a, b):
    return lax.dot_general(a.astype(BF16), b.astype(BF16), (((1,), (1,)), ((), ())),
                           preferred_element_type=F32)


def _silu(x):
    h = 0.5 * x
    return h + h * jnp.tanh(h)


def _params(semantics):
    return pltpu.CompilerParams(dimension_semantics=semantics, vmem_limit_bytes=VMEM_LIMIT)


def _weight_specs(layer, kdim, nblocks, grid_rank):
    def spec(j):
        if grid_rank == 1:
            return pl.BlockSpec((None, kdim, COL_BLOCK), lambda i: (layer, 0, j))
        return pl.BlockSpec((None, kdim, COL_BLOCK), lambda b, l: (layer, 0, j))
    return [spec(j) for j in range(nblocks)]


def _a_in_kernel(*refs, ts, ns):
    nw = A_MAIN // COL_BLOCK
    x_ref, w_refs = refs[0], refs[1:1 + nw]
    (wg_ref, cw_ref, c0_ref, alog_ref, dt_ref,
     q_ref, k_ref, v_ref, z_ref, gb_ref, cout_ref, pbuf, carry) = refs[1 + nw:]
    tm = ts * ns
    l = pl.program_id(1)
    tail = SUBLANES - (CONV_W - 1)

    @pl.when(l == 0)
    def _():
        for s in range(ns):
            carry[s, tail:SUBLANES, :] = c0_ref[s]

    xb = x_ref[...].astype(BF16)
    cb = COL_BLOCK
    nconv = CONV_DIM // cb
    outs = (q_ref, k_ref, v_ref)

    def conv_block(s, blk):
        rows = slice(s * ts, (s + 1) * ts)
        cols = slice(blk * LANES, (blk + 1) * LANES)
        ext = jnp.concatenate([carry[s, :, cols], pbuf[rows, cols]], axis=0)
        y = None
        for j in range(CONV_W):
            back = CONV_W - 1 - j
            tap = (pltpu.roll(ext, back, axis=0) if back else ext)[SUBLANES:] * cw_ref[j:j + 1, cols]
            y = tap if y is None else y + tap
        y = _silu(y)
        which, head = divmod(blk, A_HEADS)
        if which < 2:
            y = y * lax.rsqrt(jnp.sum(y * y, axis=-1, keepdims=True) + RMS_EPS)
            if which == 0:
                y = y * (A_DK ** -0.5)
        outs[which][rows, head * LANES:(head + 1) * LANES] = y.astype(outs[which].dtype)

    pc = CONV_PIECE
    npieces = CONV_DIM // pc

    def project(i):
        j, off = divmod(i * pc, cb)
        pbuf[:, i * pc:(i + 1) * pc] = jnp.dot(xb, w_refs[j][:, off:off + pc], preferred_element_type=F32)

    def gate_z(j):
        z_ref[:, j * cb:(j + 1) * cb] = jnp.dot(
            xb, w_refs[nconv + j][...], preferred_element_type=F32).astype(z_ref.dtype)

    def decay_beta():
        gates = jnp.dot(xb, wg_ref[...], preferred_element_type=F32)
        gt = gates.T[:2 * A_HEADS]
        rowi = lax.broadcasted_iota(jnp.int32, gt.shape, 0)
        sp_in = gt + dt_ref[...]
        softplus = jnp.maximum(sp_in, 0.0) + jnp.log1p(jnp.exp(-jnp.abs(sp_in)))
        gval = -jnp.exp(alog_ref[...]) * softplus
        bval = 1.0 / (1.0 + jnp.exp(-gt))
        res = jnp.where(rowi < A_HEADS, gval, bval)
        gb_ref[...] = jnp.concatenate([res, jnp.zeros((LANES - 2 * A_HEADS, tm), F32)], axis=0).T

    light = [functools.partial(gate_z, j) for j in range(A_VW // cb)] + [decay_beta]
    project(0)
    for i in range(npieces):
        if i + 1 < npieces:
            project(i + 1)
        if light and i % (npieces // 4) == 0:
            light.pop(0)()
        for s in range(ns):
            for blk in range(i * pc // LANES, (i + 1) * pc // LANES):
                conv_block(s, blk)
    assert not light

    for s in range(ns):
        last = pbuf[(s + 1) * ts - (CONV_W - 1):(s + 1) * ts, :]
        cout_ref[s] = last
        carry[s, tail:SUBLANES, :] = last


def _a_in(x2, w_in16, w_gate, conv_w, conv0, alog_col, dt_col, *, layer, batch, seq):
    if seq >= WIDE_TILE:
        ts, ns = WIDE_TILE, 1
    else:
        ts, ns = seq, min(batch, WIDE_TILE // seq)
    tm = ts * ns
    nl = seq // ts
    tokens = batch * seq
    row = lambda b, l: (b * nl + l, 0)
    const = lambda b, l: (0, 0)
    perseq = lambda b, l: (b, 0, 0)
    wide = jax.ShapeDtypeStruct((tokens, A_QK), BF16)
    nw = A_MAIN // COL_BLOCK
    return pl.pallas_call(
        functools.partial(_a_in_kernel, ts=ts, ns=ns),
        grid=(batch // ns, nl),
        in_specs=[pl.BlockSpec((tm, D_MODEL), row)] + _weight_specs(layer, D_MODEL, nw, 2) + [
            pl.BlockSpec((D_MODEL, LANES), const),
            pl.BlockSpec((CONV_W, CONV_DIM), const),
            pl.BlockSpec((ns, CONV_W - 1, CONV_DIM), perseq),
            pl.BlockSpec((2 * A_HEADS, 1), const),
            pl.BlockSpec((2 * A_HEADS, 1), const),
        ],
        out_specs=[
            pl.BlockSpec((tm, A_QK), row),
            pl.BlockSpec((tm, A_QK), row),
            pl.BlockSpec((tm, A_VW), row),
            pl.BlockSpec((tm, A_VW), row),
            pl.BlockSpec((tm, LANES), row),
            pl.BlockSpec((ns, CONV_W - 1, CONV_DIM), perseq),
        ],
        out_shape=[wide, wide, wide, wide,
                   jax.ShapeDtypeStruct((tokens, LANES), F32),
                   jax.ShapeDtypeStruct((batch, CONV_W - 1, CONV_DIM), F32)],
        scratch_shapes=[pltpu.VMEM((tm, CONV_DIM), F32), pltpu.VMEM((ns, SUBLANES, CONV_DIM), F32)],
        compiler_params=_params(("arbitrary", "arbitrary")),
        name="a_in",
    )(x2, *([w_in16] * nw), w_gate, conv_w, conv0, alog_col, dt_col)


def _delta_group(bi, r0, q_ref, k_ref, v_ref, z_ref, gb_ref, nw, o_ref, s_ref, c):
    pk = LANES // c
    lc = int(math.log2(c))
    packs = [list(range(p * pk, (p + 1) * pk)) for p in range(A_HEADS // pk)]
    npk = range(len(packs))
    row = lax.broadcasted_iota(jnp.int32, (c, LANES), 0)
    lane = lax.broadcasted_iota(jnp.int32, (c, LANES), 1)
    colr = lane & (c - 1)
    member = lane >> lc
    eye = (row == colr).astype(F32)
    incl = row >= colr
    strict = row > colr
    diag8 = (row >> 3) == (colr >> 3)
    sq0 = lax.broadcasted_iota(jnp.int32, (LANES, LANES), 0)
    sq1 = lax.broadcasted_iota(jnp.int32, (LANES, LANES), 1)
    bd_mask = (sq0 >> lc) == (sq1 >> lc)
    kr0 = lax.broadcasted_iota(jnp.int32, (LANES, pk * LANES), 0)
    kr1 = lax.broadcasted_iota(jnp.int32, (LANES, pk * LANES), 1)
    k_mask = (kr0 >> lc) == (kr1 >> int(math.log2(LANES)))

    def bd(m):
        return jnp.where(bd_mask, jnp.concatenate([m] * pk, axis=0), 0.0)

    def by_member(vals):
        out = vals[0]
        for r in range(1, pk):
            out = jnp.where(member >= r, vals[r], out)
        return out

    def wide(col, hs):
        return jnp.concatenate([jnp.broadcast_to(col[h], (c, LANES)) for h in hs], axis=1)

    def hcol(r):
        return slice(r * LANES, (r + 1) * LANES)

    rows = slice(r0, r0 + c)
    g = gb_ref[bi, rows]
    gcum = g
    step = 1
    while step < c:
        gcum = gcum + jnp.where(row >= step, pltpu.roll(gcum, step, axis=0), 0.0)
        step *= 2
    gcum_t = jnp.concatenate([gcum] * pk, axis=0).T
    gc = [gcum[:, h:h + 1] for h in range(A_HEADS)]
    beta = [g[:, A_HEADS + h:A_HEADS + h + 1] for h in range(A_HEADS)]
    glast = [gcum[c - 1:c, h:h + 1] for h in range(A_HEADS)]
    eg = [jnp.exp(x) for x in gc]
    pcols = [slice(hs[0] * LANES, (hs[-1] + 1) * LANES) for hs in packs]
    kpf = [k_ref[bi, rows, pc].astype(F32) for pc in pcols]
    qp16 = [q_ref[bi, rows, pc] for pc in pcols]
    vpf = [v_ref[bi, rows, pc].astype(F32) for pc in pcols]
    kbeta = [kpf[p] * wide(beta, hs) for p, hs in enumerate(packs)]
    decay = [jnp.exp(jnp.where(incl, by_member([jnp.broadcast_to(gc[h], (c, LANES)) for h in hs])
                               - by_member([gcum_t[h:h + 1, :] for h in hs]), -jnp.inf))
             for hs in packs]
    yield
    k_bd =[jnp.where(k_mask, jnp.concatenate([x] * pk, axis=0), 0.0).astype(BF16) for x in kpf]
    kq = [_mm_nt(jnp.concatenate([kbeta[p].astype(BF16), qp16[p]], axis=0), k_bd[p])
          for p in npk]
    yield
    a = [jnp.where(strict, kq[p][:c] * decay[p], 0.0) for p in npk]
    d = [jnp.where(diag8, x, 0.0) for x in a]
    heads = [(p, r, hs[r]) for p, hs in enumerate(packs) for r in range(pk)]
    s = [s_ref[bi, h] for _, _, h in heads]
    ks = [_mm(jnp.concatenate([kbeta[p][:, hcol(r)] * eg[h], qp16[p][:, hcol(r)].astype(F32) * eg[h]], axis=0), s[i])
          for i, (p, r, h) in enumerate(heads)]
    yield
    d2 = [_mm(x, bd(x)) for x in d]
    yield
    d4 = [_mm(x, bd(x)) for x in d2]
    p1 = [_mm(eye - x, bd(eye + y)) for x, y in zip(d, d2)]
    yield
    xs = [_mm(p, bd(eye + y)) for p, y in zip(p1, d4)]
    yield
    shift = 3
    while (1 << shift) < c:
        mask = ((row >> (shift + 1)) == (colr >> (shift + 1))) & ((row >> shift) > (colr >> shift))
        xc = [_mm(x, bd(jnp.where(mask, m, 0.0))) for x, m in zip(xs, a)]
        yield
        xs = [x - _mm(y, bd(x)) for x, y in zip(xs, xc)]
        yield
        shift += 1
    vn = []
    for p, hs in enumerate(packs):
        vb = vpf[p] * wide(beta, hs)
        stacked = jnp.concatenate([vb[:, hcol(r)] - ks[p * pk + r][:c] for r in range(pk)], axis=0)
        vn.append(_mm(bd(xs[p]), stacked))
    yield
    v_new = [vn[p][r * c:(r + 1) * c] for p, r, _ in heads]
    intra = [kq[p][c:] * decay[p] for p in npk]
    kd_t = [(kpf[p][:, hcol(r)] * jnp.exp(glast[h] - gc[h])).T for p, r, h in heads]
    ov = [_mm(jnp.concatenate([intra[p][:, r * c:(r + 1) * c], kd_t[i]], axis=0), v_new[i])
          for i, (p, r, h) in enumerate(heads)]
    yield
    for i, (p, r, h) in enumerate(heads):
        s_ref[bi, h] = s[i] * jnp.exp(glast[h]) + ov[i][c:]
        o = ks[i][c:] + ov[i][:c]
        zf = z_ref[bi, rows, h * LANES:(h + 1) * LANES].astype(F32)
        gated = o * lax.rsqrt(jnp.mean(o * o, axis=-1, keepdims=True) + RMS_EPS) * nw * _silu(zf)
        o_ref[bi, rows, h * LANES:(h + 1) * LANES] = gated.astype(o_ref.dtype)


def _run_interleaved(gens):
    live = list(gens)
    while live:
        for gen in list(live):
            try:
                next(gen)
            except StopIteration:
                live.remove(gen)


def _delta_kernel(q_ref, k_ref, v_ref, z_ref, gb_ref, s0_ref, nw_ref, o_ref, s_ref, *, c, nb, nchunks):
    n = pl.program_id(1)

    @pl.when(n == 0)
    def _():
        s_ref[...] = s0_ref[...]

    nw = nw_ref[...]
    for ci in range(nchunks):
        _run_interleaved([_delta_group(bi, ci * c, q_ref, k_ref, v_ref, z_ref, gb_ref, nw, o_ref, s_ref, c)
                          for bi in range(nb)])


def _delta(q, k, v, z, gb, s0_all, norm_w, *, layer, batch, seq):
    c = min(CHUNK, seq)
    nchunks = min(DELTA_CHUNKS, seq // c)
    n = seq // (c * nchunks)
    nb = DELTA_BATCHES
    seq3 = lambda a: a.reshape(batch, seq, a.shape[-1])
    blk = lambda b, i: (b, i, 0)
    state = lambda b, i: (b, 0, 0, 0)
    o, s_new = pl.pallas_call(
        functools.partial(_delta_kernel, c=c, nb=nb, nchunks=nchunks),
        grid=(batch // nb, n),
        in_specs=[
            pl.BlockSpec((nb, nchunks * c, A_QK), blk),
            pl.BlockSpec((nb, nchunks * c, A_QK), blk),
            pl.BlockSpec((nb, nchunks * c, A_VW), blk),
            pl.BlockSpec((nb, nchunks * c, A_VW), blk),
            pl.BlockSpec((nb, nchunks * c, LANES), blk),
            pl.BlockSpec((None, nb, A_HEADS, A_DK, A_DV), lambda b, i: (layer, b, 0, 0, 0)),
            pl.BlockSpec((1, LANES), lambda b, i: (0, 0)),
        ],
        out_specs=[
            pl.BlockSpec((nb, nchunks * c, A_VW), blk),
            pl.BlockSpec((nb, A_HEADS, A_DK, A_DV), state),
        ],
        out_shape=[jax.ShapeDtypeStruct((batch, seq, A_VW), BF16),
                   jax.ShapeDtypeStruct((batch, A_HEADS, A_DK, A_DV), F32)],
        compiler_params=_params(("arbitrary", "arbitrary")),
        name="delta_rule",
    )(seq3(q), seq3(k), seq3(v), seq3(z), seq3(gb), s0_all, norm_w)
    return o.reshape(batch * seq, A_VW), s_new


def _out_ln_kernel(*refs, tm, sub):
    nw = D_MODEL // COL_BLOCK
    o_ref, x_ref, w_refs = refs[0], refs[1], refs[2:2 + nw]
    g_ref, b_ref, y_ref = refs[2 + nw:]
    for r0 in range(0, tm, sub):
        rows = slice(r0, r0 + sub)
        o = o_ref[rows, :]
        proj = jnp.concatenate([jnp.dot(o, w[...], preferred_element_type=F32) for w in w_refs], axis=1)
        r = DN_ALPHA * x_ref[rows, :] + proj
        mu = jnp.mean(r, axis=-1, keepdims=True)
        d = r - mu
        var = jnp.mean(d * d, axis=-1, keepdims=True)
        y_ref[rows, :] = d * lax.rsqrt(var + LN_EPS) * g_ref[...] + b_ref[...]


def _out_ln(o, x2, w_out16, ln_g, ln_b, *, layer):
    tokens = x2.shape[0]
    tm = min(OUT_TILE, tokens)
    sub = min(ROW_TILE, tm)
    row = lambda i: (i, 0)
    const = lambda i: (0, 0)
    nw = D_MODEL // COL_BLOCK
    return pl.pallas_call(
        functools.partial(_out_ln_kernel, tm=tm, sub=sub),
        grid=(tokens // tm,),
        in_specs=[pl.BlockSpec((tm, o.shape[1]), row), pl.BlockSpec((tm, D_MODEL), row)]
        + _weight_specs(layer, o.shape[1], nw, 1)
        + [pl.BlockSpec((1, D_MODEL), const), pl.BlockSpec((1, D_MODEL), const)],
        out_specs=pl.BlockSpec((tm, D_MODEL), row),
        out_shape=jax.ShapeDtypeStruct((tokens, D_MODEL), F32),
        compiler_params=_params(("arbitrary",)),
        name="out_ln",
    )(o, x2, *([w_out16] * nw), ln_g, ln_b)


def _rope_slab(x, cos_t, sin_up, sin_dn):
    half = ROPE_DIMS // 2
    return x * cos_t + pltpu.roll(x, LANES - half, axis=1) * sin_up + pltpu.roll(x, half, axis=1) * sin_dn


def _row_tiling(tokens, seq):
    tm = min(WIDE_TILE, tokens)
    if seq >= tm:
        per_seq = seq // tm
        return tm, 1, (lambda i: (i % per_seq, 0))
    return tm, tm // seq, (lambda i: (0, 0))


def _tile_tables(tables, reps):
    return tables if reps == 1 else tuple(jnp.tile(t, (reps, 1)) for t in tables)


def _b_in_kernel(*refs, tm, kv):
    nw = 2 * B_QW // COL_BLOCK
    x_ref, w_refs = refs[0], refs[1:1 + nw]
    if kv is None:
        cos_ref, sup_ref, sdn_ref, qx_ref, z_ref = refs[1 + nw:]
    else:
        wkv_ref, cos_ref, sup_ref, sdn_ref, qx_ref, z_ref, k_ref, v_ref, klast_ref, vlast_ref = refs[1 + nw:]
    xb = x_ref[...].astype(BF16)
    cos_t, sin_up, sin_dn = cos_ref[...], sup_ref[...], sdn_ref[...]
    lane_half = lax.broadcasted_iota(jnp.int32, (tm, LANES), 1) >> int(math.log2(B_HD))
    cb = COL_BLOCK
    nq = B_QW // cb
    for j in range(nq):
        proj = jnp.dot(xb, w_refs[j][...], preferred_element_type=F32)
        for sl in range(cb // LANES):
            slab = j * (cb // LANES) + sl
            rot = _rope_slab(proj[:, sl * LANES:(sl + 1) * LANES], cos_t, sin_up, sin_dn) * (B_HD ** -0.5 * LOG2E)
            for p in range(2):
                hq = 2 * slab + p
                x = jnp.where(lane_half == p, rot, 0.0)
                if p != (hq // B_GROUP) % 2:
                    x = pltpu.roll(x, B_HD, axis=1)
                qx_ref[:, hq * LANES:(hq + 1) * LANES] = x.astype(qx_ref.dtype)
    for j in range(nq):
        z_ref[:, j * cb:(j + 1) * cb] = jnp.dot(
            xb, w_refs[nq + j][...], preferred_element_type=F32).astype(z_ref.dtype)
    if kv is not None:
        rows_per_seq, ns, keep = kv
        kvp = jnp.dot(xb, wkv_ref[...], preferred_element_type=F32)
        k = jnp.concatenate([_rope_slab(kvp[:, s * LANES:(s + 1) * LANES], cos_t, sin_up, sin_dn)
                             for s in range(B_KVW // LANES)], axis=1)
        v = kvp[:, B_KVW:]
        k_ref[...] = k.astype(k_ref.dtype)
        v_ref[...] = v.astype(v_ref.dtype)
        for s in range(ns):
            end = (s + 1) * rows_per_seq
            klast_ref[s] = k[end - keep:end, :]
            vlast_ref[s] = v[end - keep:end, :]


def _b_in(x2, w_in16, tables, *, layer, batch, seq, w_kv16=None):
    tokens = batch * seq
    tm, ns, tab = _row_tiling(tokens, seq)
    row = lambda i: (i, 0)
    nw = 2 * B_QW // COL_BLOCK
    in_specs = [pl.BlockSpec((tm, D_MODEL), row)] + _weight_specs(layer, D_MODEL, nw, 1)
    out_specs = [pl.BlockSpec((tm, B_QX), row), pl.BlockSpec((tm, B_QW), row)]
    out_shape = [jax.ShapeDtypeStruct((tokens, B_QX), BF16), jax.ShapeDtypeStruct((tokens, B_QW), BF16)]
    operands = [x2] + [w_in16] * nw
    kv = None
    if w_kv16 is not None:
        keep = min(WINDOW, seq)
        rows_per_seq = tm // ns
        assert keep <= rows_per_seq
        per_seq = max(1, seq // tm)
        lastb = lambda i: (i // per_seq, 0, 0)
        kv = (rows_per_seq, ns, keep)
        in_specs.append(pl.BlockSpec((D_MODEL, 2 * B_KVW), lambda i: (0, 0)))
        operands.append(w_kv16)
        out_specs += [pl.BlockSpec((tm, B_KVW), row), pl.BlockSpec((tm, B_KVW), row),
                      pl.BlockSpec((ns, keep, B_KVW), lastb), pl.BlockSpec((ns, keep, B_KVW), lastb)]
        out_shape += [jax.ShapeDtypeStruct((tokens, B_KVW), BF16)] * 2
        out_shape += [jax.ShapeDtypeStruct((batch, keep, B_KVW), F32)] * 2
    in_specs += [pl.BlockSpec((tm, LANES), tab)] * 3
    return pl.pallas_call(
        functools.partial(_b_in_kernel, tm=tm, kv=kv),
        grid=(tokens // tm,),
        in_specs=in_specs,
        out_specs=out_specs,
        out_shape=out_shape,
        compiler_params=_params(("arbitrary",)),
        name="b_in",
    )(*operands, *_tile_tables(tables, ns))


def _attn_body(sink_ref, qx_ref, z_ref, k_ref, v_ref, o_ref, *, lq, lk, nc, masked):
    step = pl.program_id(1)
    lane_half = lax.broadcasted_iota(jnp.int32, (lq, LANES), 1) >> int(math.log2(B_HD))
    kv_slabs = range(B_KVW // LANES)
    kwin, vwin, valid = [], [], []
    for ci in range(nc):
        if masked is None:
            start, seq_i = 0, ci
            valid.append(None)
        else:
            seq_i = 0
            c = step * nc + ci
            first = jnp.maximum(c - WINDOW_CHUNKS, 0)
            start = pl.multiple_of(first * CHUNK, CHUNK)
            key_chunk = first + (lax.broadcasted_iota(jnp.int32, (lq, lk), 1) >> int(math.log2(CHUNK)))
            valid.append((key_chunk <= c) if masked else None)
        kwin.append([k_ref[seq_i, pl.ds(start, lk), s * LANES:(s + 1) * LANES] for s in kv_slabs])
        vwin.append([v_ref[seq_i, pl.ds(start, lk), s * LANES:(s + 1) * LANES] for s in kv_slabs])
    units = [(ci, j) for ci in range(nc) for j in range(B_KV_HEADS)]
    qstack = [jnp.concatenate([qx_ref[ci * lq:(ci + 1) * lq, (B_GROUP * j + g) * LANES:(B_GROUP * j + g + 1) * LANES]
                               for g in range(B_GROUP)], axis=0) for ci, j in units]
    scores = [_mm_nt(qstack[u], kwin[ci][j // 2]) for u, (ci, j) in enumerate(units)]
    heads = [(u, ci, j, g) for u, (ci, j) in enumerate(units) for g in range(B_GROUP)]
    sk = [sink_ref[B_GROUP * j + g] * LOG2E for _, _, j, g in heads]
    sc = [scores[u][g * lq:(g + 1) * lq] for u, _, _, g in heads]
    sc = [s if valid[ci] is None else jnp.where(valid[ci], s, -jnp.inf) for s, (_, ci, _, _) in zip(sc, heads)]
    mx = [jnp.maximum(jnp.max(s, axis=-1, keepdims=True), k) for s, k in zip(sc, sk)]
    pstack = [jnp.concatenate([jnp.exp2(sc[h] - mx[h]).astype(BF16) for h in range(u * B_GROUP, (u + 1) * B_GROUP)],
                              axis=0) for u in range(len(units))]
    ones = jnp.ones((lk, LANES), BF16)
    pv = [_mm(pstack[u], jnp.concatenate([vwin[ci][j // 2], ones], axis=1))
          for u, (ci, j) in enumerate(units)]
    out = []
    for h, (u, _, _, g) in enumerate(heads):
        part = pv[u][g * lq:(g + 1) * lq]
        out.append(part[:, :LANES] / (part[:, LANES:] + jnp.exp2(sk[h] - mx[h])))
    for ci in range(nc):
        rows = slice(ci * lq, (ci + 1) * lq)
        for slab in range(B_QW // LANES):
            pair, g = divmod(slab, B_GROUP)
            lo = out[(ci * B_KV_HEADS + 2 * pair) * B_GROUP + g]
            hi = out[(ci * B_KV_HEADS + 2 * pair + 1) * B_GROUP + g]
            both = jnp.where(lane_half == 0, lo, hi)
            zs = z_ref[rows, slab * LANES:(slab + 1) * LANES].astype(F32)
            o_ref[rows, slab * LANES:(slab + 1) * LANES] = (both * _silu(zs)).astype(o_ref.dtype)


def _attn_kernel(sink_ref, qx_ref, z_ref, k_ref, v_ref, o_ref, *, lq, lk, nc, banded):
    body = functools.partial(_attn_body, sink_ref, qx_ref, z_ref, k_ref, v_ref, o_ref, lq=lq, lk=lk, nc=nc)
    if not banded:
        body(masked=None)
        return
    step = pl.program_id(1)
    clamped_steps = -(-WINDOW_CHUNKS // nc)

    @pl.when(step < clamped_steps)
    def _():
        body(masked=True)

    @pl.when(step >= clamped_steps)
    def _():
        body(masked=False)


def _attention(qx, z, k3, v3, sinks, *, batch, seq, banded):
    lq = min(CHUNK, seq)
    ltot = k3.shape[1]
    if banded:
        nc, nseq = min(ATTN_CHUNKS, seq // lq), 1
        lk = (WINDOW_CHUNKS + 1) * CHUNK
    else:
        assert seq == lq
        nc = nseq = min(ATTN_CHUNKS, batch)
        lk = ltot
    nq = seq * nseq // (lq * nc)
    tokens = batch * seq
    row = lambda b, c: (b * nq + c, 0)
    whole = lambda b, c: (b, 0, 0)
    return pl.pallas_call(
        functools.partial(_attn_kernel, lq=lq, lk=lk, nc=nc, banded=banded),
        grid=(batch // nseq, nq),
        in_specs=[
            pl.BlockSpec(memory_space=pltpu.SMEM),
            pl.BlockSpec((nc * lq, B_QX), row),
            pl.BlockSpec((nc * lq, B_QW), row),
            pl.BlockSpec((nseq, ltot, B_KVW), whole),
            pl.BlockSpec((nseq, ltot, B_KVW), whole),
        ],
        out_specs=pl.BlockSpec((nc * lq, B_QW), row),
        out_shape=jax.ShapeDtypeStruct((tokens, B_QW), BF16),
        compiler_params=_params(("arbitrary", "arbitrary")),
        name="swa_attention",
    )(sinks, qx, z, k3, v3)


def _rope_tables(pos):
    half = ROPE_DIMS // 2
    inv = ROPE_THETA ** (-jnp.arange(half, dtype=F32) * 2.0 / ROPE_DIMS)
    ang = pos.astype(F32)[:, None] * inv[None, :]
    cos, sin = jnp.cos(ang), jnp.sin(ang)
    ones = jnp.ones((pos.shape[0], B_HD - ROPE_DIMS), F32)
    zeros_h = jnp.zeros((pos.shape[0], half), F32)
    zeros_r = jnp.zeros((pos.shape[0], B_HD - ROPE_DIMS), F32)
    cos_head = jnp.concatenate([cos, cos, ones], axis=1)
    up_head = jnp.concatenate([-sin, zeros_h, zeros_r], axis=1)
    dn_head = jnp.concatenate([zeros_h, sin, zeros_r], axis=1)
    rep = LANES // B_HD
    return tuple(jnp.tile(t, (1, rep)) for t in (cos_head, up_head, dn_head))


def _trunk(x, pos, conv_state, delta_state, past_k, past_v, wts):
    batch, seq, _ = x.shape
    tokens = batch * seq
    h = x.reshape(tokens, D_MODEL)
    tables = _rope_tables(pos)
    new_conv, new_delta = [], []
    for i in range(N_A_LAYERS):
        q, k, v, z, gb, cbuf = _a_in(h, wts["a_w_in"], wts["a_w_gate"][i], wts["a_conv_w"][i], conv_state[i],
                                     wts["a_log"][i], wts["a_dt"][i], layer=i, batch=batch, seq=seq)
        o, s_new = _delta(q, k, v, z, gb, delta_state, wts["a_norm_w"][i], layer=i, batch=batch, seq=seq)
        h = _out_ln(o, h, wts["a_w_out"], wts["a_ln_g"][i], wts["a_ln_b"][i], layer=i)
        new_conv.append(cbuf)
        new_delta.append(s_new)
    qx, z, k2, v2, k_last, v_last = _b_in(h, wts["b_w_in"], tables, layer=0, batch=batch, seq=seq,
                                          w_kv16=wts["b_w_kv"])
    k3 = k2.reshape(batch, seq, B_KVW)
    v3 = v2.reshape(batch, seq, B_KVW)
    cached = past_k is not None
    if cached:
        pk = past_k.reshape(batch, -1, B_KVW)
        pv = past_v.reshape(batch, -1, B_KVW)
        k3 = jnp.concatenate([pk.astype(BF16), k3], axis=1)
        v3 = jnp.concatenate([pv.astype(BF16), v3], axis=1)
        k_last = jnp.concatenate([pk, k_last], axis=1)[:, -WINDOW:]
        v_last = jnp.concatenate([pv, v_last], axis=1)[:, -WINDOW:]
    new_k = k_last.reshape(batch, WINDOW, B_KV_HEADS, B_HD)
    new_v = v_last.reshape(batch, WINDOW, B_KV_HEADS, B_HD)
    for j in range(N_B_LAYERS):
        if j > 0:
            qx, z = _b_in(h, wts["b_w_in"], tables, layer=j, batch=batch, seq=seq)
        o = _attention(qx, z, k3, v3, wts["b_sinks"][j], batch=batch, seq=seq, banded=not cached)
        h = _out_ln(o, h, wts["b_w_out"], wts["b_ln_g"][j], wts["b_ln_b"][j], layer=j)
    return h.reshape(batch, seq, D_MODEL), jnp.stack(new_conv), jnp.stack(new_delta), new_k, new_v


def _paired_heads(w, axis):
    order = [B_GROUP * (2 * pair + odd) + g
             for pair in range(B_KV_HEADS // 2) for g in range(B_GROUP) for odd in range(2)]
    shape = w.shape
    blocks = w.reshape(shape[:axis] + (B_Q_HEADS, B_HD) + shape[axis + 1:])
    return jnp.take(blocks, jnp.array(order, jnp.int32), axis=axis).reshape(shape)


def kernel(x_prompt, x_sample, state_delta, state_conv, cache_k, cache_v, a_w_in, a_conv_w, a_log, a_dt_bias,
           a_norm_w, a_w_out, a_ln_g, a_ln_b, b_w_kv, b_w_in, b_sinks, b_w_out, b_ln_g, b_ln_b):
    a_w_in16 = a_w_in.astype(BF16)
    zeros_h = jnp.zeros((N_A_LAYERS, A_HEADS), F32)
    wts = {
        "a_w_in": a_w_in16,
        "a_w_gate": jnp.pad(a_w_in16[:, :, A_MAIN:], ((0, 0), (0, 0), (0, LANES - 2 * A_HEADS))),
        "a_conv_w": a_conv_w,
        "a_log": jnp.concatenate([a_log.astype(F32), zeros_h], axis=1)[:, :, None],
        "a_dt": jnp.concatenate([a_dt_bias.astype(F32), zeros_h], axis=1)[:, :, None],
        "a_norm_w": a_norm_w.reshape(N_A_LAYERS, 1, A_DV),
        "a_w_out": a_w_out.astype(BF16),
        "a_ln_g": a_ln_g.reshape(N_A_LAYERS, 1, D_MODEL),
        "a_ln_b": a_ln_b.reshape(N_A_LAYERS, 1, D_MODEL),
        "b_w_kv": b_w_kv.astype(BF16),
        "b_w_in": jnp.concatenate([b_w_in[:, :, :B_QW], _paired_heads(b_w_in[:, :, B_QW:], axis=2)],
                                  axis=2).astype(BF16),
        "b_sinks": b_sinks,
        "b_w_out": _paired_heads(b_w_out, axis=1).astype(BF16),
        "b_ln_g": b_ln_g.reshape(N_B_LAYERS, 1, D_MODEL),
        "b_ln_b": b_ln_b.reshape(N_B_LAYERS, 1, D_MODEL),
    }
    bp, lp, _ = x_prompt.shape
    bs, ls, _ = x_sample.shape
    pos_prompt = jnp.arange(lp, dtype=jnp.int32)
    pos_sample = PAST_LEN + jnp.arange(ls, dtype=jnp.int32)
    zero_conv = jnp.zeros((N_A_LAYERS, bp, CONV_W - 1, CONV_DIM), F32)
    zero_delta = jnp.zeros((N_A_LAYERS, bp, A_HEADS, A_DK, A_DV), F32)
    y_p, p_conv, p_delta, p_k, p_v = _trunk(x_prompt, pos_prompt, zero_conv, zero_delta, None, None, wts)
    y_s, s_conv, s_delta, s_k, s_v = _trunk(x_sample, pos_sample, state_conv, state_delta, cache_k, cache_v, wts)
    return (y_p, y_s, p_delta, p_conv, p_k, p_v, s_delta, s_conv, s_k, s_v)
```

```python
import functools
import math

import jax
import jax.numpy as jnp
from jax import lax
from jax.experimental import pallas as pl
from jax.experimental.pallas import tpu as pltpu

D_MODEL = 1024
DEPTH = 4
PAST_LEN = 4096
CHUNK = 64
N_A_LAYERS = DEPTH // 2
N_B_LAYERS = DEPTH - N_A_LAYERS
A_HEADS = 8
A_DK = 128
A_DV = 128
A_QK = A_HEADS * A_DK
A_VW = A_HEADS * A_DV
CONV_W = 4
CONV_DIM = 2 * A_QK + A_VW
A_MAIN = CONV_DIM + A_VW
B_Q_HEADS = 16
B_KV_HEADS = 4
B_GROUP = B_Q_HEADS // B_KV_HEADS
B_HD = 64
B_QW = B_Q_HEADS * B_HD
B_KVW = B_KV_HEADS * B_HD
WINDOW = 128
WINDOW_CHUNKS = WINDOW // CHUNK
ROPE_DIMS = B_HD // 4
ROPE_THETA = 500000.0
DN_ALPHA = (2 * DEPTH) ** 0.25
LN_EPS = 1e-5
RMS_EPS = 1e-6

LANES = 128
SUBLANES = 8
VMEM_LIMIT = 48 * 1024 * 1024
ROW_TILE = 256
WIDE_TILE = 512
OUT_TILE = 1024
CONV_PIECE = 512
COL_BLOCK = 512
DELTA_BATCHES = 2
DELTA_CHUNKS = 8
ATTN_CHUNKS = 16
B_QX = B_Q_HEADS * LANES
B_VX = B_KV_HEADS * LANES
LOG2E = math.log2(math.e)

F32 = jnp.float32
BF16 = jnp.bfloat16


def _mm(a, b):
    return jnp.dot(a.astype(BF16), b.astype(BF16), preferred_element_type=F32)


def _mm_nt(a, b):
    return lax.dot_general(a.astype(BF16), b.astype(BF16), (((1,), (1,)), ((), ())),
                           preferred_element_type=F32)


def _silu(x):
    h = 0.5 * x
    return h + h * jnp.tanh(h)


def _params(semantics):
    return pltpu.CompilerParams(dimension_semantics=semantics, vmem_limit_bytes=VMEM_LIMIT)


def _weight_specs(layer, kdim, nblocks, grid_rank):
    def spec(j):
        if grid_rank == 1:
            return pl.BlockSpec((None, kdim, COL_BLOCK), lambda i: (layer, 0, j))
        return pl.BlockSpec((None, kdim, COL_BLOCK), lambda b, l: (layer, 0, j))
    return [spec(j) for j in range(nblocks)]


def _a_in_kernel(*refs, ts, ns):
    nw = A_MAIN // COL_BLOCK
    x_ref, w_refs = refs[0], refs[1:1 + nw]
    (wg_ref, cw_ref, c0_ref, alog_ref, dt_ref,
     q_ref, k_ref, v_ref, z_ref, gb_ref, cout_ref, pbuf, carry) = refs[1 + nw:]
    tm = ts * ns
    l = pl.program_id(1)
    tail = SUBLANES - (CONV_W - 1)

    @pl.when(l == 0)
    def _():
        for s in range(ns):
            carry[s, tail:SUBLANES, :] = c0_ref[s]

    xb = x_ref[...].astype(BF16)
    cb = COL_BLOCK
    nconv = CONV_DIM // cb
    outs = (q_ref, k_ref, v_ref)

    def conv_block(s, blk):
        rows = slice(s * ts, (s + 1) * ts)
        cols = slice(blk * LANES, (blk + 1) * LANES)
        ext = jnp.concatenate([carry[s, :, cols], pbuf[rows, cols]], axis=0)
        y = None
        for j in range(CONV_W):
            back = CONV_W - 1 - j
            tap = (pltpu.roll(ext, back, axis=0) if back else ext)[SUBLANES:] * cw_ref[j:j + 1, cols]
            y = tap if y is None else y + tap
        y = _silu(y)
        which, head = divmod(blk, A_HEADS)
        if which < 2:
            y = y * lax.rsqrt(jnp.sum(y * y, axis=-1, keepdims=True) + RMS_EPS)
            if which == 0:
                y = y * (A_DK ** -0.5)
        outs[which][rows, head * LANES:(head + 1) * LANES] = y.astype(outs[which].dtype)

    pc = CONV_PIECE
    npieces = CONV_DIM // pc

    def project(i):
        j, off = divmod(i * pc, cb)
        pbuf[:, i * pc:(i + 1) * pc] = jnp.dot(xb, w_refs[j][:, off:off + pc], preferred_element_type=F32)

    def gate_z(j):
        z_ref[:, j * cb:(j + 1) * cb] = jnp.dot(
            xb, w_refs[nconv + j][...], preferred_element_type=F32).astype(z_ref.dtype)

    def decay_beta():
        gates = jnp.dot(xb, wg_ref[...], preferred_element_type=F32)
        gt = gates.T[:2 * A_HEADS]
        rowi = lax.broadcasted_iota(jnp.int32, gt.shape, 0)
        sp_in = gt + dt_ref[...]
        softplus = jnp.maximum(sp_in, 0.0) + jnp.log1p(jnp.exp(-jnp.abs(sp_in)))
        gval = -jnp.exp(alog_ref[...]) * softplus
        bval = 1.0 / (1.0 + jnp.exp(-gt))
        res = jnp.where(rowi < A_HEADS, gval, bval)
        gb_ref[...] = jnp.concatenate([res, jnp.zeros((LANES - 2 * A_HEADS, tm), F32)], axis=0).T

    light = [functools.partial(gate_z, j) for j in range(A_VW // cb)] + [decay_beta]
    project(0)
    for i in range(npieces):
        if i + 1 < npieces:
            project(i + 1)
        if light and i % (npieces // 4) == 0:
            light.pop(0)()
        for s in range(ns):
            for blk in range(i * pc // LANES, (i + 1) * pc // LANES):
                conv_block(s, blk)
    assert not light

    for s in range(ns):
        last = pbuf[(s + 1) * ts - (CONV_W - 1):(s + 1) * ts, :]
        cout_ref[s] = last
        carry[s, tail:SUBLANES, :] = last


def _a_in(x2, w_in16, w_gate, conv_w, conv0, alog_col, dt_col, *, layer, batch, seq):
    if seq >= WIDE_TILE:
        ts, ns = WIDE_TILE, 1
    else:
        ts, ns = seq, min(batch, WIDE_TILE // seq)
    tm = ts * ns
    nl = seq // ts
    tokens = batch * seq
    row = lambda b, l: (b * nl + l, 0)
    const = lambda b, l: (0, 0)
    perseq = lambda b, l: (b, 0, 0)
    wide = jax.ShapeDtypeStruct((tokens, A_QK), BF16)
    nw = A_MAIN // COL_BLOCK
    return pl.pallas_call(
        functools.partial(_a_in_kernel, ts=ts, ns=ns),
        grid=(batch // ns, nl),
        in_specs=[pl.BlockSpec((tm, D_MODEL), row)] + _weight_specs(layer, D_MODEL, nw, 2) + [
            pl.BlockSpec((D_MODEL, LANES), const),
            pl.BlockSpec((CONV_W, CONV_DIM), const),
            pl.BlockSpec((ns, CONV_W - 1, CONV_DIM), perseq),
            pl.BlockSpec((2 * A_HEADS, 1), const),
            pl.BlockSpec((2 * A_HEADS, 1), const),
        ],
        out_specs=[
            pl.BlockSpec((tm, A_QK), row),
            pl.BlockSpec((tm, A_QK), row),
            pl.BlockSpec((tm, A_VW), row),
            pl.BlockSpec((tm, A_VW), row),
            pl.BlockSpec((tm, LANES), row),
            pl.BlockSpec((ns, CONV_W - 1, CONV_DIM), perseq),
        ],
        out_shape=[wide, wide, wide, wide,
                   jax.ShapeDtypeStruct((tokens, LANES), F32),
                   jax.ShapeDtypeStruct((batch, CONV_W - 1, CONV_DIM), F32)],
        scratch_shapes=[pltpu.VMEM((tm, CONV_DIM), F32), pltpu.VMEM((ns, SUBLANES, CONV_DIM), F32)],
        compiler_params=_params(("arbitrary", "arbitrary")),
        name="a_in",
    )(x2, *([w_in16] * nw), w_gate, conv_w, conv0, alog_col, dt_col)


def _delta_group(bi, r0, q_ref, k_ref, v_ref, z_ref, gb_ref, nw, o_ref, s_ref, c):
    pk = LANES // c
    lc = int(math.log2(c))
    packs = [list(range(p * pk, (p + 1) * pk)) for p in range(A_HEADS // pk)]
    npk = range(len(packs))
    row = lax.broadcasted_iota(jnp.int32, (c, LANES), 0)
    lane = lax.broadcasted_iota(jnp.int32, (c, LANES), 1)
    colr = lane & (c - 1)
    member = lane >> lc
    eye = (row == colr).astype(F32)
    incl = row >= colr
    strict = row > colr
    diag8 = (row >> 3) == (colr >> 3)
    sq0 = lax.broadcasted_iota(jnp.int32, (LANES, LANES), 0)
    sq1 = lax.broadcasted_iota(jnp.int32, (LANES, LANES), 1)
    bd_mask = (sq0 >> lc) == (sq1 >> lc)
    kr0 = lax.broadcasted_iota(jnp.int32, (LANES, pk * LANES), 0)
    kr1 = lax.broadcasted_iota(jnp.int32, (LANES, pk * LANES), 1)
    k_mask = (kr0 >> lc) == (kr1 >> int(math.log2(LANES)))

    def bd(m):
        return jnp.where(bd_mask, jnp.concatenate([m] * pk, axis=0), 0.0)

    def by_member(vals):
        out = vals[0]
        for r in range(1, pk):
            out = jnp.where(member >= r, vals[r], out)
        return out

    def wide(col, hs):
        return jnp.concatenate([jnp.broadcast_to(col[h], (c, LANES)) for h in hs], axis=1)

    def hcol(r):
        return slice(r * LANES, (r + 1) * LANES)

    rows = slice(r0, r0 + c)
    g = gb_ref[bi, rows]
    gcum = g
    step = 1
    while step < c:
        gcum = gcum + jnp.where(row >= step, pltpu.roll(gcum, step, axis=0), 0.0)
        step *= 2
    gcum_t = jnp.concatenate([gcum] * pk, axis=0).T
    gc = [gcum[:, h:h + 1] for h in range(A_HEADS)]
    beta = [g[:, A_HEADS + h:A_HEADS + h + 1] for h in range(A_HEADS)]
    glast = [gcum[c - 1:c, h:h + 1] for h in range(A_HEADS)]
    eg = [jnp.exp(x) for x in gc]
    pcols = [slice(hs[0] * LANES, (hs[-1] + 1) * LANES) for hs in packs]
    kpf = [k_ref[bi, rows, pc].astype(F32) for pc in pcols]
    qp16 = [q_ref[bi, rows, pc] for pc in pcols]
    vpf = [v_ref[bi, rows, pc].astype(F32) for pc in pcols]
    kbeta = [kpf[p] * wide(beta, hs) for p, hs in enumerate(packs)]
    decay = [jnp.exp(jnp.where(incl, by_member([jnp.broadcast_to(gc[h], (c, LANES)) for h in hs])
                               - by_member([gcum_t[h:h + 1, :] for h in hs]), -jnp.inf))
             for hs in packs]
    yield
    k_bd =[jnp.where(k_mask, jnp.concatenate([x] * pk, axis=0), 0.0).astype(BF16) for x in kpf]
    kq = [_mm_nt(jnp.concatenate([kbeta[p].astype(BF16), qp16[p]], axis=0), k_bd[p])
          for p in npk]
    yield
    a = [jnp.where(strict, kq[p][:c] * decay[p], 0.0) for p in npk]
    d = [jnp.where(diag8, x, 0.0) for x in a]
    heads = [(p, r, hs[r]) for p, hs in enumerate(packs) for r in range(pk)]
    s = [s_ref[bi, h] for _, _, h in heads]
    ks = [_mm(jnp.concatenate([kbeta[p][:, hcol(r)] * eg[h], qp16[p][:, hcol(r)].astype(F32) * eg[h]], axis=0), s[i])
          for i, (p, r, h) in enumerate(heads)]
    yield
    d2 = [_mm(x, bd(x)) for x in d]
    yield
    d4 = [_mm(x, bd(x)) for x in d2]
    p1 = [_mm(eye - x, bd(eye + y)) for x, y in zip(d, d2)]
    yield
    xs = [_mm(p, bd(eye + y)) for p, y in zip(p1, d4)]
    yield
    shift = 3
    while (1 << shift) < c:
        mask = ((row >> (shift + 1)) == (colr >> (shift + 1))) & ((row >> shift) > (colr >> shift))
        xc = [_mm(x, bd(jnp.where(mask, m, 0.0))) for x, m in zip(xs, a)]
        yield
        xs = [x - _mm(y, bd(x)) for x, y in zip(xs, xc)]
        yield
        shift += 1
    vn = []
    for p, hs in enumerate(packs):
        vb = vpf[p] * wide(beta, hs)
        stacked = jnp.concatenate([vb[:, hcol(r)] - ks[p * pk + r][:c] for r in range(pk)], axis=0)
        vn.append(_mm(bd(xs[p]), stacked))
    yield
    v_new = [vn[p][r * c:(r + 1) * c] for p, r, _ in heads]
    intra = [kq[p][c:] * decay[p] for p in npk]
    kd_t = [(kpf[p][:, hcol(r)] * jnp.exp(glast[h] - gc[h])).T for p, r, h in heads]
    ov = [_mm(jnp.concatenate([intra[p][:, r * c:(r + 1) * c], kd_t[i]], axis=0), v_new[i])
          for i, (p, r, h) in enumerate(heads)]
    yield
    for i, (p, r, h) in enumerate(heads):
        s_ref[bi, h] = s[i] * jnp.exp(glast[h]) + ov[i][c:]
        o = ks[i][c:] + ov[i][:c]
        zf = z_ref[bi, rows, h * LANES:(h + 1) * LANES].astype(F32)
        gated = o * lax.rsqrt(jnp.mean(o * o, axis=-1, keepdims=True) + RMS_EPS) * nw * _silu(zf)
        o_ref[bi, rows, h * LANES:(h + 1) * LANES] = gated.astype(o_ref.dtype)


def _run_interleaved(gens):
    live = list(gens)
    while live:
        for gen in list(live):
            try:
                next(gen)
            except StopIteration:
                live.remove(gen)


def _delta_kernel(q_ref, k_ref, v_ref, z_ref, gb_ref, s0_ref, nw_ref, o_ref, s_ref, *, c, nb, nchunks):
    n = pl.program_id(1)

    @pl.when(n == 0)
    def _():
        s_ref[...] = s0_ref[...]

    nw = nw_ref[...]
    for ci in range(nchunks):
        _run_interleaved([_delta_group(bi, ci * c, q_ref, k_ref, v_ref, z_ref, gb_ref, nw, o_ref, s_ref, c)
                          for bi in range(nb)])


def _delta(q, k, v, z, gb, s0_all, norm_w, *, layer, batch, seq):
    c = min(CHUNK, seq)
    nchunks = min(DELTA_CHUNKS, seq // c)
    n = seq // (c * nchunks)
    nb = DELTA_BATCHES
    seq3 = lambda a: a.reshape(batch, seq, a.shape[-1])
    blk = lambda b, i: (b, i, 0)
    state = lambda b, i: (b, 0, 0, 0)
    o, s_new = pl.pallas_call(
        functools.partial(_delta_kernel, c=c, nb=nb, nchunks=nchunks),
        grid=(batch // nb, n),
        in_specs=[
            pl.BlockSpec((nb, nchunks * c, A_QK), blk),
            pl.BlockSpec((nb, nchunks * c, A_QK), blk),
            pl.BlockSpec((nb, nchunks * c, A_VW), blk),
            pl.BlockSpec((nb, nchunks * c, A_VW), blk),
            pl.BlockSpec((nb, nchunks * c, LANES), blk),
            pl.BlockSpec((None, nb, A_HEADS, A_DK, A_DV), lambda b, i: (layer, b, 0, 0, 0)),
            pl.BlockSpec((1, LANES), lambda b, i: (0, 0)),
        ],
        out_specs=[
            pl.BlockSpec((nb, nchunks * c, A_VW), blk),
            pl.BlockSpec((nb, A_HEADS, A_DK, A_DV), state),
        ],
        out_shape=[jax.ShapeDtypeStruct((batch, seq, A_VW), BF16),
                   jax.ShapeDtypeStruct((batch, A_HEADS, A_DK, A_DV), F32)],
        compiler_params=_params(("arbitrary", "arbitrary")),
        name="delta_rule",
    )(seq3(q), seq3(k), seq3(v), seq3(z), seq3(gb), s0_all, norm_w)
    return o.reshape(batch * seq, A_VW), s_new


def _out_ln_kernel(*refs, tm, sub):
    nw = D_MODEL // COL_BLOCK
    o_ref, x_ref, w_refs = refs[0], refs[1], refs[2:2 + nw]
    g_ref, b_ref, y_ref = refs[2 + nw:]
    for r0 in range(0, tm, sub):
        rows = slice(r0, r0 + sub)
        o = o_ref[rows, :]
        proj = jnp.concatenate([jnp.dot(o, w[...], preferred_element_type=F32) for w in w_refs], axis=1)
        r = DN_ALPHA * x_ref[rows, :] + proj
        mu = jnp.mean(r, axis=-1, keepdims=True)
        d = r - mu
        var = jnp.mean(d * d, axis=-1, keepdims=True)
        y_ref[rows, :] = d * lax.rsqrt(var + LN_EPS) * g_ref[...] + b_ref[...]


def _out_ln(o, x2, w_out16, ln_g, ln_b, *, layer):
    tokens = x2.shape[0]
    tm = min(OUT_TILE, tokens)
    sub = min(ROW_TILE, tm)
    row = lambda i: (i, 0)
    const = lambda i: (0, 0)
    nw = D_MODEL // COL_BLOCK
    return pl.pallas_call(
        functools.partial(_out_ln_kernel, tm=tm, sub=sub),
        grid=(tokens // tm,),
        in_specs=[pl.BlockSpec((tm, o.shape[1]), row), pl.BlockSpec((tm, D_MODEL), row)]
        + _weight_specs(layer, o.shape[1], nw, 1)
        + [pl.BlockSpec((1, D_MODEL), const), pl.BlockSpec((1, D_MODEL), const)],
        out_specs=pl.BlockSpec((tm, D_MODEL), row),
        out_shape=jax.ShapeDtypeStruct((tokens, D_MODEL), F32),
        compiler_params=_params(("arbitrary",)),
        name="out_ln",
    )(o, x2, *([w_out16] * nw), ln_g, ln_b)


def _rope_slab(x, cos_t, sin_up, sin_dn):
    half = ROPE_DIMS // 2
    return x * cos_t + pltpu.roll(x, LANES - half, axis=1) * sin_up + pltpu.roll(x, half, axis=1) * sin_dn


def _row_tiling(tokens, seq):
    tm = min(WIDE_TILE, tokens)
    if seq >= tm:
        per_seq = seq // tm
        return tm, 1, (lambda i: (i % per_seq, 0))
    return tm, tm // seq, (lambda i: (0, 0))


def _tile_tables(tables, reps):
    return tables if reps == 1 else tuple(jnp.tile(t, (reps, 1)) for t in tables)


def _b_in_kernel(*refs, tm, kv):
    nw = 2 * B_QW // COL_BLOCK
    x_ref, w_refs = refs[0], refs[1:1 + nw]
    if kv is None:
        cos_ref, sup_ref, sdn_ref, qx_ref, z_ref = refs[1 + nw:]
    else:
        wkv_ref, cos_ref, sup_ref, sdn_ref, qx_ref, z_ref, k_ref, v_ref, klast_ref, vlast_ref = refs[1 + nw:]
    xb = x_ref[...].astype(BF16)
    cos_t, sin_up, sin_dn = cos_ref[...], sup_ref[...], sdn_ref[...]
    lane_half = lax.broadcasted_iota(jnp.int32, (tm, LANES), 1) >> int(math.log2(B_HD))
    cb = COL_BLOCK
    nq = B_QW // cb
    for j in range(nq):
        proj = jnp.dot(xb, w_refs[j][...], preferred_element_type=F32)
        for sl in range(cb // LANES):
            slab = j * (cb // LANES) + sl
            rot = _rope_slab(proj[:, sl * LANES:(sl + 1) * LANES], cos_t, sin_up, sin_dn) * (B_HD ** -0.5 * LOG2E)
            for p in range(2):
                hq = 2 * slab + p
                x = jnp.where(lane_half == p, rot, 0.0)
                if p != (hq // B_GROUP) % 2:
                    x = pltpu.roll(x, B_HD, axis=1)
                qx_ref[:, hq * LANES:(hq + 1) * LANES] = x.astype(qx_ref.dtype)
    for j in range(nq):
        z_ref[:, j * cb:(j + 1) * cb] = jnp.dot(
            xb, w_refs[nq + j][...], preferred_element_type=F32).astype(z_ref.dtype)
    if kv is not None:
        rows_per_seq, ns, keep = kv
        kvp = jnp.dot(xb, wkv_ref[...], preferred_element_type=F32)
        k = jnp.concatenate([_rope_slab(kvp[:, s * LANES:(s + 1) * LANES], cos_t, sin_up, sin_dn)
                             for s in range(B_KVW // LANES)], axis=1)
        v = kvp[:, B_KVW:]
        k_ref[...] = k.astype(k_ref.dtype)
        v_ref[...] = jnp.concatenate(
            [jnp.where(lane_half == j % 2, v[:, (j // 2) * LANES:(j // 2 + 1) * LANES], 1.0)
             for j in range(B_KV_HEADS)], axis=1).astype(v_ref.dtype)
        for s in range(ns):
            end = (s + 1) * rows_per_seq
            klast_ref[s] = k[end - keep:end, :]
            vlast_ref[s] = v[end - keep:end, :]


def _b_in(x2, w_in16, tables, *, layer, batch, seq, w_kv16=None):
    tokens = batch * seq
    tm, ns, tab = _row_tiling(tokens, seq)
    row = lambda i: (i, 0)
    nw = 2 * B_QW // COL_BLOCK
    in_specs = [pl.BlockSpec((tm, D_MODEL), row)] + _weight_specs(layer, D_MODEL, nw, 1)
    out_specs = [pl.BlockSpec((tm, B_QX), row), pl.BlockSpec((tm, B_QW), row)]
    out_shape = [jax.ShapeDtypeStruct((tokens, B_QX), BF16), jax.ShapeDtypeStruct((tokens, B_QW), BF16)]
    operands = [x2] + [w_in16] * nw
    kv = None
    if w_kv16 is not None:
        keep = min(WINDOW, seq)
        rows_per_seq = tm // ns
        assert keep <= rows_per_seq
        per_seq = max(1, seq // tm)
        lastb = lambda i: (i // per_seq, 0, 0)
        kv = (rows_per_seq, ns, keep)
        in_specs.append(pl.BlockSpec((D_MODEL, 2 * B_KVW), lambda i: (0, 0)))
        operands.append(w_kv16)
        out_specs += [pl.BlockSpec((tm, B_KVW), row), pl.BlockSpec((tm, B_VX), row),
                      pl.BlockSpec((ns, keep, B_KVW), lastb), pl.BlockSpec((ns, keep, B_KVW), lastb)]
        out_shape += [jax.ShapeDtypeStruct((tokens, B_KVW), BF16), jax.ShapeDtypeStruct((tokens, B_VX), BF16)]
        out_shape += [jax.ShapeDtypeStruct((batch, keep, B_KVW), F32)] * 2
    in_specs += [pl.BlockSpec((tm, LANES), tab)] * 3
    return pl.pallas_call(
        functools.partial(_b_in_kernel, tm=tm, kv=kv),
        grid=(tokens // tm,),
        in_specs=in_specs,
        out_specs=out_specs,
        out_shape=out_shape,
        compiler_params=_params(("arbitrary",)),
        name="b_in",
    )(*operands, *_tile_tables(tables, ns))


def _attn_body(sink_ref, qx_ref, z_ref, k_ref, v_ref, o_ref, *, lq, lk, nc, masked):
    step = pl.program_id(1)
    lane_half = lax.broadcasted_iota(jnp.int32, (lq, LANES), 1) >> int(math.log2(B_HD))
    kv_slabs = range(B_KVW // LANES)
    kwin, vwin, valid = [], [], []
    for ci in range(nc):
        if masked is None:
            start, seq_i = 0, ci
            valid.append(None)
        else:
            seq_i = 0
            c = step * nc + ci
            first = jnp.maximum(c - WINDOW_CHUNKS, 0)
            start = pl.multiple_of(first * CHUNK, CHUNK)
            key_chunk = first + (lax.broadcasted_iota(jnp.int32, (lq, lk), 1) >> int(math.log2(CHUNK)))
            valid.append((key_chunk <= c) if masked else None)
        kwin.append([k_ref[seq_i, pl.ds(start, lk), s * LANES:(s + 1) * LANES] for s in kv_slabs])
        vwin.append([v_ref[seq_i, pl.ds(start, lk), j * LANES:(j + 1) * LANES] for j in range(B_KV_HEADS)])
    units = [(ci, j) for ci in range(nc) for j in range(B_KV_HEADS)]
    qstack = [jnp.concatenate([qx_ref[ci * lq:(ci + 1) * lq, (B_GROUP * j + g) * LANES:(B_GROUP * j + g + 1) * LANES]
                               for g in range(B_GROUP)], axis=0) for ci, j in units]
    scores = [_mm_nt(qstack[u], kwin[ci][j // 2]) for u, (ci, j) in enumerate(units)]
    heads = [(u, ci, j, g) for u, (ci, j) in enumerate(units) for g in range(B_GROUP)]
    sk = [sink_ref[B_GROUP * j + g] * LOG2E for _, _, j, g in heads]
    sc = [scores[u][g * lq:(g + 1) * lq] for u, _, _, g in heads]
    sc = [s if valid[ci] is None else jnp.where(valid[ci], s, -jnp.inf) for s, (_, ci, _, _) in zip(sc, heads)]
    mx = [jnp.maximum(jnp.max(s, axis=-1, keepdims=True), k) for s, k in zip(sc, sk)]
    pstack = [jnp.concatenate([jnp.exp2(sc[h] - mx[h]).astype(BF16) for h in range(u * B_GROUP, (u + 1) * B_GROUP)],
                              axis=0) for u in range(len(units))]
    pv = [_mm(pstack[u], vwin[ci][j]) for u, (ci, j) in enumerate(units)]
    for ci in range(nc):
        rows = slice(ci * lq, (ci + 1) * lq)
        for slab in range(B_QW // LANES):
            pair, g = divmod(slab, B_GROUP)
            h_lo = (ci * B_KV_HEADS + 2 * pair) * B_GROUP + g
            h_hi = (ci * B_KV_HEADS + 2 * pair + 1) * B_GROUP + g
            lo = pv[h_lo // B_GROUP][g * lq:(g + 1) * lq]
            hi = pv[h_hi // B_GROUP][g * lq:(g + 1) * lq]
            sums = pltpu.roll(jnp.where(lane_half == 0, hi, lo), B_HD, axis=1)
            sink = jnp.where(lane_half == 0, jnp.exp2(sk[h_lo] - mx[h_lo]), jnp.exp2(sk[h_hi] - mx[h_hi]))
            both = jnp.where(lane_half == 0, lo, hi) / (sums + sink)
            zs = z_ref[rows, slab * LANES:(slab + 1) * LANES].astype(F32)
            o_ref[rows, slab * LANES:(slab + 1) * LANES] = (both * _silu(zs)).astype(o_ref.dtype)


def _attn_kernel(sink_ref, qx_ref, z_ref, k_ref, v_ref, o_ref, *, lq, lk, nc, banded):
    body = functools.partial(_attn_body, sink_ref, qx_ref, z_ref, k_ref, v_ref, o_ref, lq=lq, lk=lk, nc=nc)
    if not banded:
        body(masked=None)
        return
    step = pl.program_id(1)
    clamped_steps = -(-WINDOW_CHUNKS // nc)

    @pl.when(step < clamped_steps)
    def _():
        body(masked=True)

    @pl.when(step >= clamped_steps)
    def _():
        body(masked=False)


def _attention(qx, z, k3, v3, sinks, *, batch, seq, banded):
    lq = min(CHUNK, seq)
    ltot = k3.shape[1]
    if banded:
        nc, nseq = min(ATTN_CHUNKS, seq // lq), 1
        lk = (WINDOW_CHUNKS + 1) * CHUNK
    else:
        assert seq == lq
        nc = nseq = min(ATTN_CHUNKS, batch)
        lk = ltot
    nq = seq * nseq // (lq * nc)
    tokens = batch * seq
    row = lambda b, c: (b * nq + c, 0)
    whole = lambda b, c: (b, 0, 0)
    return pl.pallas_call(
        functools.partial(_attn_kernel, lq=lq, lk=lk, nc=nc, banded=banded),
        grid=(batch // nseq, nq),
        in_specs=[
            pl.BlockSpec(memory_space=pltpu.SMEM),
            pl.BlockSpec((nc * lq, B_QX), row),
            pl.BlockSpec((nc * lq, B_QW), row),
            pl.BlockSpec((nseq, ltot, B_KVW), whole),
            pl.BlockSpec((nseq, ltot, B_VX), whole),
        ],
        out_specs=pl.BlockSpec((nc * lq, B_QW), row),
        out_shape=jax.ShapeDtypeStruct((tokens, B_QW), BF16),
        compiler_params=_params(("arbitrary", "arbitrary")),
        name="swa_attention",
    )(sinks, qx, z, k3, v3)


def _rope_tables(pos):
    half = ROPE_DIMS // 2
    inv = ROPE_THETA ** (-jnp.arange(half, dtype=F32) * 2.0 / ROPE_DIMS)
    ang = pos.astype(F32)[:, None] * inv[None, :]
    cos, sin = jnp.cos(ang), jnp.sin(ang)
    ones = jnp.ones((pos.shape[0], B_HD - ROPE_DIMS), F32)
    zeros_h = jnp.zeros((pos.shape[0], half), F32)
    zeros_r = jnp.zeros((pos.shape[0], B_HD - ROPE_DIMS), F32)
    cos_head = jnp.concatenate([cos, cos, ones], axis=1)
    up_head = jnp.concatenate([-sin, zeros_h, zeros_r], axis=1)
    dn_head = jnp.concatenate([zeros_h, sin, zeros_r], axis=1)
    rep = LANES // B_HD
    return tuple(jnp.tile(t, (1, rep)) for t in (cos_head, up_head, dn_head))


def _trunk(x, pos, conv_state, delta_state, past_k, past_v, wts):
    batch, seq, _ = x.shape
    tokens = batch * seq
    h = x.reshape(tokens, D_MODEL)
    tables = _rope_tables(pos)
    new_conv, new_delta = [], []
    for i in range(N_A_LAYERS):
        q, k, v, z, gb, cbuf = _a_in(h, wts["a_w_in"], wts["a_w_gate"][i], wts["a_conv_w"][i], conv_state[i],
                                     wts["a_log"][i], wts["a_dt"][i], layer=i, batch=batch, seq=seq)
        o, s_new = _delta(q, k, v, z, gb, delta_state, wts["a_norm_w"][i], layer=i, batch=batch, seq=seq)
        h = _out_ln(o, h, wts["a_w_out"], wts["a_ln_g"][i], wts["a_ln_b"][i], layer=i)
        new_conv.append(cbuf)
        new_delta.append(s_new)
    qx, z, k2, v2, k_last, v_last = _b_in(h, wts["b_w_in"], tables, layer=0, batch=batch, seq=seq,
                                          w_kv16=wts["b_w_kv"])
    k3 = k2.reshape(batch, seq, B_KVW)
    v3 = v2.reshape(batch, seq, B_VX)
    cached = past_k is not None
    if cached:
        pk = past_k.reshape(batch, -1, B_KVW)
        pv = past_v.reshape(batch, -1, B_KVW)
        k3 = jnp.concatenate([pk.astype(BF16), k3], axis=1)
        pv4 = pv.astype(BF16).reshape(batch, -1, B_KV_HEADS, B_HD)
        fill = jnp.ones_like(pv4)
        pvx = jnp.stack([jnp.concatenate([pv4[:, :, j], fill[:, :, j]] if j % 2 == 0 else
                                         [fill[:, :, j], pv4[:, :, j]], axis=-1) for j in range(B_KV_HEADS)], axis=2)
        v3 = jnp.concatenate([pvx.reshape(batch, -1, B_VX), v3], axis=1)
        k_last = jnp.concatenate([pk, k_last], axis=1)[:, -WINDOW:]
        v_last = jnp.concatenate([pv, v_last], axis=1)[:, -WINDOW:]
    new_k = k_last.reshape(batch, WINDOW, B_KV_HEADS, B_HD)
    new_v = v_last.reshape(batch, WINDOW, B_KV_HEADS, B_HD)
    for j in range(N_B_LAYERS):
        if j > 0:
            qx, z = _b_in(h, wts["b_w_in"], tables, layer=j, batch=batch, seq=seq)
        o = _attention(qx, z, k3, v3, wts["b_sinks"][j], batch=batch, seq=seq, banded=not cached)
        h = _out_ln(o, h, wts["b_w_out"], wts["b_ln_g"][j], wts["b_ln_b"][j], layer=j)
    return h.reshape(batch, seq, D_MODEL), jnp.stack(new_conv), jnp.stack(new_delta), new_k, new_v


def _paired_heads(w, axis):
    order = [B_GROUP * (2 * pair + odd) + g
             for pair in range(B_KV_HEADS // 2) for g in range(B_GROUP) for odd in range(2)]
    shape = w.shape
    blocks = w.reshape(shape[:axis] + (B_Q_HEADS, B_HD) + shape[axis + 1:])
    return jnp.take(blocks, jnp.array(order, jnp.int32), axis=axis).reshape(shape)


def kernel(x_prompt, x_sample, state_delta, state_conv, cache_k, cache_v, a_w_in, a_conv_w, a_log, a_dt_bias,
           a_norm_w, a_w_out, a_ln_g, a_ln_b, b_w_kv, b_w_in, b_sinks, b_w_out, b_ln_g, b_ln_b):
    a_w_in16 = a_w_in.astype(BF16)
    zeros_h = jnp.zeros((N_A_LAYERS, A_HEADS), F32)
    wts = {
        "a_w_in": a_w_in16,
        "a_w_gate": jnp.pad(a_w_in16[:, :, A_MAIN:], ((0, 0), (0, 0), (0, LANES - 2 * A_HEADS))),
        "a_conv_w": a_conv_w,
        "a_log": jnp.concatenate([a_log.astype(F32), zeros_h], axis=1)[:, :, None],
        "a_dt": jnp.concatenate([a_dt_bias.astype(F32), zeros_h], axis=1)[:, :, None],
        "a_norm_w": a_norm_w.reshape(N_A_LAYERS, 1, A_DV),
        "a_w_out": a_w_out.astype(BF16),
        "a_ln_g": a_ln_g.reshape(N_A_LAYERS, 1, D_MODEL),
        "a_ln_b": a_ln_b.reshape(N_A_LAYERS, 1, D_MODEL),
        "b_w_kv": b_w_kv.astype(BF16),
        "b_w_in": jnp.concatenate([b_w_in[:, :, :B_QW], _paired_heads(b_w_in[:, :, B_QW:], axis=2)],
                                  axis=2).astype(BF16),
        "b_sinks": b_sinks,
        "b_w_out": _paired_heads(b_w_out, axis=1).astype(BF16),
        "b_ln_g": b_ln_g.reshape(N_B_LAYERS, 1, D_MODEL),
        "b_ln_b": b_ln_b.reshape(N_B_LAYERS, 1, D_MODEL),
    }
    bp, lp, _ = x_prompt.shape
    bs, ls, _ = x_sample.shape
    pos_prompt = jnp.arange(lp, dtype=jnp.int32)
    pos_sample = PAST_LEN + jnp.arange(ls, dtype=jnp.int32)
    zero_conv = jnp.zeros((N_A_LAYERS, bp, CONV_W - 1, CONV_DIM), F32)
    zero_delta = jnp.zeros((N_A_LAYERS, bp, A_HEADS, A_DK, A_DV), F32)
    y_p, p_conv, p_delta, p_k, p_v = _trunk(x_prompt, pos_prompt, zero_conv, zero_delta, None, None, wts)
    y_s, s_conv, s_delta, s_k, s_v = _trunk(x_sample, pos_sample, state_conv, state_delta, cache_k, cache_v, wts)
    return (y_p, y_s, p_delta, p_conv, p_k, p_v, s_delta, s_conv, s_k, s_v)
```
